```python
import math
import jax
import jax.numpy as jnp
from jax import lax
import numpy as np

D_MODEL = 1024
BATCH = 2
SEQ = 8192
DEPTH = 2

GRID_W = 64
CTX_LEN = 256
EPS = 1e-6
ROPE_BASE = 10000.0
Q_BLOCK = 128

DIFF_HEADS = 4
DIFF_DIM = 64
DIFF_VDIM = 2 * DIFF_DIM
MLA_HEADS = 4
MLA_Q_RANK = 256
MLA_KV_RANK = 128
MLA_NOPE = 64
MLA_ROPE = 32
MLA_QK = MLA_NOPE + MLA_ROPE
MLA_VDIM = 128
EV_QA = DIFF_HEADS * 2 * DIFF_DIM
EV_VA = DIFF_HEADS * DIFF_VDIM
EV_SPLITS = (EV_QA, 2 * EV_QA, 2 * EV_QA + EV_VA, 2 * EV_QA + EV_VA + MLA_Q_RANK,
             2 * EV_QA + EV_VA + MLA_Q_RANK + MLA_KV_RANK)
EV_IN = EV_SPLITS[-1] + MLA_ROPE
EV_MIX = EV_VA + MLA_HEADS * MLA_VDIM

DN_HEADS = 8
DN_DK = 128
DN_DV = 128
DN_CONV = 5
DN_CHUNK = 64
DN_QKV = 2 * DN_HEADS * DN_DK + DN_HEADS * DN_DV
DN_V = DN_HEADS * DN_DV
DN_IN = DN_QKV + DN_V + 4 * DN_HEADS
DN_MIX = DN_V

PEER_HEADS = 8
PEER_NKEYS = 128
PEER_N = PEER_NKEYS * PEER_NKEYS
PEER_DKEY = 128
PEER_TOPK = 16
PEER_BLOCK = 128

kernel_name = 'hybrid_diffusion_trunk'


def _rms(x, g=None):
    xf = x.astype(jnp.float32)
    y = xf * lax.rsqrt(jnp.mean(xf * xf, axis=-1, keepdims=True) + EPS)
    if g is not None:
        y = y * g.astype(jnp.float32)
    return y.astype(x.dtype)


def _l2n(x):
    xf = x.astype(jnp.float32)
    return (xf * lax.rsqrt(jnp.sum(xf * xf, axis=-1, keepdims=True) + EPS)).astype(x.dtype)


def _axial_rope(rows, rot_dim):
    r = jnp.broadcast_to(jnp.arange(rows, dtype=jnp.float32)[:, None], (rows, GRID_W)).reshape(-1)
    cl = jnp.broadcast_to(jnp.arange(GRID_W, dtype=jnp.float32)[None, :], (rows, GRID_W)).reshape(-1)
    nf = rot_dim // 4
    inv = ROPE_BASE ** (-jnp.arange(nf, dtype=jnp.float32) / nf)
    ang = jnp.concatenate([r[:, None] * inv, cl[:, None] * inv], axis=-1)
    return jnp.cos(ang), jnp.sin(ang)


def _rope(x, cs):
    cos, sin = cs
    h = x.shape[-1] // 2
    x1 = x[..., :h].astype(jnp.float32)
    x2 = x[..., h:].astype(jnp.float32)
    return jnp.concatenate([x1 * cos - x2 * sin, x1 * sin + x2 * cos], axis=-1).astype(x.dtype)


def _attend(q, k, v):
    b, h, s, d = q.shape
    nb = s // Q_BLOCK
    qb = jnp.moveaxis(q.reshape(b, h, nb, Q_BLOCK, d), 2, 0)
    scale = d ** -0.5

    def blk(qi):
        sc = jnp.einsum('bhqd,bhkd->bhqk', qi, k).astype(jnp.float32) * scale
        p = jax.nn.softmax(sc, axis=-1)
        return jnp.einsum('bhqk,bhkd->bhqd', p.astype(v.dtype), v)

    out = lax.map(blk, qb)
    return jnp.moveaxis(out, 0, 2).reshape(b, h, s, v.shape[-1])


def _even_heads(h, w_in, q_norm, k_norm, cq_norm, ckv_norm, w_uq, w_ukv, mq_norm, mk_norm, rope_a, rope_b):
    B, S, _ = h.shape
    qa, ka, va, cq, ckv, kr = jnp.split(h @ w_in, EV_SPLITS, axis=-1)
    qa = _rms(qa.reshape(B, S, DIFF_HEADS, 2, DIFF_DIM), q_norm).transpose(0, 2, 3, 1, 4)
    ka = _rms(ka.reshape(B, S, DIFF_HEADS, 2, DIFF_DIM), k_norm).transpose(0, 2, 3, 1, 4)
    va = va.reshape(B, S, DIFF_HEADS, DIFF_VDIM).transpose(0, 2, 1, 3)
    qb = (_rms(cq, cq_norm) @ w_uq).reshape(B, S, MLA_HEADS, MLA_QK)
    kvb = (_rms(ckv, ckv_norm) @ w_ukv).reshape(B, S, MLA_HEADS, MLA_NOPE + MLA_VDIM)
    kb = jnp.concatenate([kvb[..., :MLA_NOPE],
                          jnp.broadcast_to(kr[:, :, None, :], (B, S, MLA_HEADS, MLA_ROPE))], axis=-1)
    vb = kvb[..., MLA_NOPE:].transpose(0, 2, 1, 3)
    qb = _rms(qb, mq_norm).transpose(0, 2, 1, 3)
    kb = _rms(kb, mk_norm).transpose(0, 2, 1, 3)
    if rope_a is not None:
        qa = _rope(qa, rope_a)
        ka = _rope(ka, rope_a)
        qb = jnp.concatenate([qb[..., :MLA_NOPE], _rope(qb[..., MLA_NOPE:], rope_b)], axis=-1)
        kb = jnp.concatenate([kb[..., :MLA_NOPE], _rope(kb[..., MLA_NOPE:], rope_b)], axis=-1)
    return qa, ka, va, qb, kb, vb


def _even_out(qa, qb, k1, k2, va, kb, vb, lam, lam_init, subln, w_out):
    o1 = _attend(qa[:, :, 0], k1, va)
    o2 = _attend(qa[:, :, 1], k2, va)
    oa = _rms(o1 - lam.astype(o1.dtype) * o2, subln) * (1.0 - lam_init)
    ob = _attend(qb, kb, vb)
    o = jnp.concatenate([oa, ob], axis=1)
    B, _, S, _ = o.shape
    return o.transpose(0, 2, 1, 3).reshape(B, S, EV_MIX) @ w_out


def _even_mixer(hx, hc, w_in, w_out, q_norm, k_norm, lq1, lk1, lq2, lk2, subln,
                cq_norm, ckv_norm, w_uq, w_ukv, mq_norm, mk_norm, lam_init, rope_a, rope_b, with_ctx):
    qa_x, ka_x, va_x, qb_x, kb_x, vb_x = _even_heads(hx, w_in, q_norm, k_norm, cq_norm, ckv_norm,
                                                     w_uq, w_ukv, mq_norm, mk_norm, rope_a, rope_b)
    qa_c, ka_c, va_c, qb_c, kb_c, vb_c = _even_heads(hc, w_in, q_norm, k_norm, cq_norm, ckv_norm,
                                                     w_uq, w_ukv, mq_norm, mk_norm, None, None)
    lam = (jnp.exp(jnp.sum(lq1.astype(jnp.float32) * lk1.astype(jnp.float32)))
           - jnp.exp(jnp.sum(lq2.astype(jnp.float32) * lk2.astype(jnp.float32))) + lam_init)
    cat = lambda a, b: jnp.concatenate([a, b], axis=2)
    ox = _even_out(qa_x, qb_x, cat(ka_c[:, :, 0], ka_x[:, :, 0]), cat(ka_c[:, :, 1], ka_x[:, :, 1]),
                   cat(va_c, va_x), cat(kb_c, kb_x), cat(vb_c, vb_x), lam, lam_init, subln, w_out)
    oc = None
    if with_ctx:
        oc = _even_out(qa_c, qb_c, ka_c[:, :, 0], ka_c[:, :, 1], va_c, kb_c, vb_c, lam, lam_init, subln, w_out)
    return ox, oc


def _dwconv(x, w):
    return lax.conv_general_dilated(x, w[:, None, :].astype(x.dtype), (1,), [(DN_CONV // 2, DN_CONV // 2)],
                                    dimension_numbers=('NWC', 'WIO', 'NWC'), feature_group_count=x.shape[-1])


def _dn_project(h, w_in, conv_w, a_log, dt_bias):
    B, S, _ = h.shape
    z = h @ w_in
    qkv = jax.nn.silu(_dwconv(z[..., :DN_QKV], conv_w))
    nq = DN_HEADS * DN_DK
    q = _l2n(qkv[..., :nq].reshape(B, S, DN_HEADS, DN_DK)) * (DN_DK ** -0.5)
    k = _l2n(qkv[..., nq:2 * nq].reshape(B, S, DN_HEADS, DN_DK))
    v = qkv[..., 2 * nq:].reshape(B, S, DN_HEADS, DN_DV)
    zg = z[..., DN_QKV:DN_QKV + DN_V]
    ab = z[..., DN_QKV + DN_V:].astype(jnp.float32).reshape(B, S, 2, 2, DN_HEADS)
    g = -jnp.exp(a_log) * jax.nn.softplus(ab[:, :, 0] + dt_bias)
    beta = jax.nn.sigmoid(ab[:, :, 1])
    hd = lambda t: t.transpose(0, 2, 1, 3)
    return hd(q), hd(k), hd(v), zg, g.transpose(2, 0, 3, 1), beta.transpose(2, 0, 3, 1)


def _gated_delta_chunked(q, k, v, g, beta, s0):
    B, H, S, _ = q.shape
    dv = v.shape[-1]
    C = DN_CHUNK
    n = S // C
    q, k, v = (t.astype(jnp.float32).reshape(B, H, n, C, -1) for t in (q, k, v))
    gc = jnp.cumsum(g.astype(jnp.float32).reshape(B, H, n, C), axis=-1)
    beta = beta.astype(jnp.float32).reshape(B, H, n, C)
    idx = jnp.arange(C)
    lower = idx[:, None] >= idx[None, :]
    strict = idx[:, None] > idx[None, :]
    decay = jnp.exp(jnp.where(lower, gc[..., :, None] - gc[..., None, :], -jnp.inf))
    kb = k * beta[..., None]
    L = jnp.where(strict, jnp.einsum('bhnid,bhnjd->bhnij', kb, k) * decay, 0.0)
    rhs = jnp.concatenate([v * beta[..., None], kb * jnp.exp(gc)[..., None]], axis=-1)
    sol = lax.linalg.triangular_solve(jnp.eye(C, dtype=jnp.float32) + L, rhs,
                                      left_side=True, lower=True, unit_diagonal=True)
    u, w = sol[..., :dv], sol[..., dv:]
    a_qk = jnp.where(lower, jnp.einsum('bhnid,bhnjd->bhnij', q, k) * decay, 0.0)
    q_dec = q * jnp.exp(gc)[..., None]
    k_dec = k * jnp.exp(gc[..., -1:] - gc)[..., None]
    g_last = jnp.exp(gc[..., -1])

    def step(s, xs):
        qd, kd, uu, ww, aa, gl = xs
        v_new = uu - jnp.einsum('bhcd,bhde->bhce', ww, s)
        o = jnp.einsum('bhcd,bhde->bhce', qd, s) + jnp.einsum('bhij,bhje->bhie', aa, v_new)
        s = s * gl[..., None, None] + jnp.einsum('bhcd,bhce->bhde', kd, v_new)
        return s, o

    xs = tuple(jnp.moveaxis(t, 2, 0) for t in (q_dec, k_dec, u, w, a_qk, g_last))
    s_fin, o = lax.scan(step, s0.astype(jnp.float32), xs)
    return jnp.moveaxis(o, 0, 2).reshape(B, H, S, dv), s_fin


def _dn_direction(q, k, v, g, beta, s0, reverse):
    if reverse:
        q, k, v, g, beta = (jnp.flip(t, axis=2) for t in (q, k, v, g, beta))
    o, s = _gated_delta_chunked(q, k, v, g, beta, s0)
    if reverse:
        o = jnp.flip(o, axis=2)
    return o, s


def _dn_out(o, zg, o_norm, w_out):
    B, H, S, dv = o.shape
    y = _rms(o.transpose(0, 2, 1, 3), o_norm) * jax.nn.silu(zg.reshape(B, S, H, dv).astype(jnp.float32))
    return y.reshape(B, S, H * dv).astype(zg.dtype) @ w_out


def _odd_mixer(hx, hc, w_in, conv_w, a_log, dt_bias, o_norm, w_out, with_ctx):
    qx, kx, vx, zx, gx, bx = _dn_project(hx, w_in, conv_w, a_log, dt_bias)
    qc, kc, vc, zc, gcx, bc = _dn_project(hc, w_in, conv_w, a_log, dt_bias)
    s0 = jnp.zeros((hx.shape[0], DN_HEADS, DN_DK, DN_DV), jnp.float32)
    ox, oc = [], []
    for d, rev in enumerate((False, True)):
        o_c, s_c = _dn_direction(qc, kc, vc, gcx[d], bc[d], s0, rev)
        o_x, _ = _dn_direction(qx, kx, vx, gx[d], bx[d], s_c, rev)
        ox.append(o_x)
        oc.append(o_c)
    out_x = _dn_out(ox[0] + ox[1], zx, o_norm, w_out)
    out_c = _dn_out(oc[0] + oc[1], zc, o_norm, w_out) if with_ctx else None
    return out_x, out_c


def _peer(h, wq, k1, k2, u_tab, v_tab):
    B, S, D = h.shape
    t = h.reshape(-1, D)
    nb = t.shape[0] // PEER_BLOCK

    def blk(tb):
        qh = (tb @ wq).reshape(PEER_BLOCK, PEER_HEADS, 2, PEER_DKEY)
        s1 = jnp.einsum('thd,kd->thk', qh[:, :, 0], k1).astype(jnp.float32)
        s2 = jnp.einsum('thd,kd->thk', qh[:, :, 1], k2).astype(jnp.float32)
        v1, i1 = lax.top_k(s1, PEER_TOPK)
        v2, i2 = lax.top_k(s2, PEER_TOPK)
        cand = (v1[..., :, None] + v2[..., None, :]).reshape(PEER_BLOCK, PEER_HEADS, PEER_TOPK * PEER_TOPK)
        cidx = (i1[..., :, None] * PEER_NKEYS + i2[..., None, :]).reshape(PEER_BLOCK, PEER_HEADS, -1)
        sc, j = lax.top_k(cand, PEER_TOPK)
        e = jnp.take_along_axis(cidx, j, axis=-1)
        gate = jax.nn.softmax(sc, axis=-1)
        act = jax.nn.gelu(jnp.einsum('thkd,td->thk', u_tab[e], tb).astype(jnp.float32), approximate=False)
        return jnp.einsum('thk,thkd->td', (gate * act).astype(tb.dtype), v_tab[e])

    return lax.map(blk, t.reshape(nb, PEER_BLOCK, D)).reshape(B, S, D)


def setup_inputs(seed: int = 0) -> dict:
    key = jax.random.key(seed)
    ks = iter(jax.random.split(key, 40))
    f32 = jnp.float32
    D = D_MODEL
    ne, no = (DEPTH + 1) // 2, DEPTH // 2

    def nrm(shape, scale):
        return jax.random.normal(next(ks), shape, f32) * scale

    def gain(shape):
        return 1.0 + nrm(shape, 0.02)

    dt = jnp.exp(jax.random.uniform(next(ks), (no, 2, DN_HEADS), f32, math.log(1e-3), math.log(1e-1)))
    dt_bias = dt + jnp.log(-jnp.expm1(-dt))
    a_log = jnp.log(jax.random.uniform(next(ks), (no, 2, DN_HEADS), f32, 1.0, 16.0))
    return {
        'x': nrm((BATCH, SEQ, D), 1.0),
        'c': nrm((BATCH, D), 1.0),
        'ctx': nrm((BATCH, CTX_LEN, D), 1.0),
        'c_ctx': nrm((D,), 1.0),
        'ada_w': nrm((DEPTH, D, 6 * D), 0.5 * D ** -0.5),
        'ada_b': nrm((DEPTH, 6 * D), 0.02),
        'ev_w_in': nrm((ne, D, EV_IN), D ** -0.5),
        'ev_w_out': nrm((ne, EV_MIX, D), EV_MIX ** -0.5),
        'diff_q_norm': gain((ne, DIFF_DIM)),
        'diff_k_norm': gain((ne, DIFF_DIM)),
        'diff_lam_q1': nrm((ne, DIFF_DIM), 0.1),
        'diff_lam_k1': nrm((ne, DIFF_DIM), 0.1),
        'diff_lam_q2': nrm((ne, DIFF_DIM), 0.1),
        'diff_lam_k2': nrm((ne, DIFF_DIM), 0.1),
        'diff_subln': gain((ne, DIFF_VDIM)),
        'mla_cq_norm': gain((ne, MLA_Q_RANK)),
        'mla_ckv_norm': gain((ne, MLA_KV_RANK)),
        'mla_w_uq': nrm((ne, MLA_Q_RANK, MLA_HEADS * MLA_QK), MLA_Q_RANK ** -0.5),
        'mla_w_ukv': nrm((ne, MLA_KV_RANK, MLA_HEADS * (MLA_NOPE + MLA_VDIM)), MLA_KV_RANK ** -0.5),
        'mla_q_norm': gain((ne, MLA_QK)),
        'mla_k_norm': gain((ne, MLA_QK)),
        'dn_w_in': nrm((no, D, DN_IN), D ** -0.5),
        'dn_conv': nrm((no, DN_CONV, DN_QKV), DN_CONV ** -0.5),
        'dn_a_log': a_log,
        'dn_dt_bias': dt_bias,
        'dn_o_norm': gain((no, DN_DV)),
        'dn_w_out': nrm((no, DN_MIX, D), DN_MIX ** -0.5),
        'peer_wq': nrm((DEPTH, D, PEER_HEADS * 2 * PEER_DKEY), D ** -0.5),
        'peer_k1': nrm((DEPTH, PEER_NKEYS, PEER_DKEY), PEER_DKEY ** -0.5),
        'peer_k2': nrm((DEPTH, PEER_NKEYS, PEER_DKEY), PEER_DKEY ** -0.5),
        'peer_u': nrm((DEPTH, PEER_N, D), D ** -0.5),
        'peer_v': nrm((DEPTH, PEER_N, D), 1.0),
    }


def reference(x, c, ctx, c_ctx, ada_w, ada_b, ev_w_in, ev_w_out, diff_q_norm, diff_k_norm,
              diff_lam_q1, diff_lam_k1, diff_lam_q2, diff_lam_k2, diff_subln, mla_cq_norm, mla_ckv_norm,
              mla_w_uq, mla_w_ukv, mla_q_norm, mla_k_norm, dn_w_in, dn_conv, dn_a_log, dn_dt_bias,
              dn_o_norm, dn_w_out, peer_wq, peer_k1, peer_k2, peer_u, peer_v):
    rows = x.shape[1] // GRID_W
    rope_a = _axial_rope(rows, DIFF_DIM)
    rope_b = _axial_rope(rows, MLA_ROPE)
    for l in range(DEPTH):
        last = l == DEPTH - 1
        i = l // 2
        mx = jnp.split((jax.nn.silu(c) @ ada_w[l] + ada_b[l])[:, None, :], 6, axis=-1)
        mc = jnp.split(jax.nn.silu(c_ctx) @ ada_w[l] + ada_b[l], 6, axis=-1)
        hx = _rms(x) * (1.0 + mx[1]) + mx[0]
        hc = _rms(ctx) * (1.0 + mc[1]) + mc[0]
        if l % 2 == 0:
            lam_init = 0.8 - 0.6 * math.exp(-0.3 * l)
            ox, oc = _even_mixer(hx, hc, ev_w_in[i], ev_w_out[i], diff_q_norm[i], diff_k_norm[i],
                                 diff_lam_q1[i], diff_lam_k1[i], diff_lam_q2[i], diff_lam_k2[i], diff_subln[i],
                                 mla_cq_norm[i], mla_ckv_norm[i], mla_w_uq[i], mla_w_ukv[i],
                                 mla_q_norm[i], mla_k_norm[i], lam_init, rope_a, rope_b, not last)
        else:
            ox, oc = _odd_mixer(hx, hc, dn_w_in[i], dn_conv[i], dn_a_log[i], dn_dt_bias[i],
                                dn_o_norm[i], dn_w_out[i], not last)
        x = x + mx[2] * ox
        x = x + mx[5] * _peer(_rms(x) * (1.0 + mx[4]) + mx[3], peer_wq[l], peer_k1[l], peer_k2[l],
                              peer_u[l], peer_v[l])
        if not last:
            ctx = ctx + mc[2] * oc
            ctx = ctx + mc[5] * _peer(_rms(ctx) * (1.0 + mc[4]) + mc[3], peer_wq[l], peer_k1[l],
                                      peer_k2[l], peer_u[l], peer_v[l])
    return x
```

```python
import functools
import math

import jax
import jax.numpy as jnp
import numpy as np
from jax import lax
from jax.experimental import pallas as pl
from jax.experimental.pallas import tpu as pltpu

F32 = jnp.float32
BF16 = jnp.bfloat16
HI = lax.Precision.HIGHEST
NT = (((1,), (1,)), ((), ()))

EPS = 1e-6
ROPE_BASE = 10000.0
GRID_W = 64
LANES = 128
TOKEN_BLOCK = 256
VMEM_LIMIT = 56 * 1024 * 1024

DIFF_HEADS, DIFF_DIM = 4, 64
MLA_HEADS, MLA_Q_RANK, MLA_KV_RANK, MLA_NOPE, MLA_ROPE, MLA_VDIM = 4, 256, 128, 64, 32, 128
MLA_QK = MLA_NOPE + MLA_ROPE
HEAD_W = DIFF_HEADS * LANES
DN_HEADS, DN_DK, DN_CONV, DN_CHUNK = 8, 128, 5, 64
DN_QKV = 3 * DN_HEADS * DN_DK
PEER_HEADS, PEER_NKEYS, PEER_TOPK = 8, 128, 16
PEER_TOKEN_BLOCK = 512
PEER_EXPERT_BLOCK = 256


def _cparams(sem):
    return pltpu.CompilerParams(dimension_semantics=sem, vmem_limit_bytes=VMEM_LIMIT)


def _rms_rows(x):
    return x * lax.rsqrt(jnp.mean(x * x, axis=-1, keepdims=True) + EPS)


def _silu(x):
    return x * jax.nn.sigmoid(x)


def _group_sum(sq, bd):
    hi = sq.astype(BF16)
    lo = (sq - hi.astype(F32)).astype(BF16)
    return jnp.dot(hi, bd, preferred_element_type=F32) + jnp.dot(lo, bd, preferred_element_type=F32)


def _adaln_kernel(c_ref, w_ref, b_ref, o_ref):
    s = _silu(c_ref[...])
    o_ref[0] = jnp.dot(s, w_ref[0], precision=HI, preferred_element_type=F32) + b_ref[0]


def _adaln(cc, ada_w, ada_b):
    depth, d, n = ada_w.shape
    tn = 1536
    return pl.pallas_call(
        _adaln_kernel,
        grid=(depth, n // tn),
        in_specs=[pl.BlockSpec((8, d), lambda l, j: (0, 0)),
                  pl.BlockSpec((1, d, tn), lambda l, j: (l, 0, j)),
                  pl.BlockSpec((1, 1, tn), lambda l, j: (l, 0, j))],
        out_specs=pl.BlockSpec((1, 8, tn), lambda l, j: (l, 0, j)),
        out_shape=jax.ShapeDtypeStruct((depth, 8, n), F32),
        compiler_params=_cparams(("arbitrary", "arbitrary")),
    )(cc, ada_w, ada_b.reshape(depth, 1, n))


def _modmm_kernel(*refs, has_y):
    if has_y:
        x_ref, y_ref, gmod_ref, mod_ref, w_ref, xo_ref, z_ref = refs
        x = x_ref[...] + gmod_ref[0, 5:6, :] * y_ref[...]
        xo_ref[...] = x
    else:
        x_ref, mod_ref, w_ref, z_ref = refs
        x = x_ref[...]
    h = _rms_rows(x) * (1.0 + mod_ref[0, 1:2, :]) + mod_ref[0, 0:1, :]
    z_ref[...] = jnp.dot(h.astype(BF16), w_ref[...], preferred_element_type=F32)


def _modmm(geo, x, mod, w, y=None, gmod=None):
    B, nblk, tm, mrow = geo
    n, d = x.shape
    nout = w.shape[1]
    tok = pl.BlockSpec((tm, d), lambda b, i: (b * nblk + i, 0))
    modspec = pl.BlockSpec((1, 6, d), lambda b, i: (mrow(b, i), 0, 0))
    wspec = pl.BlockSpec((d, nout), lambda b, i: (0, 0))
    zspec = pl.BlockSpec((tm, nout), lambda b, i: (b * nblk + i, 0))
    zshape = jax.ShapeDtypeStruct((n, nout), F32)
    if y is None:
        return pl.pallas_call(
            functools.partial(_modmm_kernel, has_y=False), grid=(B, nblk),
            in_specs=[tok, modspec, wspec], out_specs=zspec, out_shape=zshape,
            compiler_params=_cparams(("parallel", "parallel")))(x, mod, w)
    return pl.pallas_call(
        functools.partial(_modmm_kernel, has_y=True), grid=(B, nblk),
        in_specs=[tok, tok, modspec, modspec, wspec], out_specs=[tok, zspec],
        out_shape=[jax.ShapeDtypeStruct((n, d), F32), zshape],
        compiler_params=_cparams(("parallel", "parallel")))(x, y, gmod, mod, w)


def _rope_lanes(v, tab_ref, half):
    rep = v.shape[-1] // LANES
    c = jnp.concatenate([tab_ref[0]] * rep, axis=-1)
    sm = jnp.concatenate([tab_ref[1]] * rep, axis=-1)
    sp = jnp.concatenate([tab_ref[2]] * rep, axis=-1)
    n = v.shape[-1]
    return v * c + pltpu.roll(v, n - half, 1) * sm + pltpu.roll(v, half, 1) * sp


def _even_prep_kernel(z_ref, ropea_ref, ropeb_ref, gains_ref, cqn_ref, ckvn_ref, wuq_ref, wuk_ref, wuv_ref,
                      bd64_ref, bd128_ref, qa_ref, ka_ref, va_ref, qb_ref, kb_ref, vb_ref):
    W = HEAD_W
    bd64 = bd64_ref[...]
    bd128 = bd128_ref[...]

    def norm_groups(v, bd, width, gain):
        ms = _group_sum(v * v, bd) * (1.0 / width)
        return v * lax.rsqrt(ms + EPS) * gain

    qa = norm_groups(z_ref[:, 0:W], bd64, DIFF_DIM, gains_ref[0:1, :])
    ka = norm_groups(z_ref[:, W:2 * W], bd64, DIFF_DIM, gains_ref[1:2, :])
    qa_ref[...] = (_rope_lanes(qa, ropea_ref, DIFF_DIM // 2) * DIFF_DIM ** -0.5).astype(BF16)
    ka_ref[...] = _rope_lanes(ka, ropea_ref, DIFF_DIM // 2).astype(BF16)
    va_ref[...] = z_ref[:, 2 * W:3 * W].astype(BF16)

    o = 3 * W
    cq = (_rms_rows(z_ref[:, o:o + MLA_Q_RANK]) * cqn_ref[...]).astype(BF16)
    o += MLA_Q_RANK
    ckv = (_rms_rows(z_ref[:, o:o + MLA_KV_RANK]) * ckvn_ref[...]).astype(BF16)
    o += MLA_KV_RANK
    kr = z_ref[:, o:o + W]
    qb = jnp.dot(cq, wuq_ref[...], preferred_element_type=F32)
    kb = jnp.dot(ckv, wuk_ref[...], preferred_element_type=F32) + kr
    qb = norm_groups(qb, bd128, MLA_QK, gains_ref[2:3, :])
    kb = norm_groups(kb, bd128, MLA_QK, gains_ref[3:4, :])
    qb_ref[...] = (_rope_lanes(qb, ropeb_ref, MLA_ROPE // 2) * MLA_QK ** -0.5).astype(BF16)
    kb_ref[...] = _rope_lanes(kb, ropeb_ref, MLA_ROPE // 2).astype(BF16)
    vb_ref[...] = jnp.dot(ckv, wuv_ref[...], preferred_element_type=F32).astype(BF16)


def _even_prep(geo, z, ropea, ropeb, gains, cqn, ckvn, wuq, wuk, wuv, bd64, bd128):
    B, nblk, tm, _ = geo
    n = z.shape[0]
    W = HEAD_W
    full = lambda a: pl.BlockSpec(a.shape, lambda b, i: (0,) * a.ndim)
    rope = pl.BlockSpec((3, tm, LANES), lambda b, i: (0, i, 0))
    out = pl.BlockSpec((tm, W), lambda b, i: (b * nblk + i, 0))
    return pl.pallas_call(
        _even_prep_kernel, grid=(B, nblk),
        in_specs=[pl.BlockSpec((tm, z.shape[1]), lambda b, i: (b * nblk + i, 0)), rope, rope, full(gains),
                  full(cqn), full(ckvn), full(wuq), full(wuk), full(wuv), full(bd64), full(bd128)],
        out_specs=[out] * 6, out_shape=[jax.ShapeDtypeStruct((n, W), BF16)] * 6,
        compiler_params=_cparams(("parallel", "parallel")))(z, ropea, ropeb, gains, cqn, ckvn, wuq, wuk, wuv,
                                                            bd64, bd128)


def _attn_kernel(*refs, diff, tk, n_ctx_q, ctx_len, n_keys, lam_init):
    if diff:
        q_ref, k_ref, v_ref, lam_ref, subln_ref, o_ref = refs
    else:
        q_ref, k_ref, v_ref, o_ref = refs
    qi = pl.program_id(2)
    q = q_ref[...]
    tq = q.shape[0]
    if diff:
        lane = lax.broadcasted_iota(jnp.int32, q.shape, 1)
        zero = jnp.zeros_like(q)
        qs = [jnp.where(lane < DIFF_DIM, q, zero), jnp.where(lane >= DIFF_DIM, q, zero)]
    else:
        qs = [q]

    def chunk(start, size, carry):
        k = k_ref[pl.ds(start, size), :]
        v = v_ref[pl.ds(start, size), :]
        out = []
        for (m, l, acc), qq in zip(carry, qs):
            s = lax.dot_general(qq, k, NT, preferred_element_type=F32)
            mn = jnp.maximum(m, jnp.max(s, axis=-1, keepdims=True))
            alpha = jnp.exp(m - mn)
            p = jnp.exp(s - mn)
            l = alpha * l + jnp.sum(p, axis=-1, keepdims=True)
            acc = alpha * acc + jnp.dot(p.astype(BF16), v, preferred_element_type=F32)
            out.append((mn, l, acc))
        return tuple(out)

    init = tuple((jnp.full((tq, 1), -1e30, F32), jnp.zeros((tq, 1), F32), jnp.zeros((tq, LANES), F32))
                 for _ in qs)

    def finish(carry):
        outs = [acc / l for (_, l, acc) in carry]
        if diff:
            lam = (jnp.exp(jnp.sum(lam_ref[0:1, :] * lam_ref[1:2, :], axis=-1, keepdims=True))
                   - jnp.exp(jnp.sum(lam_ref[2:3, :] * lam_ref[3:4, :], axis=-1, keepdims=True)) + lam_init)
            o = _rms_rows(outs[0] - lam * outs[1]) * subln_ref[...] * (1.0 - lam_init)
        else:
            o = outs[0]
        o_ref[...] = o.astype(o_ref.dtype)

    @pl.when(qi < n_ctx_q)
    def _():
        finish(chunk(0, ctx_len, init))

    @pl.when(qi >= n_ctx_q)
    def _():
        finish(lax.fori_loop(0, n_keys // tk, lambda c, cr: chunk(pl.multiple_of(c * tk, tk), tk, cr), init))


def _attention(B, T, ctx_len, q, k, v, lam_vecs=None, subln=None, lam_init=0.0):
    diff = lam_vecs is not None
    tq = TOKEN_BLOCK
    tk = 768 if T % 768 == 0 else TOKEN_BLOCK
    nq = T // tq
    heads = q.shape[1] // LANES
    qspec = pl.BlockSpec((tq, LANES), lambda b, h, i: (b * nq + i, h))
    kvspec = pl.BlockSpec((T, LANES), lambda b, h, i: (b, h))
    in_specs = [qspec, kvspec, kvspec]
    args = [q, k, v]
    if diff:
        in_specs += [pl.BlockSpec(lam_vecs.shape, lambda b, h, i: (0, 0)),
                     pl.BlockSpec(subln.shape, lambda b, h, i: (0, 0))]
        args += [lam_vecs, subln]
    return pl.pallas_call(
        functools.partial(_attn_kernel, diff=diff, tk=tk, n_ctx_q=ctx_len // tq, ctx_len=ctx_len, n_keys=T,
                          lam_init=lam_init),
        grid=(B, heads, nq), in_specs=in_specs, out_specs=qspec,
        out_shape=jax.ShapeDtypeStruct(q.shape, BF16),
        compiler_params=_cparams(("parallel", "parallel", "arbitrary")))(*args)


def _residual_tail(x, y, mod_ref, xo_ref, hqt_ref):
    xn = x + mod_ref[0, 2:3, :] * y
    xo_ref[...] = xn
    hq = _rms_rows(xn) * (1.0 + mod_ref[0, 4:5, :]) + mod_ref[0, 3:4, :]
    hqt_ref[...] = hq.T.astype(BF16)


def _even_out_kernel(oa_ref, ob_ref, x_ref, mod_ref, wo_ref, xo_ref, hqt_ref):
    W = HEAD_W
    y = (jnp.dot(oa_ref[...], wo_ref[0:W, :], preferred_element_type=F32)
         + jnp.dot(ob_ref[...], wo_ref[W:2 * W, :], preferred_element_type=F32))
    _residual_tail(x_ref[...], y, mod_ref, xo_ref, hqt_ref)


def _dn_out_kernel(of_ref, or_ref, zg_ref, onorm_ref, x_ref, mod_ref, wo_ref, xo_ref, hqt_ref):
    parts = []
    for h in range(DN_HEADS):
        hs = slice(h * LANES, (h + 1) * LANES)
        o = of_ref[:, hs] + or_ref[:, hs]
        parts.append((_rms_rows(o) * onorm_ref[...] * _silu(zg_ref[:, hs])).astype(BF16))
    y = jnp.dot(jnp.concatenate(parts, axis=-1), wo_ref[...], preferred_element_type=F32)
    _residual_tail(x_ref[...], y, mod_ref, xo_ref, hqt_ref)


def _mixer_out(geo, kernel_fn, token_args, small_args, x, mod, wo):
    B, nblk, tm, mrow = geo
    n, d = x.shape
    tokspec = lambda a, col: pl.BlockSpec((tm, a[1]), lambda b, i: (b * nblk + i, col))
    in_specs = [tokspec((a, w), col) for (a, w, col) in token_args]
    in_specs += [pl.BlockSpec(a.shape, lambda b, i: (0,) * a.ndim) for a in small_args]
    in_specs += [pl.BlockSpec((tm, d), lambda b, i: (b * nblk + i, 0)),
                 pl.BlockSpec((1, 6, d), lambda b, i: (mrow(b, i), 0, 0)),
                 pl.BlockSpec(wo.shape, lambda b, i: (0, 0))]
    return pl.pallas_call(
        kernel_fn, grid=(B, nblk), in_specs=in_specs,
        out_specs=[pl.BlockSpec((tm, d), lambda b, i: (b * nblk + i, 0)),
                   pl.BlockSpec((d, tm), lambda b, i: (0, b * nblk + i))],
        out_shape=[jax.ShapeDtypeStruct((n, d), F32), jax.ShapeDtypeStruct((d, n), BF16)],
        compiler_params=_cparams(("parallel", "parallel")))(
            *[a for (a, _, _) in token_args], *small_args, x, mod, wo)


def _top_values(s, k):
    rows = lax.broadcasted_iota(jnp.int32, (k, s.shape[1]), 0)

    def body(r, carry):
        s, vals = carry
        m = jnp.max(s, axis=0, keepdims=True)
        vals = jnp.where(rows == r, m, vals)
        return jnp.where(s == m, -jnp.inf, s), vals

    return lax.fori_loop(0, k, body, (s, jnp.zeros((k, s.shape[1]), F32)))[1]


def _peer_select_kernel(hqt_ref, wqt_ref, k1_ref, k2_ref, a1_ref, e1_ref, s2_ref, e2_ref, tau_ref):
    K = PEER_TOPK
    qt = jnp.dot(wqt_ref[...], hqt_ref[...], preferred_element_type=F32)
    dk = k1_ref.shape[1]
    for h in range(PEER_HEADS):
        q1 = qt[(2 * h) * dk:(2 * h + 1) * dk, :].astype(BF16)
        q2 = qt[(2 * h + 1) * dk:(2 * h + 2) * dk, :].astype(BF16)
        s1 = jnp.dot(k1_ref[...], q1, preferred_element_type=F32)
        s2 = jnp.dot(k2_ref[...], q2, preferred_element_type=F32)
        v1 = _top_values(s1, K)
        v2 = _top_values(s2, K)
        row8 = lax.broadcasted_iota(jnp.int32, (8, s1.shape[1]), 0)
        pieces = [v1[0:1, :] + v2, v1[1:2, :] + v2[0:8, :]]
        pieces += [jnp.where(row8 < K // (r1 + 1), v1[r1:r1 + 1, :] + v2[0:8, :], -jnp.inf) for r1 in range(2, 8)]
        pieces.append(v1[8:16, :] + v2[0:1, :])
        cand = jnp.concatenate(pieces, axis=0)
        tau = _top_values(cand, K)[K - 1:K, :]
        m1, m2 = v1[0:1, :], v2[0:1, :]
        zsum = jnp.sum(jnp.where(cand >= tau, jnp.exp(cand - (m1 + m2)), 0.0), axis=0, keepdims=True)
        a1_ref[h] = s1
        e1_ref[h] = jnp.exp(s1 - m1) / zsum
        s2_ref[h] = s2
        e2_ref[h] = jnp.exp(s2 - m2)
        tau_ref[h:h + 1, :] = tau


def _peer_select(hqt, wqt, k1, k2):
    d, n = hqt.shape
    tm = TOKEN_BLOCK
    H, NK = PEER_HEADS, PEER_NKEYS
    big = pl.BlockSpec((H, NK, tm), lambda j: (0, 0, j))
    bigshape = jax.ShapeDtypeStruct((H, NK, n), F32)
    full = lambda a: pl.BlockSpec(a.shape, lambda j: (0,) * a.ndim)
    return pl.pallas_call(
        _peer_select_kernel, grid=(n // tm,),
        in_specs=[pl.BlockSpec((d, tm), lambda j: (0, j)), full(wqt), full(k1), full(k2)],
        out_specs=[big, big, big, big, pl.BlockSpec((H, tm), lambda j: (0, j))],
        out_shape=[bigshape] * 4 + [jax.ShapeDtypeStruct((H, n), F32)],
        compiler_params=_cparams(("parallel",)))(hqt, wqt, k1, k2)


def _gelu(a):
    return 0.5 * a * (1.0 + lax.erf(a * np.float32(math.sqrt(0.5))))


def _peer_dense_kernel(hqt_ref, u_ref, vt_ref, a1_ref, e1_ref, s2_ref, e2_ref, tau_ref, y_ref, acc_ref):
    c = pl.program_id(1)
    NK = PEER_NKEYS

    @pl.when(c == 0)
    def _():
        acc_ref[...] = jnp.zeros_like(acc_ref)

    act = _gelu(jnp.dot(u_ref[...], hqt_ref[...], preferred_element_type=F32))
    gates = []
    for ii in range(PEER_EXPERT_BLOCK // NK):
        i = c * (PEER_EXPERT_BLOCK // NK) + ii
        g = jnp.zeros((NK, act.shape[1]), F32)
        for h in range(PEER_HEADS):
            a = a1_ref[h, pl.ds(i, 1), :]
            e = e1_ref[h, pl.ds(i, 1), :]
            sel = (a + s2_ref[h]) >= tau_ref[h:h + 1, :]
            g = g + jnp.where(sel, e * e2_ref[h], 0.0)
        gates.append(g)
    wt = (jnp.concatenate(gates, axis=0) * act).astype(BF16)
    acc_ref[...] += jnp.dot(vt_ref[...], wt, preferred_element_type=F32)

    @pl.when(c == pl.num_programs(1) - 1)
    def _():
        y_ref[...] = acc_ref[...].T


def _peer_dense(hqt, u, vt, a1, e1, s2, e2, tau):
    d, n = hqt.shape
    tm, te = PEER_TOKEN_BLOCK, PEER_EXPERT_BLOCK
    H, NK = PEER_HEADS, PEER_NKEYS
    big = pl.BlockSpec((H, NK, tm), lambda j, c: (0, 0, j))
    return pl.pallas_call(
        _peer_dense_kernel, grid=(n // tm, u.shape[0] // te),
        in_specs=[pl.BlockSpec((d, tm), lambda j, c: (0, j)),
                  pl.BlockSpec((te, d), lambda j, c: (c, 0)),
                  pl.BlockSpec((d, te), lambda j, c: (0, c)),
                  big, big, big, big, pl.BlockSpec((H, tm), lambda j, c: (0, j))],
        out_specs=pl.BlockSpec((tm, d), lambda j, c: (j, 0)),
        out_shape=jax.ShapeDtypeStruct((n, d), F32),
        scratch_shapes=[pltpu.VMEM((d, tm), F32)],
        compiler_params=_cparams(("parallel", "arbitrary")))(hqt, u, vt, a1, e1, s2, e2, tau)


def _peer(hqt, wq, k1, k2, u_tab, v_tab):
    sel = _peer_select(hqt, wq.T.astype(BF16), k1.astype(BF16), k2.astype(BF16))
    return _peer_dense(hqt, u_tab.astype(BF16), v_tab.T.astype(BF16), *sel)


def _dn_prep_kernel(z_ref, prev_ref, next_ref, conv_ref, alog_ref, dtb_ref, ab_ref, q_ref, k_ref, v_ref, gb_ref,
                    ext_ref, *, n_ctx_blk, nblk):
    i = pl.program_id(1)
    tm = z_ref.shape[0]
    halo = prev_ref.shape[0]
    pad = DN_CONV // 2
    has_prev = jnp.logical_and(i != 0, i != n_ctx_blk)
    has_next = jnp.logical_and(i != n_ctx_blk - 1, i != nblk - 1)
    ext_ref[0:halo, :] = jnp.where(has_prev, prev_ref[...], 0.0)
    ext_ref[halo:halo + tm, :] = z_ref[...]
    ext_ref[halo + tm:, :] = jnp.where(has_next, next_ref[...], 0.0)
    nq = DN_HEADS * DN_DK
    for j in range(DN_QKV // LANES):
        cs = slice(j * LANES, (j + 1) * LANES)
        acc = conv_ref[0:1, cs] * ext_ref[halo - pad:halo - pad + tm, cs]
        for t in range(1, DN_CONV):
            acc = acc + conv_ref[t:t + 1, cs] * ext_ref[halo - pad + t:halo - pad + t + tm, cs]
        y = _silu(acc)
        if j * LANES < 2 * nq:
            y = y * lax.rsqrt(jnp.sum(y * y, axis=-1, keepdims=True) + EPS)
        if j * LANES < nq:
            q_ref[:, cs] = y * DN_DK ** -0.5
        elif j * LANES < 2 * nq:
            k_ref[:, slice(j * LANES - nq, (j + 1) * LANES - nq)] = y
        else:
            v_ref[:, slice(j * LANES - 2 * nq, (j + 1) * LANES - 2 * nq)] = y
    ab = ab_ref[...]
    nh = 2 * DN_HEADS
    xa = ab[:, 0:nh] + dtb_ref[...]
    softplus = jnp.maximum(xa, 0.0) + jnp.log(1.0 + jnp.exp(-jnp.abs(xa)))
    gb_ref[:, 0:nh] = -jnp.exp(alog_ref[...]) * softplus
    gb_ref[:, nh:2 * nh] = jax.nn.sigmoid(ab[:, nh:2 * nh])


def _dn_prep(geo, n_ctx_blk, z, conv_w, alog, dtb):
    B, nblk, tm, _ = geo
    n = z.shape[0]
    halo = 8
    r = tm // halo
    nh8 = n // halo
    d = DN_HEADS * DN_DK
    prev = pl.BlockSpec((halo, DN_QKV), lambda b, i: (jnp.maximum((b * nblk + i) * r - 1, 0), 0))
    nxt = pl.BlockSpec((halo, DN_QKV), lambda b, i: (jnp.minimum((b * nblk + i + 1) * r, nh8 - 1), 0))
    ab = z[:, DN_QKV + d:DN_QKV + d + 4 * DN_HEADS]
    tok = pl.BlockSpec((tm, d), lambda b, i: (b * nblk + i, 0))
    gbspec = pl.BlockSpec((tm, 4 * DN_HEADS), lambda b, i: (b * nblk + i, 0))
    full = lambda a: pl.BlockSpec(a.shape, lambda b, i: (0,) * a.ndim)
    return pl.pallas_call(
        functools.partial(_dn_prep_kernel, n_ctx_blk=n_ctx_blk, nblk=nblk), grid=(B, nblk),
        in_specs=[pl.BlockSpec((tm, DN_QKV), lambda b, i: (b * nblk + i, 0)), prev, nxt, full(conv_w),
                  full(alog), full(dtb), gbspec],
        out_specs=[tok, tok, tok, gbspec],
        out_shape=[jax.ShapeDtypeStruct((n, d), F32)] * 3 + [jax.ShapeDtypeStruct((n, 4 * DN_HEADS), F32)],
        scratch_shapes=[pltpu.VMEM((tm + 2 * halo, DN_QKV), F32)],
        compiler_params=_cparams(("parallel", "parallel")))(z, z, z, conv_w, alog, dtb, ab)


def _dn_local_kernel(q_ref, k_ref, v_ref, gb_ref, u_ref, w_ref, qd_ref, kdt_ref, a_ref, gl_ref):
    C = DN_CHUNK
    ri = lax.broadcasted_iota(jnp.int32, (C, C), 0)
    ci = lax.broadcasted_iota(jnp.int32, (C, C), 1)
    eye = (ri == ci).astype(F32)
    ri2 = lax.broadcasted_iota(jnp.int32, (LANES, LANES), 0)
    ci2 = lax.broadcasted_iota(jnp.int32, (LANES, LANES), 1)
    eye128 = (ri2 == ci2).astype(F32)
    nh = DN_HEADS
    for d in range(2):
        incl = (ri >= ci) if d == 0 else (ri <= ci)
        strict = (ri > ci) if d == 0 else (ri < ci)
        tri = incl.astype(F32)
        last = C - 1 if d == 0 else 0
        for h in range(nh):
            hs = slice(h * LANES, (h + 1) * LANES)
            col = d * nh + h
            gcol = jnp.broadcast_to(gb_ref[:, col:col + 1], (C, C))
            gc = jnp.dot(tri, gcol, precision=HI, preferred_element_type=F32)
            gct = lax.dot_general(eye, gc, NT, precision=HI, preferred_element_type=F32)
            decay = jnp.where(incl, jnp.exp(jnp.where(incl, gc - gct, 0.0)), 0.0)
            bet = gb_ref[:, 2 * nh + col:2 * nh + col + 1]
            qh, kh, vh = q_ref[:, hs], k_ref[:, hs], v_ref[:, hs]
            kbm = kh * bet
            kk = lax.dot_general(kbm, kh, NT, precision=HI, preferred_element_type=F32)
            qk = lax.dot_general(qh, kh, NT, precision=HI, preferred_element_type=F32)
            nm = jnp.where(strict, -kk * decay, 0.0)
            aqk = jnp.where(incl, qk * decay, 0.0)
            tinv = eye + nm
            p = nm
            for _ in range(int(math.log2(C)) - 1):
                p = jnp.dot(p, p, precision=HI, preferred_element_type=F32)
                tinv = tinv + jnp.dot(tinv, p, precision=HI, preferred_element_type=F32)
            eg = jnp.exp(gc[:, 0:1])
            rhs = jnp.concatenate([vh * bet, kbm * eg], axis=-1)
            sol = jnp.dot(tinv, rhs, precision=HI, preferred_element_type=F32)
            u_ref[d, :, hs] = sol[:, 0:LANES]
            w_ref[d, :, hs] = sol[:, LANES:2 * LANES]
            qd_ref[d, :, hs] = qh * eg
            gl = gc[last:last + 1, :]
            kd = kh * jnp.exp(gl[:, 0:1] - gc[:, 0:1])
            kdt_ref[d, 0, :, h * C:(h + 1) * C] = lax.dot_general(eye128, kd, NT, precision=HI,
                                                                   preferred_element_type=F32)
            a_ref[d, :, h * C:(h + 1) * C] = aqk
            gl_ref[d, 0, h:h + 1, :] = jnp.exp(jnp.concatenate([gl, gl], axis=-1))


def _dn_local(q, k, v, gb):
    n, d = q.shape
    C = DN_CHUNK
    nc = n // C
    tok = pl.BlockSpec((C, d), lambda j: (j, 0))
    out2 = pl.BlockSpec((2, C, d), lambda j: (0, j, 0))
    return pl.pallas_call(
        _dn_local_kernel, grid=(nc,),
        in_specs=[tok, tok, tok, pl.BlockSpec((C, gb.shape[1]), lambda j: (j, 0))],
        out_specs=[out2, out2, out2,
                   pl.BlockSpec((2, 1, DN_DK, DN_HEADS * C), lambda j: (0, j, 0, 0)),
                   pl.BlockSpec((2, C, DN_HEADS * C), lambda j: (0, j, 0)),
                   pl.BlockSpec((2, 1, DN_HEADS, LANES), lambda j: (0, j, 0, 0))],
        out_shape=[jax.ShapeDtypeStruct((2, n, d), F32)] * 3
        + [jax.ShapeDtypeStruct((2, nc, DN_DK, DN_HEADS * C), F32),
           jax.ShapeDtypeStruct((2, n, DN_HEADS * C), F32),
           jax.ShapeDtypeStruct((2, nc, DN_HEADS, LANES), F32)],
        compiler_params=_cparams(("parallel",)))(q, k, v, gb)


def _dn_scan_kernel(*refs):
    ins, (of_ref, or_ref, s_ref) = refs[:12], refs[12:]
    C = DN_CHUNK

    @pl.when(pl.program_id(1) == 0)
    def _():
        s_ref[...] = jnp.zeros_like(s_ref)

    for d, o_ref in enumerate((of_ref, or_ref)):
        u_ref, w_ref, qd_ref, kdt_ref, a_ref, gl_ref = ins[6 * d:6 * d + 6]
        for h in range(DN_HEADS):
            hs = slice(h * LANES, (h + 1) * LANES)
            s = s_ref[d, h]
            vn = u_ref[0, :, hs] - jnp.dot(w_ref[0, :, hs], s, precision=HI, preferred_element_type=F32)
            o_ref[:, hs] = (jnp.dot(qd_ref[0, :, hs], s, precision=HI, preferred_element_type=F32)
                            + jnp.dot(a_ref[0, :, h * C:(h + 1) * C], vn, precision=HI,
                                      preferred_element_type=F32))
            s_ref[d, h] = s * gl_ref[0, 0, h:h + 1, :] + jnp.dot(kdt_ref[0, 0, :, h * C:(h + 1) * C], vn,
                                                                  precision=HI, preferred_element_type=F32)


def _dn_scan(B, T, ctx_len, u, w, qd, kdt, a, gl):
    _, n, d = u.shape
    C = DN_CHUNK
    nch, ncc = T // C, ctx_len // C
    rpos = lambda c: jnp.where(c < ncc, ncc - 1 - c, nch - 1 - (c - ncc))
    pos = (lambda b, c: b * nch + c, lambda b, c: b * nch + rpos(c))
    in_specs, args = [], []
    for dd in range(2):
        p = pos[dd]
        in_specs += [pl.BlockSpec((1, C, d), lambda b, c, p=p, dd=dd: (dd, p(b, c), 0))] * 3
        in_specs += [pl.BlockSpec((1, 1, DN_DK, DN_HEADS * C), lambda b, c, p=p, dd=dd: (dd, p(b, c), 0, 0)),
                     pl.BlockSpec((1, C, DN_HEADS * C), lambda b, c, p=p, dd=dd: (dd, p(b, c), 0)),
                     pl.BlockSpec((1, 1, DN_HEADS, LANES), lambda b, c, p=p, dd=dd: (dd, p(b, c), 0, 0))]
        args += [u, w, qd, kdt, a, gl]
    return pl.pallas_call(
        _dn_scan_kernel, grid=(B, nch), in_specs=in_specs,
        out_specs=[pl.BlockSpec((C, d), lambda b, c: (pos[0](b, c), 0)),
                   pl.BlockSpec((C, d), lambda b, c: (pos[1](b, c), 0))],
        out_shape=[jax.ShapeDtypeStruct((n, d), F32)] * 2,
        scratch_shapes=[pltpu.VMEM((2, DN_HEADS, DN_DK, LANES), F32)],
        compiler_params=_cparams(("arbitrary", "arbitrary")))(*args)


def _final_kernel(x_ref, y_ref, mod_ref, o_ref):
    o_ref[0] = x_ref[...] + mod_ref[0, 5:6, :] * y_ref[...]


def _final(B, T, ctx_len, x, y, mod):
    n, d = x.shape
    tm = TOKEN_BLOCK
    nblk, ncb = T // tm, ctx_len // tm
    tok = pl.BlockSpec((tm, d), lambda b, i: (b * nblk + ncb + i, 0))
    return pl.pallas_call(
        _final_kernel, grid=(B, nblk - ncb),
        in_specs=[tok, tok, pl.BlockSpec((1, 6, d), lambda b, i: (b, 0, 0))],
        out_specs=pl.BlockSpec((1, tm, d), lambda b, i: (b, i, 0)),
        out_shape=jax.ShapeDtypeStruct((B, T - ctx_len, d), F32),
        compiler_params=_cparams(("parallel", "parallel")))(x, y, mod)


def _rope_tables(seq, ctx_len):
    rows = seq // GRID_W
    r = jnp.broadcast_to(jnp.arange(rows, dtype=F32)[:, None], (rows, GRID_W)).reshape(-1)
    cl = jnp.broadcast_to(jnp.arange(GRID_W, dtype=F32)[None, :], (rows, GRID_W)).reshape(-1)

    def angles(rot_dim):
        nf = rot_dim // 4
        inv = ROPE_BASE ** (-jnp.arange(nf, dtype=F32) / nf)
        ang = jnp.concatenate([r[:, None] * inv, cl[:, None] * inv], axis=-1)
        ang = jnp.concatenate([jnp.zeros((ctx_len, rot_dim // 2), F32), ang], axis=0)
        return jnp.cos(ang), jnp.sin(ang)

    T = seq + ctx_len
    ca, sa = angles(DIFF_DIM)
    z = jnp.zeros_like(sa)
    ta = jnp.stack([jnp.concatenate([ca, ca] * 2, -1), jnp.concatenate([-sa, z] * 2, -1),
                    jnp.concatenate([z, sa] * 2, -1)])
    cb, sb = angles(MLA_ROPE)
    one, zn, zt = jnp.ones((T, MLA_NOPE), F32), jnp.zeros((T, MLA_NOPE), F32), jnp.zeros((T, LANES - MLA_QK), F32)
    zb = jnp.zeros_like(sb)
    tb = jnp.stack([jnp.concatenate([one, cb, cb, 1.0 + zt], -1), jnp.concatenate([zn, -sb, zb, zt], -1),
                    jnp.concatenate([zn, zb, sb, zt], -1)])
    return ta, tb


def _pad_heads(w, heads, width):
    lead = w.shape[:-1]
    w = w.reshape(lead + (heads, width))
    return jnp.pad(w, [(0, 0)] * len(lead) + [(0, 0), (0, LANES - width)]).reshape(lead + (heads * LANES,))


def _block_diag_ones(n, group):
    idx = np.arange(n) // group
    return jnp.asarray(idx[:, None] == idx[None, :], dtype=BF16)


def kernel(x, c, ctx, c_ctx, ada_w, ada_b, ev_w_in, ev_w_out, diff_q_norm, diff_k_norm, diff_lam_q1, diff_lam_k1, diff_lam_q2, diff_lam_k2, diff_subln, mla_cq_norm, mla_ckv_norm, mla_w_uq, mla_w_ukv, mla_q_norm, mla_k_norm, dn_w_in, dn_conv, dn_a_log, dn_dt_bias, dn_o_norm, dn_w_out, peer_wq, peer_k1, peer_k2, peer_u, peer_v):
    B, S, D = x.shape
    CTX = ctx.shape[1]
    T = CTX + S
    depth = ada_w.shape[0]
    tm = TOKEN_BLOCK
    assert CTX % tm == 0 and S % tm == 0 and (B * T) % PEER_TOKEN_BLOCK == 0 and B + 1 <= 8
    nblk, ncb = T // tm, CTX // tm
    geo = (B, nblk, tm, lambda b, i: jnp.where(i < ncb, B, b))

    cc = jnp.zeros((8, D), F32).at[:B].set(c).at[B].set(c_ctx)
    mods = _adaln(cc, ada_w, ada_b).reshape(depth, 8, 6, D)
    xs = jnp.concatenate([ctx, x], axis=1).reshape(B * T, D)
    ropea, ropeb = _rope_tables(S, CTX)
    bd64, bd128 = _block_diag_ones(HEAD_W, DIFF_DIM), _block_diag_ones(HEAD_W, LANES)

    y = None
    for l in range(depth):
        i = l // 2
        mod = mods[l]
        gmod = mods[l - 1] if l else None
        if l % 2 == 0:
            lam_init = 0.8 - 0.6 * math.exp(-0.3 * l)
            w_in = ev_w_in[i]
            o = 3 * HEAD_W + MLA_Q_RANK + MLA_KV_RANK
            kr_rep = jnp.pad(jnp.broadcast_to(w_in[:, None, o:o + MLA_ROPE], (D, MLA_HEADS, MLA_ROPE)),
                             ((0, 0), (0, 0), (MLA_NOPE, LANES - MLA_QK))).reshape(D, HEAD_W)
            w_cat = jnp.concatenate([w_in[:, :o], kr_rep], axis=1).astype(BF16)
            res = _modmm(geo, xs, mod, w_cat, y, gmod)
            (xs, z) = res if y is not None else (xs, res)
            ukv = mla_w_ukv[i].reshape(MLA_KV_RANK, MLA_HEADS, MLA_NOPE + MLA_VDIM)
            gains = jnp.stack([jnp.tile(diff_q_norm[i], HEAD_W // DIFF_DIM), jnp.tile(diff_k_norm[i], HEAD_W // DIFF_DIM),
                               jnp.tile(jnp.pad(mla_q_norm[i], (0, LANES - MLA_QK)), MLA_HEADS),
                               jnp.tile(jnp.pad(mla_k_norm[i], (0, LANES - MLA_QK)), MLA_HEADS)])
            qa, ka, va, qb, kb, vb = _even_prep(
                geo, z, ropea, ropeb, gains, mla_cq_norm[i][None, :], mla_ckv_norm[i][None, :],
                _pad_heads(mla_w_uq[i], MLA_HEADS, MLA_QK).astype(BF16),
                _pad_heads(ukv[:, :, :MLA_NOPE].reshape(MLA_KV_RANK, -1), MLA_HEADS, MLA_NOPE).astype(BF16),
                ukv[:, :, MLA_NOPE:].reshape(MLA_KV_RANK, -1).astype(BF16), bd64, bd128)
            lam_vecs = jnp.stack([diff_lam_q1[i], diff_lam_k1[i], diff_lam_q2[i], diff_lam_k2[i]])
            oa = _attention(B, T, CTX, qa, ka, va, lam_vecs, diff_subln[i][None, :], lam_init)
            ob = _attention(B, T, CTX, qb, kb, vb)
            xs, hqt = _mixer_out(geo, _even_out_kernel, [(oa, HEAD_W, 0), (ob, HEAD_W, 0)], [], xs, mod,
                                 ev_w_out[i].astype(BF16))
        else:
            nin = dn_w_in.shape[2]
            w_in = jnp.pad(dn_w_in[i], ((0, 0), (0, -nin % LANES))).astype(BF16)
            res = _modmm(geo, xs, mod, w_in, y, gmod)
            (xs, z) = res if y is not None else (xs, res)
            q, k, v, gb = _dn_prep(geo, ncb, z, dn_conv[i], dn_a_log[i].reshape(1, -1),
                                   dn_dt_bias[i].reshape(1, -1))
            of, orv = _dn_scan(B, T, CTX, *_dn_local(q, k, v, gb))
            xs, hqt = _mixer_out(geo, _dn_out_kernel,
                                 [(of, D, 0), (orv, D, 0), (z, D, DN_QKV // D)], [dn_o_norm[i][None, :]], xs, mod,
                                 dn_w_out[i].astype(BF16))
        y = _peer(hqt, peer_wq[l], peer_k1[l], peer_k2[l], peer_u[l], peer_v[l])
    return _final(B, T, CTX, xs, y, mods[depth - 1])
```

```python
import functools
import math

import jax
import jax.numpy as jnp
import numpy as np
from jax import lax
from jax.experimental import pallas as pl
from jax.experimental.pallas import tpu as pltpu

F32 = jnp.float32
BF16 = jnp.bfloat16
HI = lax.Precision.HIGHEST
NT = (((1,), (1,)), ((), ()))

EPS = 1e-6
ROPE_BASE = 10000.0
GRID_W = 64
LANES = 128
TOKEN_BLOCK = 256
VMEM_LIMIT = 56 * 1024 * 1024

DIFF_HEADS, DIFF_DIM = 4, 64
MLA_HEADS, MLA_Q_RANK, MLA_KV_RANK, MLA_NOPE, MLA_ROPE, MLA_VDIM = 4, 256, 128, 64, 32, 128
MLA_QK = MLA_NOPE + MLA_ROPE
HEAD_W = DIFF_HEADS * LANES
DN_HEADS, DN_DK, DN_CONV, DN_CHUNK = 8, 128, 5, 64
DN_QKV = 3 * DN_HEADS * DN_DK
PEER_HEADS, PEER_NKEYS, PEER_TOPK = 8, 128, 16
PEER_TOKEN_BLOCK = 512
PEER_EXPERT_BLOCK = 512


def _cparams(sem):
    return pltpu.CompilerParams(dimension_semantics=sem, vmem_limit_bytes=VMEM_LIMIT)


def _rms_rows(x):
    return x * lax.rsqrt(jnp.mean(x * x, axis=-1, keepdims=True) + EPS)


def _silu(x):
    return x * jax.nn.sigmoid(x)


def _group_sum(sq, bd):
    hi = sq.astype(BF16)
    lo = (sq - hi.astype(F32)).astype(BF16)
    return jnp.dot(hi, bd, preferred_element_type=F32) + jnp.dot(lo, bd, preferred_element_type=F32)


def _adaln_kernel(c_ref, w_ref, b_ref, o_ref):
    s = _silu(c_ref[...])
    o_ref[0] = jnp.dot(s, w_ref[0], precision=HI, preferred_element_type=F32) + b_ref[0]


def _adaln(cc, ada_w, ada_b):
    depth, d, n = ada_w.shape
    tn = 1536
    return pl.pallas_call(
        _adaln_kernel,
        grid=(depth, n // tn),
        in_specs=[pl.BlockSpec((8, d), lambda l, j: (0, 0)),
                  pl.BlockSpec((1, d, tn), lambda l, j: (l, 0, j)),
                  pl.BlockSpec((1, 1, tn), lambda l, j: (l, 0, j))],
        out_specs=pl.BlockSpec((1, 8, tn), lambda l, j: (l, 0, j)),
        out_shape=jax.ShapeDtypeStruct((depth, 8, n), F32),
        compiler_params=_cparams(("arbitrary", "arbitrary")),
    )(cc, ada_w, ada_b.reshape(depth, 1, n))


def _modmm_kernel(*refs, has_y):
    if has_y:
        x_ref, y_ref, gmod_ref, mod_ref, w_ref, xo_ref, z_ref = refs
        x = x_ref[...] + gmod_ref[0, 5:6, :] * y_ref[...]
        xo_ref[...] = x
    else:
        x_ref, mod_ref, w_ref, z_ref = refs
        x = x_ref[...]
    h = _rms_rows(x) * (1.0 + mod_ref[0, 1:2, :]) + mod_ref[0, 0:1, :]
    z_ref[...] = jnp.dot(h.astype(BF16), w_ref[...], preferred_element_type=F32)


def _modmm(geo, x, mod, w, y=None, gmod=None):
    B, nblk, tm, mrow = geo
    n, d = x.shape
    nout = w.shape[1]
    tok = pl.BlockSpec((tm, d), lambda b, i: (b * nblk + i, 0))
    modspec = pl.BlockSpec((1, 6, d), lambda b, i: (mrow(b, i), 0, 0))
    wspec = pl.BlockSpec((d, nout), lambda b, i: (0, 0))
    zspec = pl.BlockSpec((tm, nout), lambda b, i: (b * nblk + i, 0))
    zshape = jax.ShapeDtypeStruct((n, nout), F32)
    if y is None:
        return pl.pallas_call(
            functools.partial(_modmm_kernel, has_y=False), grid=(B, nblk),
            in_specs=[tok, modspec, wspec], out_specs=zspec, out_shape=zshape,
            compiler_params=_cparams(("parallel", "parallel")))(x, mod, w)
    return pl.pallas_call(
        functools.partial(_modmm_kernel, has_y=True), grid=(B, nblk),
        in_specs=[tok, tok, modspec, modspec, wspec], out_specs=[tok, zspec],
        out_shape=[jax.ShapeDtypeStruct((n, d), F32), zshape],
        compiler_params=_cparams(("parallel", "parallel")))(x, y, gmod, mod, w)


def _rope_lanes(v, tab_ref, half):
    rep = v.shape[-1] // LANES
    c = jnp.concatenate([tab_ref[0]] * rep, axis=-1)
    sm = jnp.concatenate([tab_ref[1]] * rep, axis=-1)
    sp = jnp.concatenate([tab_ref[2]] * rep, axis=-1)
    n = v.shape[-1]
    return v * c + pltpu.roll(v, n - half, 1) * sm + pltpu.roll(v, half, 1) * sp


def _even_prep_kernel(z_ref, ropea_ref, ropeb_ref, gains_ref, cqn_ref, ckvn_ref, wuq_ref, wuk_ref, wuv_ref,
                      bd64_ref, bd128_ref, qa_ref, ka_ref, va_ref, qb_ref, kb_ref, vb_ref):
    W = HEAD_W
    bd64 = bd64_ref[...]
    bd128 = bd128_ref[...]

    def norm_groups(v, bd, width, gain):
        ms = _group_sum(v * v, bd) * (1.0 / width)
        return v * lax.rsqrt(ms + EPS) * gain

    qa = norm_groups(z_ref[:, 0:W], bd64, DIFF_DIM, gains_ref[0:1, :])
    ka = norm_groups(z_ref[:, W:2 * W], bd64, DIFF_DIM, gains_ref[1:2, :])
    qa_ref[...] = (_rope_lanes(qa, ropea_ref, DIFF_DIM // 2) * DIFF_DIM ** -0.5).astype(BF16)
    ka_ref[...] = _rope_lanes(ka, ropea_ref, DIFF_DIM // 2).astype(BF16)
    va_ref[...] = z_ref[:, 2 * W:3 * W].astype(BF16)

    o = 3 * W
    cq = (_rms_rows(z_ref[:, o:o + MLA_Q_RANK]) * cqn_ref[...]).astype(BF16)
    o += MLA_Q_RANK
    ckv = (_rms_rows(z_ref[:, o:o + MLA_KV_RANK]) * ckvn_ref[...]).astype(BF16)
    o += MLA_KV_RANK
    kr = z_ref[:, o:o + W]
    qb = jnp.dot(cq, wuq_ref[...], preferred_element_type=F32)
    kb = jnp.dot(ckv, wuk_ref[...], preferred_element_type=F32) + kr
    qb = norm_groups(qb, bd128, MLA_QK, gains_ref[2:3, :])
    kb = norm_groups(kb, bd128, MLA_QK, gains_ref[3:4, :])
    qb_ref[...] = (_rope_lanes(qb, ropeb_ref, MLA_ROPE // 2) * MLA_QK ** -0.5).astype(BF16)
    kb_ref[...] = _rope_lanes(kb, ropeb_ref, MLA_ROPE // 2).astype(BF16)
    vb_ref[...] = jnp.dot(ckv, wuv_ref[...], preferred_element_type=F32).astype(BF16)


def _even_prep(geo, z, ropea, ropeb, gains, cqn, ckvn, wuq, wuk, wuv, bd64, bd128):
    B, nblk, tm, _ = geo
    n = z.shape[0]
    W = HEAD_W
    full = lambda a: pl.BlockSpec(a.shape, lambda b, i: (0,) * a.ndim)
    rope = pl.BlockSpec((3, tm, LANES), lambda b, i: (0, i, 0))
    out = pl.BlockSpec((tm, W), lambda b, i: (b * nblk + i, 0))
    return pl.pallas_call(
        _even_prep_kernel, grid=(B, nblk),
        in_specs=[pl.BlockSpec((tm, z.shape[1]), lambda b, i: (b * nblk + i, 0)), rope, rope, full(gains),
                  full(cqn), full(ckvn), full(wuq), full(wuk), full(wuv), full(bd64), full(bd128)],
        out_specs=[out] * 6, out_shape=[jax.ShapeDtypeStruct((n, W), BF16)] * 6,
        compiler_params=_cparams(("parallel", "parallel")))(z, ropea, ropeb, gains, cqn, ckvn, wuq, wuk, wuv,
                                                            bd64, bd128)


def _attn_kernel(*refs, diff, tk, n_ctx_q, ctx_len, n_keys, lam_init):
    if diff:
        q_ref, k_ref, v_ref, lam_ref, subln_ref, o_ref = refs
    else:
        q_ref, k_ref, v_ref, o_ref = refs
    qi = pl.program_id(2)
    q = q_ref[...]
    tq = q.shape[0]
    if diff:
        lane = lax.broadcasted_iota(jnp.int32, q.shape, 1)
        zero = jnp.zeros_like(q)
        qs = [jnp.where(lane < DIFF_DIM, q, zero), jnp.where(lane >= DIFF_DIM, q, zero)]
    else:
        qs = [q]

    def chunk(start, size, carry):
        k = k_ref[pl.ds(start, size), :]
        v = v_ref[pl.ds(start, size), :]
        out = []
        for (m, l, acc), qq in zip(carry, qs):
            s = lax.dot_general(qq, k, NT, preferred_element_type=F32)
            mn = jnp.maximum(m, jnp.max(s, axis=-1, keepdims=True))
            alpha = jnp.exp(m - mn)
            p = jnp.exp(s - mn)
            l = alpha * l + jnp.sum(p, axis=-1, keepdims=True)
            acc = alpha * acc + jnp.dot(p.astype(BF16), v, preferred_element_type=F32)
            out.append((mn, l, acc))
        return tuple(out)

    init = tuple((jnp.full((tq, 1), -1e30, F32), jnp.zeros((tq, 1), F32), jnp.zeros((tq, LANES), F32))
                 for _ in qs)

    def finish(carry):
        outs = [acc / l for (_, l, acc) in carry]
        if diff:
            lam = (jnp.exp(jnp.sum(lam_ref[0:1, :] * lam_ref[1:2, :], axis=-1, keepdims=True))
                   - jnp.exp(jnp.sum(lam_ref[2:3, :] * lam_ref[3:4, :], axis=-1, keepdims=True)) + lam_init)
            o = _rms_rows(outs[0] - lam * outs[1]) * subln_ref[...] * (1.0 - lam_init)
        else:
            o = outs[0]
        o_ref[...] = o.astype(o_ref.dtype)

    @pl.when(qi < n_ctx_q)
    def _():
        finish(chunk(0, ctx_len, init))

    @pl.when(qi >= n_ctx_q)
    def _():
        finish(lax.fori_loop(0, n_keys // tk, lambda c, cr: chunk(pl.multiple_of(c * tk, tk), tk, cr), init))


def _attention(B, T, ctx_len, q, k, v, lam_vecs=None, subln=None, lam_init=0.0):
    diff = lam_vecs is not None
    tq = TOKEN_BLOCK
    tk = 768 if T % 768 == 0 else TOKEN_BLOCK
    nq = T // tq
    heads = q.shape[1] // LANES
    qspec = pl.BlockSpec((tq, LANES), lambda b, h, i: (b * nq + i, h))
    kvspec = pl.BlockSpec((T, LANES), lambda b, h, i: (b, h))
    in_specs = [qspec, kvspec, kvspec]
    args = [q, k, v]
    if diff:
        in_specs += [pl.BlockSpec(lam_vecs.shape, lambda b, h, i: (0, 0)),
                     pl.BlockSpec(subln.shape, lambda b, h, i: (0, 0))]
        args += [lam_vecs, subln]
    return pl.pallas_call(
        functools.partial(_attn_kernel, diff=diff, tk=tk, n_ctx_q=ctx_len // tq, ctx_len=ctx_len, n_keys=T,
                          lam_init=lam_init),
        grid=(B, heads, nq), in_specs=in_specs, out_specs=qspec,
        out_shape=jax.ShapeDtypeStruct(q.shape, BF16),
        compiler_params=_cparams(("parallel", "parallel", "arbitrary")))(*args)


def _residual_tail(x, y, mod_ref, xo_ref, hqt_ref):
    xn = x + mod_ref[0, 2:3, :] * y
    xo_ref[...] = xn
    hq = _rms_rows(xn) * (1.0 + mod_ref[0, 4:5, :]) + mod_ref[0, 3:4, :]
    hqt_ref[...] = hq.T.astype(BF16)


def _even_out_kernel(oa_ref, ob_ref, x_ref, mod_ref, wo_ref, xo_ref, hqt_ref):
    W = HEAD_W
    y = (jnp.dot(oa_ref[...], wo_ref[0:W, :], preferred_element_type=F32)
         + jnp.dot(ob_ref[...], wo_ref[W:2 * W, :], preferred_element_type=F32))
    _residual_tail(x_ref[...], y, mod_ref, xo_ref, hqt_ref)


def _dn_out_kernel(of_ref, or_ref, zg_ref, onorm_ref, x_ref, mod_ref, wo_ref, xo_ref, hqt_ref):
    parts = []
    for h in range(DN_HEADS):
        hs = slice(h * LANES, (h + 1) * LANES)
        o = of_ref[:, hs] + or_ref[:, hs]
        parts.append((_rms_rows(o) * onorm_ref[...] * _silu(zg_ref[:, hs])).astype(BF16))
    y = jnp.dot(jnp.concatenate(parts, axis=-1), wo_ref[...], preferred_element_type=F32)
    _residual_tail(x_ref[...], y, mod_ref, xo_ref, hqt_ref)


def _mixer_out(geo, kernel_fn, token_args, small_args, x, mod, wo):
    B, nblk, tm, mrow = geo
    n, d = x.shape
    tokspec = lambda a, col: pl.BlockSpec((tm, a[1]), lambda b, i: (b * nblk + i, col))
    in_specs = [tokspec((a, w), col) for (a, w, col) in token_args]
    in_specs += [pl.BlockSpec(a.shape, lambda b, i: (0,) * a.ndim) for a in small_args]
    in_specs += [pl.BlockSpec((tm, d), lambda b, i: (b * nblk + i, 0)),
                 pl.BlockSpec((1, 6, d), lambda b, i: (mrow(b, i), 0, 0)),
                 pl.BlockSpec(wo.shape, lambda b, i: (0, 0))]
    return pl.pallas_call(
        kernel_fn, grid=(B, nblk), in_specs=in_specs,
        out_specs=[pl.BlockSpec((tm, d), lambda b, i: (b * nblk + i, 0)),
                   pl.BlockSpec((d, tm), lambda b, i: (0, b * nblk + i))],
        out_shape=[jax.ShapeDtypeStruct((n, d), F32), jax.ShapeDtypeStruct((d, n), BF16)],
        compiler_params=_cparams(("parallel", "parallel")))(
            *[a for (a, _, _) in token_args], *small_args, x, mod, wo)


def _top_values(s, k):
    rows = lax.broadcasted_iota(jnp.int32, (k, s.shape[1]), 0)

    def body(r, carry):
        s, vals = carry
        m = jnp.max(s, axis=0, keepdims=True)
        vals = jnp.where(rows == r, m, vals)
        return jnp.where(s == m, -jnp.inf, s), vals

    return lax.fori_loop(0, k, body, (s, jnp.zeros((k, s.shape[1]), F32)))[1]


def _peer_select_kernel(hqt_ref, wqt_ref, k1_ref, k2_ref, a1_ref, e1_ref, s2_ref, e2_ref, tau_ref):
    K = PEER_TOPK
    qt = jnp.dot(wqt_ref[...], hqt_ref[...], preferred_element_type=F32)
    dk = k1_ref.shape[1]
    for h in range(PEER_HEADS):
        q1 = qt[(2 * h) * dk:(2 * h + 1) * dk, :].astype(BF16)
        q2 = qt[(2 * h + 1) * dk:(2 * h + 2) * dk, :].astype(BF16)
        s1 = jnp.dot(k1_ref[...], q1, preferred_element_type=F32)
        s2 = jnp.dot(k2_ref[...], q2, preferred_element_type=F32)
        v1 = _top_values(s1, K)
        v2 = _top_values(s2, K)
        row8 = lax.broadcasted_iota(jnp.int32, (8, s1.shape[1]), 0)
        pieces = [v1[0:1, :] + v2, v1[1:2, :] + v2[0:8, :]]
        pieces += [jnp.where(row8 < K // (r1 + 1), v1[r1:r1 + 1, :] + v2[0:8, :], -jnp.inf) for r1 in range(2, 8)]
        pieces.append(v1[8:16, :] + v2[0:1, :])
        cand = jnp.concatenate(pieces, axis=0)
        tau = _top_values(cand, K)[K - 1:K, :]
        m1, m2 = v1[0:1, :], v2[0:1, :]
        zsum = jnp.sum(jnp.where(cand >= tau, jnp.exp(cand - (m1 + m2)), 0.0), axis=0, keepdims=True)
        a1_ref[h] = s1
        e1_ref[h] = jnp.exp(s1 - m1) / zsum
        s2_ref[h] = s2
        e2_ref[h] = jnp.exp(s2 - m2)
        tau_ref[h:h + 1, :] = tau


def _peer_select(hqt, wqt, k1, k2):
    d, n = hqt.shape
    tm = TOKEN_BLOCK
    H, NK = PEER_HEADS, PEER_NKEYS
    big = pl.BlockSpec((H, NK, tm), lambda j: (0, 0, j))
    bigshape = jax.ShapeDtypeStruct((H, NK, n), F32)
    full = lambda a: pl.BlockSpec(a.shape, lambda j: (0,) * a.ndim)
    return pl.pallas_call(
        _peer_select_kernel, grid=(n // tm,),
        in_specs=[pl.BlockSpec((d, tm), lambda j: (0, j)), full(wqt), full(k1), full(k2)],
        out_specs=[big, big, big, big, pl.BlockSpec((H, tm), lambda j: (0, j))],
        out_shape=[bigshape] * 4 + [jax.ShapeDtypeStruct((H, n), F32)],
        compiler_params=_cparams(("parallel",)))(hqt, wqt, k1, k2)


def _gelu(a):
    return 0.5 * a * (1.0 + lax.erf(a * np.float32(math.sqrt(0.5))))


def _peer_dense_kernel(hqt_ref, u_ref, vt_ref, a1_ref, e1_ref, s2_ref, e2_ref, tau_ref, y_ref, acc_ref, act_ref,
                       wt_ref, row_ref):
    c = pl.program_id(1)
    NK = PEER_NKEYS
    n_i = PEER_EXPERT_BLOCK // NK

    @pl.when(c == 0)
    def _():
        acc_ref[...] = jnp.zeros_like(acc_ref)

    act_ref[...] = _gelu(jnp.dot(u_ref[...], hqt_ref[...], preferred_element_type=F32))
    H = PEER_HEADS
    for ii in range(n_i):
        for h in range(H):
            i = c * n_i + ii
            row_ref[0, ii * H + h:ii * H + h + 1, :] = a1_ref[h, pl.ds(i, 1), :]
            row_ref[1, ii * H + h:ii * H + h + 1, :] = e1_ref[h, pl.ds(i, 1), :]
    for ts in range(hqt_ref.shape[1] // LANES):
        tl = slice(ts * LANES, (ts + 1) * LANES)
        for ii in range(n_i):
            g = jnp.zeros((NK, LANES), F32)
            for h in range(H):
                a = row_ref[0, ii * H + h:ii * H + h + 1, tl]
                e = row_ref[1, ii * H + h:ii * H + h + 1, tl]
                sel = (a + s2_ref[h, :, tl]) >= tau_ref[h:h + 1, tl]
                g = g + jnp.where(sel, e * e2_ref[h, :, tl], 0.0)
            rows = slice(ii * NK, (ii + 1) * NK)
            wt_ref[rows, tl] = (g * act_ref[rows, tl]).astype(BF16)
    acc_ref[...] += jnp.dot(vt_ref[...], wt_ref[...], preferred_element_type=F32)

    @pl.when(c == pl.num_programs(1) - 1)
    def _():
        y_ref[...] = acc_ref[...].T


def _peer_dense(hqt, u, vt, a1, e1, s2, e2, tau):
    d, n = hqt.shape
    tm, te = PEER_TOKEN_BLOCK, PEER_EXPERT_BLOCK
    H, NK = PEER_HEADS, PEER_NKEYS
    big = pl.BlockSpec((H, NK, tm), lambda j, c: (0, 0, j))
    return pl.pallas_call(
        _peer_dense_kernel, grid=(n // tm, u.shape[0] // te),
        in_specs=[pl.BlockSpec((d, tm), lambda j, c: (0, j)),
                  pl.BlockSpec((te, d), lambda j, c: (c, 0)),
                  pl.BlockSpec((d, te), lambda j, c: (0, c)),
                  big, big, big, big, pl.BlockSpec((H, tm), lambda j, c: (0, j))],
        out_specs=pl.BlockSpec((tm, d), lambda j, c: (j, 0)),
        out_shape=jax.ShapeDtypeStruct((n, d), F32),
        scratch_shapes=[pltpu.VMEM((d, tm), F32), pltpu.VMEM((te, tm), F32), pltpu.VMEM((te, tm), BF16),
                        pltpu.VMEM((2, te // NK * H, tm), F32)],
        compiler_params=_cparams(("parallel", "arbitrary")))(hqt, u, vt, a1, e1, s2, e2, tau)


def _peer(hqt, wq, k1, k2, u_tab, v_tab):
    sel = _peer_select(hqt, wq.T.astype(BF16), k1.astype(BF16), k2.astype(BF16))
    return _peer_dense(hqt, u_tab.astype(BF16), v_tab.T.astype(BF16), *sel)


def _dn_prep_kernel(z_ref, prev_ref, next_ref, conv_ref, alog_ref, dtb_ref, ab_ref, q_ref, k_ref, v_ref, gb_ref,
                    ext_ref, *, n_ctx_blk, nblk):
    i = pl.program_id(1)
    tm = z_ref.shape[0]
    halo = prev_ref.shape[0]
    pad = DN_CONV // 2
    has_prev = jnp.logical_and(i != 0, i != n_ctx_blk)
    has_next = jnp.logical_and(i != n_ctx_blk - 1, i != nblk - 1)
    ext_ref[0:halo, :] = jnp.where(has_prev, prev_ref[...], 0.0)
    ext_ref[halo:halo + tm, :] = z_ref[...]
    ext_ref[halo + tm:, :] = jnp.where(has_next, next_ref[...], 0.0)
    nq = DN_HEADS * DN_DK
    for j in range(DN_QKV // LANES):
        cs = slice(j * LANES, (j + 1) * LANES)
        acc = conv_ref[0:1, cs] * ext_ref[halo - pad:halo - pad + tm, cs]
        for t in range(1, DN_CONV):
            acc = acc + conv_ref[t:t + 1, cs] * ext_ref[halo - pad + t:halo - pad + t + tm, cs]
        y = _silu(acc)
        if j * LANES < 2 * nq:
            y = y * lax.rsqrt(jnp.sum(y * y, axis=-1, keepdims=True) + EPS)
        if j * LANES < nq:
            q_ref[:, cs] = y * DN_DK ** -0.5
        elif j * LANES < 2 * nq:
            k_ref[:, slice(j * LANES - nq, (j + 1) * LANES - nq)] = y
        else:
            v_ref[:, slice(j * LANES - 2 * nq, (j + 1) * LANES - 2 * nq)] = y
    ab = ab_ref[...]
    nh = 2 * DN_HEADS
    xa = ab[:, 0:nh] + dtb_ref[...]
    softplus = jnp.maximum(xa, 0.0) + jnp.log(1.0 + jnp.exp(-jnp.abs(xa)))
    gb_ref[:, 0:nh] = -jnp.exp(alog_ref[...]) * softplus
    gb_ref[:, nh:2 * nh] = jax.nn.sigmoid(ab[:, nh:2 * nh])


def _dn_prep(geo, n_ctx_blk, z, conv_w, alog, dtb):
    B, nblk, tm, _ = geo
    n = z.shape[0]
    halo = 8
    r = tm // halo
    nh8 = n // halo
    d = DN_HEADS * DN_DK
    prev = pl.BlockSpec((halo, DN_QKV), lambda b, i: (jnp.maximum((b * nblk + i) * r - 1, 0), 0))
    nxt = pl.BlockSpec((halo, DN_QKV), lambda b, i: (jnp.minimum((b * nblk + i + 1) * r, nh8 - 1), 0))
    ab = z[:, DN_QKV + d:DN_QKV + d + 4 * DN_HEADS]
    tok = pl.BlockSpec((tm, d), lambda b, i: (b * nblk + i, 0))
    gbspec = pl.BlockSpec((tm, 4 * DN_HEADS), lambda b, i: (b * nblk + i, 0))
    full = lambda a: pl.BlockSpec(a.shape, lambda b, i: (0,) * a.ndim)
    return pl.pallas_call(
        functools.partial(_dn_prep_kernel, n_ctx_blk=n_ctx_blk, nblk=nblk), grid=(B, nblk),
        in_specs=[pl.BlockSpec((tm, DN_QKV), lambda b, i: (b * nblk + i, 0)), prev, nxt, full(conv_w),
                  full(alog), full(dtb), gbspec],
        out_specs=[tok, tok, tok, gbspec],
        out_shape=[jax.ShapeDtypeStruct((n, d), F32)] * 3 + [jax.ShapeDtypeStruct((n, 4 * DN_HEADS), F32)],
        scratch_shapes=[pltpu.VMEM((tm + 2 * halo, DN_QKV), F32)],
        compiler_params=_cparams(("parallel", "parallel")))(z, z, z, conv_w, alog, dtb, ab)


DN_GROUP = 4


def _bdot(a, b, dims=None):
    a, b = a.astype(BF16), b.astype(BF16)
    if dims is None:
        return jnp.dot(a, b, preferred_element_type=F32)
    return lax.dot_general(a, b, dims, preferred_element_type=F32)


def _block_diag(x, nblk):
    r, n = x.shape
    w = n // nblk
    tall = jnp.concatenate([x] * nblk, axis=0)
    rb = lax.broadcasted_iota(jnp.int32, tall.shape, 0) // r
    lb = lax.broadcasted_iota(jnp.int32, tall.shape, 1) // w
    return jnp.where(rb == lb, tall, jnp.zeros_like(tall))


def _dn_local_kernel(q_ref, k_ref, v_ref, gb_ref, u_ref, wq_ref, akd_ref, gl_ref):
    C, G, nh = DN_CHUNK, DN_GROUP, DN_HEADS
    d_all = nh * LANES
    ri = lax.broadcasted_iota(jnp.int32, (C, G * C), 0)
    ci = lax.broadcasted_iota(jnp.int32, (C, G * C), 1) % C
    eyecat = (ri == ci).astype(F32)
    ones = jnp.ones((C, C), F32)
    r2 = lax.broadcasted_iota(jnp.int32, (C, C), 0)
    c2 = lax.broadcasted_iota(jnp.int32, (C, C), 1)
    eye128 = (lax.broadcasted_iota(jnp.int32, (LANES, LANES), 0)
              == lax.broadcasted_iota(jnp.int32, (LANES, LANES), 1)).astype(BF16)
    gb = gb_ref[...]
    ncol = gb.shape[1]
    sel_row = lax.broadcasted_iota(jnp.int32, (ncol, d_all), 0)
    sel_head = lax.broadcasted_iota(jnp.int32, (ncol, d_all), 1) // LANES
    cat_row = lax.broadcasted_iota(jnp.int32, (ncol, G * C), 0)
    cat_head = lax.broadcasted_iota(jnp.int32, (ncol, G * C), 1) // C
    q, k, v = q_ref[...], k_ref[...], v_ref[...]
    for d in range(2):
        incl = (ri >= ci) if d == 0 else (ri <= ci)
        strict = (ri > ci) if d == 0 else (ri < ci)
        tri = ((r2 >= c2) if d == 0 else (r2 <= c2)).astype(F32)
        last = C - 1 if d == 0 else 0
        gcs = jnp.dot(tri, gb, precision=HI, preferred_element_type=F32)
        spread = lambda m, first: jnp.dot(m, (sel_row == first + sel_head).astype(F32), precision=HI,
                                          preferred_element_type=F32)
        gc = spread(gcs, d * nh)
        bet = spread(gb, 2 * nh + d * nh)
        eg = jnp.exp(gc)
        glrow = gc[last:last + 1, :]
        kbm = k * bet
        kd = k * jnp.exp(glrow - gc)
        wq_ref[d, C:2 * C, :] = (q * eg).astype(BF16)
        gl_ref[d, 0] = jnp.exp(glrow)
        vb = v * bet
        kbe = kbm * eg
        for g in range(nh // G):
            gs = slice(g * G * LANES, (g + 1) * G * LANES)
            cs = slice(g * G * C, (g + 1) * G * C)
            kq = _bdot(jnp.concatenate([kbm[:, gs], q[:, gs]], axis=0), _block_diag(k[:, gs], G), NT)
            gcol = jnp.dot(gcs, (cat_row == d * nh + g * G + cat_head).astype(F32), precision=HI,
                           preferred_element_type=F32)
            grow = jnp.dot(ones, gcol * eyecat, precision=HI, preferred_element_type=F32)
            decay = jnp.where(incl, jnp.exp(jnp.where(incl, gcol - grow, 0.0)), 0.0)
            nm = jnp.where(strict, -kq[0:C] * decay, 0.0)
            akd_ref[d, 0, 0:C, cs] = jnp.where(incl, kq[C:2 * C] * decay, 0.0).astype(BF16)
            tinv = eyecat + nm
            p, pbd = nm, _block_diag(nm, G)
            for _ in range(int(math.log2(C)) - 1):
                p = _bdot(p, pbd)
                pbd = _block_diag(p, G)
                tinv = tinv + _bdot(tinv, pbd)
            u_ref[d, :, gs] = _bdot(tinv, _block_diag(vb[:, gs], G))
            wq_ref[d, 0:C, gs] = _bdot(tinv, _block_diag(kbe[:, gs], G)).astype(BF16)
            kdstack = jnp.concatenate([kd[:, (g * G + j) * LANES:(g * G + j + 1) * LANES] for j in range(G)],
                                      axis=0)
            akd_ref[d, 0, C:C + DN_DK, cs] = _bdot(eye128, kdstack, NT).astype(BF16)


def _dn_local(q, k, v, gb):
    n, d = q.shape
    C = DN_CHUNK
    nc = n // C
    tok = pl.BlockSpec((C, d), lambda j: (j, 0))
    return pl.pallas_call(
        _dn_local_kernel, grid=(nc,),
        in_specs=[tok, tok, tok, pl.BlockSpec((C, gb.shape[1]), lambda j: (j, 0))],
        out_specs=[pl.BlockSpec((2, C, d), lambda j: (0, j, 0)),
                   pl.BlockSpec((2, 2 * C, d), lambda j: (0, j, 0)),
                   pl.BlockSpec((2, 1, C + DN_DK, DN_HEADS * C), lambda j: (0, j, 0, 0)),
                   pl.BlockSpec((2, 1, 1, d), lambda j: (0, j, 0, 0))],
        out_shape=[jax.ShapeDtypeStruct((2, n, d), F32),
                   jax.ShapeDtypeStruct((2, 2 * n, d), BF16),
                   jax.ShapeDtypeStruct((2, nc, C + DN_DK, DN_HEADS * C), BF16),
                   jax.ShapeDtypeStruct((2, nc, 1, d), F32)],
        compiler_params=_cparams(("parallel",)))(q, k, v, gb)


def _dn_scan_kernel(*refs):
    ins, (of_ref, or_ref, s_ref) = refs[:8], refs[8:]
    C = DN_CHUNK

    @pl.when(pl.program_id(1) == 0)
    def _():
        s_ref[...] = jnp.zeros_like(s_ref)

    for d, o_ref in enumerate((of_ref, or_ref)):
        u_ref, wq_ref, akd_ref, gl_ref = ins[4 * d:4 * d + 4]
        for p in range(DN_HEADS // 2):
            ps = slice(2 * p * LANES, (2 * p + 2) * LANES)
            s2 = jnp.concatenate([s_ref[d, 2 * p], s_ref[d, 2 * p + 1]], axis=-1)
            r = _bdot(wq_ref[0, :, ps], _block_diag(s2, 2))
            vn = u_ref[0, :, ps] - r[0:C]
            r2 = _bdot(akd_ref[0, 0, :, 2 * p * C:(2 * p + 2) * C], _block_diag(vn, 2))
            o_ref[:, ps] = r[C:2 * C] + r2[0:C]
            snew = s2 * gl_ref[0, 0, :, ps] + r2[C:C + DN_DK]
            s_ref[d, 2 * p] = snew[:, 0:LANES]
            s_ref[d, 2 * p + 1] = snew[:, LANES:2 * LANES]


def _dn_scan(B, T, ctx_len, u, wq, akd, gl):
    _, n, d = u.shape
    C = DN_CHUNK
    nch, ncc = T // C, ctx_len // C
    rpos = lambda c: jnp.where(c < ncc, ncc - 1 - c, nch - 1 - (c - ncc))
    pos = (lambda b, c: b * nch + c, lambda b, c: b * nch + rpos(c))
    in_specs, args = [], []
    for dd in range(2):
        p = pos[dd]
        in_specs += [pl.BlockSpec((1, C, d), lambda b, c, p=p, dd=dd: (dd, p(b, c), 0)),
                     pl.BlockSpec((1, 2 * C, d), lambda b, c, p=p, dd=dd: (dd, p(b, c), 0)),
                     pl.BlockSpec((1, 1, C + DN_DK, DN_HEADS * C), lambda b, c, p=p, dd=dd: (dd, p(b, c), 0, 0)),
                     pl.BlockSpec((1, 1, 1, d), lambda b, c, p=p, dd=dd: (dd, p(b, c), 0, 0))]
        args += [u, wq, akd, gl]
    return pl.pallas_call(
        _dn_scan_kernel, grid=(B, nch), in_specs=in_specs,
        out_specs=[pl.BlockSpec((C, d), lambda b, c: (pos[0](b, c), 0)),
                   pl.BlockSpec((C, d), lambda b, c: (pos[1](b, c), 0))],
        out_shape=[jax.ShapeDtypeStruct((n, d), F32)] * 2,
        scratch_shapes=[pltpu.VMEM((2, DN_HEADS, DN_DK, LANES), F32)],
        compiler_params=_cparams(("arbitrary", "arbitrary")))(*args)


def _final_kernel(x_ref, y_ref, mod_ref, o_ref):
    o_ref[0] = x_ref[...] + mod_ref[0, 5:6, :] * y_ref[...]


def _final(B, T, ctx_len, x, y, mod):
    n, d = x.shape
    tm = TOKEN_BLOCK
    nblk, ncb = T // tm, ctx_len // tm
    tok = pl.BlockSpec((tm, d), lambda b, i: (b * nblk + ncb + i, 0))
    return pl.pallas_call(
        _final_kernel, grid=(B, nblk - ncb),
        in_specs=[tok, tok, pl.BlockSpec((1, 6, d), lambda b, i: (b, 0, 0))],
        out_specs=pl.BlockSpec((1, tm, d), lambda b, i: (b, i, 0)),
        out_shape=jax.ShapeDtypeStruct((B, T - ctx_len, d), F32),
        compiler_params=_cparams(("parallel", "parallel")))(x, y, mod)


def _rope_tables(seq, ctx_len):
    rows = seq // GRID_W
    r = jnp.broadcast_to(jnp.arange(rows, dtype=F32)[:, None], (rows, GRID_W)).reshape(-1)
    cl = jnp.broadcast_to(jnp.arange(GRID_W, dtype=F32)[None, :], (rows, GRID_W)).reshape(-1)

    def angles(rot_dim):
        nf = rot_dim // 4
        inv = ROPE_BASE ** (-jnp.arange(nf, dtype=F32) / nf)
        ang = jnp.concatenate([r[:, None] * inv, cl[:, None] * inv], axis=-1)
        ang = jnp.concatenate([jnp.zeros((ctx_len, rot_dim // 2), F32), ang], axis=0)
        return jnp.cos(ang), jnp.sin(ang)

    T = seq + ctx_len
    ca, sa = angles(DIFF_DIM)
    z = jnp.zeros_like(sa)
    ta = jnp.stack([jnp.concatenate([ca, ca] * 2, -1), jnp.concatenate([-sa, z] * 2, -1),
                    jnp.concatenate([z, sa] * 2, -1)])
    cb, sb = angles(MLA_ROPE)
    one, zn, zt = jnp.ones((T, MLA_NOPE), F32), jnp.zeros((T, MLA_NOPE), F32), jnp.zeros((T, LANES - MLA_QK), F32)
    zb = jnp.zeros_like(sb)
    tb = jnp.stack([jnp.concatenate([one, cb, cb, 1.0 + zt], -1), jnp.concatenate([zn, -sb, zb, zt], -1),
                    jnp.concatenate([zn, zb, sb, zt], -1)])
    return ta, tb


def _pad_heads(w, heads, width):
    lead = w.shape[:-1]
    w = w.reshape(lead + (heads, width))
    return jnp.pad(w, [(0, 0)] * len(lead) + [(0, 0), (0, LANES - width)]).reshape(lead + (heads * LANES,))


def _block_diag_ones(n, group):
    idx = np.arange(n) // group
    return jnp.asarray(idx[:, None] == idx[None, :], dtype=BF16)


def kernel(x, c, ctx, c_ctx, ada_w, ada_b, ev_w_in, ev_w_out, diff_q_norm, diff_k_norm, diff_lam_q1, diff_lam_k1, diff_lam_q2, diff_lam_k2, diff_subln, mla_cq_norm, mla_ckv_norm, mla_w_uq, mla_w_ukv, mla_q_norm, mla_k_norm, dn_w_in, dn_conv, dn_a_log, dn_dt_bias, dn_o_norm, dn_w_out, peer_wq, peer_k1, peer_k2, peer_u, peer_v):
    B, S, D = x.shape
    CTX = ctx.shape[1]
    T = CTX + S
    depth = ada_w.shape[0]
    tm = TOKEN_BLOCK
    assert CTX % tm == 0 and S % tm == 0 and (B * T) % PEER_TOKEN_BLOCK == 0 and B + 1 <= 8
    nblk, ncb = T // tm, CTX // tm
    geo = (B, nblk, tm, lambda b, i: jnp.where(i < ncb, B, b))

    cc = jnp.zeros((8, D), F32).at[:B].set(c).at[B].set(c_ctx)
    mods = _adaln(cc, ada_w, ada_b).reshape(depth, 8, 6, D)
    xs = jnp.concatenate([ctx, x], axis=1).reshape(B * T, D)
    ropea, ropeb = _rope_tables(S, CTX)
    bd64, bd128 = _block_diag_ones(HEAD_W, DIFF_DIM), _block_diag_ones(HEAD_W, LANES)

    y = None
    for l in range(depth):
        i = l // 2
        mod = mods[l]
        gmod = mods[l - 1] if l else None
        if l % 2 == 0:
            lam_init = 0.8 - 0.6 * math.exp(-0.3 * l)
            w_in = ev_w_in[i]
            o = 3 * HEAD_W + MLA_Q_RANK + MLA_KV_RANK
            kr_rep = jnp.pad(jnp.broadcast_to(w_in[:, None, o:o + MLA_ROPE], (D, MLA_HEADS, MLA_ROPE)),
                             ((0, 0), (0, 0), (MLA_NOPE, LANES - MLA_QK))).reshape(D, HEAD_W)
            w_cat = jnp.concatenate([w_in[:, :o], kr_rep], axis=1).astype(BF16)
            res = _modmm(geo, xs, mod, w_cat, y, gmod)
            (xs, z) = res if y is not None else (xs, res)
            ukv = mla_w_ukv[i].reshape(MLA_KV_RANK, MLA_HEADS, MLA_NOPE + MLA_VDIM)
            gains = jnp.stack([jnp.tile(diff_q_norm[i], HEAD_W // DIFF_DIM), jnp.tile(diff_k_norm[i], HEAD_W // DIFF_DIM),
                               jnp.tile(jnp.pad(mla_q_norm[i], (0, LANES - MLA_QK)), MLA_HEADS),
                               jnp.tile(jnp.pad(mla_k_norm[i], (0, LANES - MLA_QK)), MLA_HEADS)])
            qa, ka, va, qb, kb, vb = _even_prep(
                geo, z, ropea, ropeb, gains, mla_cq_norm[i][None, :], mla_ckv_norm[i][None, :],
                _pad_heads(mla_w_uq[i], MLA_HEADS, MLA_QK).astype(BF16),
                _pad_heads(ukv[:, :, :MLA_NOPE].reshape(MLA_KV_RANK, -1), MLA_HEADS, MLA_NOPE).astype(BF16),
                ukv[:, :, MLA_NOPE:].reshape(MLA_KV_RANK, -1).astype(BF16), bd64, bd128)
            lam_vecs = jnp.stack([diff_lam_q1[i], diff_lam_k1[i], diff_lam_q2[i], diff_lam_k2[i]])
            oa = _attention(B, T, CTX, qa, ka, va, lam_vecs, diff_subln[i][None, :], lam_init)
            ob = _attention(B, T, CTX, qb, kb, vb)
            xs, hqt = _mixer_out(geo, _even_out_kernel, [(oa, HEAD_W, 0), (ob, HEAD_W, 0)], [], xs, mod,
                                 ev_w_out[i].astype(BF16))
        else:
            nin = dn_w_in.shape[2]
            w_in = jnp.pad(dn_w_in[i], ((0, 0), (0, -nin % LANES))).astype(BF16)
            res = _modmm(geo, xs, mod, w_in, y, gmod)
            (xs, z) = res if y is not None else (xs, res)
            q, k, v, gb = _dn_prep(geo, ncb, z, dn_conv[i], dn_a_log[i].reshape(1, -1),
                                   dn_dt_bias[i].reshape(1, -1))
            of, orv = _dn_scan(B, T, CTX, *_dn_local(q, k, v, gb))
            xs, hqt = _mixer_out(geo, _dn_out_kernel,
                                 [(of, D, 0), (orv, D, 0), (z, D, DN_QKV // D)], [dn_o_norm[i][None, :]], xs, mod,
                                 dn_w_out[i].astype(BF16))
        y = _peer(hqt, peer_wq[l], peer_k1[l], peer_k2[l], peer_u[l], peer_v[l])
    return _final(B, T, CTX, xs, y, mods[depth - 1])
```

```python
import functools
import math

import jax
import jax.numpy as jnp
import numpy as np
from jax import lax
from jax.experimental import pallas as pl
from jax.experimental.pallas import tpu as pltpu

F32 = jnp.float32
BF16 = jnp.bfloat16
HI = lax.Precision.HIGHEST
NT = (((1,), (1,)), ((), ()))

EPS = 1e-6
LOG2E = math.log2(math.e)
ROPE_BASE = 10000.0
GRID_W = 64
LANES = 128
TOKEN_BLOCK = 256
VMEM_LIMIT = 56 * 1024 * 1024

DIFF_HEADS, DIFF_DIM = 4, 64
MLA_HEADS, MLA_Q_RANK, MLA_KV_RANK, MLA_NOPE, MLA_ROPE, MLA_VDIM = 4, 256, 128, 64, 32, 128
MLA_QK = MLA_NOPE + MLA_ROPE
HEAD_W = DIFF_HEADS * LANES
DN_HEADS, DN_DK, DN_CONV, DN_CHUNK = 8, 128, 5, 64
DN_QKV = 3 * DN_HEADS * DN_DK
PEER_HEADS, PEER_NKEYS, PEER_TOPK = 8, 128, 16
PEER_CAND_ROWS = 80
PEER_TOKEN_BLOCK = 512
PEER_EXPERT_BLOCK = 512


def _cparams(sem):
    return pltpu.CompilerParams(dimension_semantics=sem, vmem_limit_bytes=VMEM_LIMIT)


def _rms_rows(x):
    return x * lax.rsqrt(jnp.mean(x * x, axis=-1, keepdims=True) + EPS)


def _silu(x):
    return x * jax.nn.sigmoid(x)


def _group_sum(sq, bd):
    hi = sq.astype(BF16)
    lo = (sq - hi.astype(F32)).astype(BF16)
    return jnp.dot(hi, bd, preferred_element_type=F32) + jnp.dot(lo, bd, preferred_element_type=F32)


def _adaln_kernel(c_ref, w_ref, b_ref, o_ref):
    s = _silu(c_ref[...])
    o_ref[0] = jnp.dot(s, w_ref[0], precision=HI, preferred_element_type=F32) + b_ref[0]


def _adaln(cc, ada_w, ada_b):
    depth, d, n = ada_w.shape
    tn = 1536
    return pl.pallas_call(
        _adaln_kernel,
        grid=(depth, n // tn),
        in_specs=[pl.BlockSpec((8, d), lambda l, j: (0, 0)),
                  pl.BlockSpec((1, d, tn), lambda l, j: (l, 0, j)),
                  pl.BlockSpec((1, 1, tn), lambda l, j: (l, 0, j))],
        out_specs=pl.BlockSpec((1, 8, tn), lambda l, j: (l, 0, j)),
        out_shape=jax.ShapeDtypeStruct((depth, 8, n), F32),
        compiler_params=_cparams(("arbitrary", "arbitrary")),
    )(cc, ada_w, ada_b.reshape(depth, 1, n))


def _modmm_kernel(*refs, has_y):
    if has_y:
        x_ref, y_ref, gmod_ref, mod_ref, w_ref, xo_ref, z_ref = refs
        x = x_ref[...] + gmod_ref[0, 5:6, :] * y_ref[...]
        xo_ref[...] = x
    else:
        x_ref, mod_ref, w_ref, z_ref = refs
        x = x_ref[...]
    h = _rms_rows(x) * (1.0 + mod_ref[0, 1:2, :]) + mod_ref[0, 0:1, :]
    z_ref[...] = jnp.dot(h.astype(BF16), w_ref[...], preferred_element_type=F32)


def _modmm(geo, x, mod, w, y=None, gmod=None):
    B, nblk, tm, mrow = geo
    n, d = x.shape
    nout = w.shape[1]
    tok = pl.BlockSpec((tm, d), lambda b, i: (b * nblk + i, 0))
    modspec = pl.BlockSpec((1, 6, d), lambda b, i: (mrow(b, i), 0, 0))
    wspec = pl.BlockSpec((d, nout), lambda b, i: (0, 0))
    zspec = pl.BlockSpec((tm, nout), lambda b, i: (b * nblk + i, 0))
    zshape = jax.ShapeDtypeStruct((n, nout), F32)
    if y is None:
        return pl.pallas_call(
            functools.partial(_modmm_kernel, has_y=False), grid=(B, nblk),
            in_specs=[tok, modspec, wspec], out_specs=zspec, out_shape=zshape,
            compiler_params=_cparams(("parallel", "parallel")))(x, mod, w)
    return pl.pallas_call(
        functools.partial(_modmm_kernel, has_y=True), grid=(B, nblk),
        in_specs=[tok, tok, modspec, modspec, wspec], out_specs=[tok, zspec],
        out_shape=[jax.ShapeDtypeStruct((n, d), F32), zshape],
        compiler_params=_cparams(("parallel", "parallel")))(x, y, gmod, mod, w)


def _rope_lanes(v, tab_ref, half):
    rep = v.shape[-1] // LANES
    c = jnp.concatenate([tab_ref[0]] * rep, axis=-1)
    sm = jnp.concatenate([tab_ref[1]] * rep, axis=-1)
    sp = jnp.concatenate([tab_ref[2]] * rep, axis=-1)
    n = v.shape[-1]
    return v * c + pltpu.roll(v, n - half, 1) * sm + pltpu.roll(v, half, 1) * sp


def _even_prep_kernel(z_ref, ropea_ref, ropeb_ref, gains_ref, cqn_ref, ckvn_ref, wuq_ref, wuk_ref, wuv_ref,
                      bd64_ref, bd128_ref, qa_ref, ka_ref, va_ref, qb_ref, kb_ref, vb_ref):
    W = HEAD_W
    bd64 = bd64_ref[...]
    bd128 = bd128_ref[...]

    def norm_groups(v, bd, width, gain):
        ms = _group_sum(v * v, bd) * (1.0 / width)
        return v * lax.rsqrt(ms + EPS) * gain

    qa = norm_groups(z_ref[:, 0:W], bd64, DIFF_DIM, gains_ref[0:1, :])
    ka = norm_groups(z_ref[:, W:2 * W], bd64, DIFF_DIM, gains_ref[1:2, :])
    qa_ref[...] = (_rope_lanes(qa, ropea_ref, DIFF_DIM // 2) * (DIFF_DIM ** -0.5 * LOG2E)).astype(BF16)
    ka_ref[...] = _rope_lanes(ka, ropea_ref, DIFF_DIM // 2).astype(BF16)
    va_ref[...] = z_ref[:, 2 * W:3 * W].astype(BF16)

    o = 3 * W
    cq = (_rms_rows(z_ref[:, o:o + MLA_Q_RANK]) * cqn_ref[...]).astype(BF16)
    o += MLA_Q_RANK
    ckv = (_rms_rows(z_ref[:, o:o + MLA_KV_RANK]) * ckvn_ref[...]).astype(BF16)
    o += MLA_KV_RANK
    kr = z_ref[:, o:o + W]
    qb = jnp.dot(cq, wuq_ref[...], preferred_element_type=F32)
    kb = jnp.dot(ckv, wuk_ref[...], preferred_element_type=F32) + kr
    qb = norm_groups(qb, bd128, MLA_QK, gains_ref[2:3, :])
    kb = norm_groups(kb, bd128, MLA_QK, gains_ref[3:4, :])
    qb_ref[...] = (_rope_lanes(qb, ropeb_ref, MLA_ROPE // 2) * (MLA_QK ** -0.5 * LOG2E)).astype(BF16)
    kb_ref[...] = _rope_lanes(kb, ropeb_ref, MLA_ROPE // 2).astype(BF16)
    vb_ref[...] = jnp.dot(ckv, wuv_ref[...], preferred_element_type=F32).astype(BF16)


def _even_prep(geo, z, ropea, ropeb, gains, cqn, ckvn, wuq, wuk, wuv, bd64, bd128):
    B, nblk, tm, _ = geo
    n = z.shape[0]
    W = HEAD_W
    full = lambda a: pl.BlockSpec(a.shape, lambda b, i: (0,) * a.ndim)
    rope = pl.BlockSpec((3, tm, LANES), lambda b, i: (0, i, 0))
    out = pl.BlockSpec((tm, W), lambda b, i: (b * nblk + i, 0))
    return pl.pallas_call(
        _even_prep_kernel, grid=(B, nblk),
        in_specs=[pl.BlockSpec((tm, z.shape[1]), lambda b, i: (b * nblk + i, 0)), rope, rope, full(gains),
                  full(cqn), full(ckvn), full(wuq), full(wuk), full(wuv), full(bd64), full(bd128)],
        out_specs=[out] * 6, out_shape=[jax.ShapeDtypeStruct((n, W), BF16)] * 6,
        compiler_params=_cparams(("parallel", "parallel")))(z, ropea, ropeb, gains, cqn, ckvn, wuq, wuk, wuv,
                                                            bd64, bd128)


def _attn_kernel(*refs, diff, tk, n_ctx_q, ctx_len, n_keys, lam_init):
    if diff:
        q_ref, k_ref, v_ref, lam_ref, subln_ref, o_ref = refs
    else:
        q_ref, k_ref, v_ref, o_ref = refs
    qi = pl.program_id(2)
    q = q_ref[...]
    tq = q.shape[0]
    if diff:
        lane = lax.broadcasted_iota(jnp.int32, q.shape, 1)
        zero = jnp.zeros_like(q)
        qs = [jnp.where(lane < DIFF_DIM, q, zero), jnp.where(lane >= DIFF_DIM, q, zero)]
    else:
        qs = [q]

    def chunk(start, size, carry):
        k = k_ref[pl.ds(start, size), :]
        v = v_ref[pl.ds(start, size), :]
        s = [lax.dot_general(qq, k, NT, preferred_element_type=F32) for qq in qs]
        mn = [jnp.maximum(m, jnp.max(x, axis=-1, keepdims=True)) for (m, _, _), x in zip(carry, s)]
        p = [jnp.exp2(x - y) for x, y in zip(s, mn)]
        alpha = [jnp.exp2(m - y) for (m, _, _), y in zip(carry, mn)]
        pv = [jnp.dot(x.astype(BF16), v, preferred_element_type=F32) for x in p]
        return tuple((y, a * l + jnp.sum(x, axis=-1, keepdims=True), a * acc + z)
                     for (_, l, acc), y, a, x, z in zip(carry, mn, alpha, p, pv))

    init = tuple((jnp.full((tq, 1), -1e30, F32), jnp.zeros((tq, 1), F32), jnp.zeros((tq, LANES), F32))
                 for _ in qs)

    def finish(carry):
        outs = [acc / l for (_, l, acc) in carry]
        if diff:
            lam = (jnp.exp(jnp.sum(lam_ref[0:1, :] * lam_ref[1:2, :], axis=-1, keepdims=True))
                   - jnp.exp(jnp.sum(lam_ref[2:3, :] * lam_ref[3:4, :], axis=-1, keepdims=True)) + lam_init)
            o = _rms_rows(outs[0] - lam * outs[1]) * subln_ref[...] * (1.0 - lam_init)
        else:
            o = outs[0]
        o_ref[...] = o.astype(o_ref.dtype)

    @pl.when(qi < n_ctx_q)
    def _():
        finish(chunk(0, ctx_len, init))

    @pl.when(qi >= n_ctx_q)
    def _():
        finish(lax.fori_loop(0, n_keys // tk, lambda c, cr: chunk(pl.multiple_of(c * tk, tk), tk, cr), init))


def _attention(B, T, ctx_len, q, k, v, lam_vecs=None, subln=None, lam_init=0.0):
    diff = lam_vecs is not None
    tq = TOKEN_BLOCK
    tk = 768 if T % 768 == 0 else TOKEN_BLOCK
    nq = T // tq
    heads = q.shape[1] // LANES
    qspec = pl.BlockSpec((tq, LANES), lambda b, h, i: (b * nq + i, h))
    kvspec = pl.BlockSpec((T, LANES), lambda b, h, i: (b, h))
    in_specs = [qspec, kvspec, kvspec]
    args = [q, k, v]
    if diff:
        in_specs += [pl.BlockSpec(lam_vecs.shape, lambda b, h, i: (0, 0)),
                     pl.BlockSpec(subln.shape, lambda b, h, i: (0, 0))]
        args += [lam_vecs, subln]
    return pl.pallas_call(
        functools.partial(_attn_kernel, diff=diff, tk=tk, n_ctx_q=ctx_len // tq, ctx_len=ctx_len, n_keys=T,
                          lam_init=lam_init),
        grid=(B, heads, nq), in_specs=in_specs, out_specs=qspec,
        out_shape=jax.ShapeDtypeStruct(q.shape, BF16),
        compiler_params=_cparams(("parallel", "parallel", "arbitrary")))(*args)


def _residual_tail(x, y, mod_ref, xo_ref, hqt_ref):
    xn = x + mod_ref[0, 2:3, :] * y
    xo_ref[...] = xn
    hq = _rms_rows(xn) * (1.0 + mod_ref[0, 4:5, :]) + mod_ref[0, 3:4, :]
    hqt_ref[...] = hq.T.astype(BF16)


def _even_out_kernel(oa_ref, ob_ref, x_ref, mod_ref, wo_ref, xo_ref, hqt_ref):
    W = HEAD_W
    y = (jnp.dot(oa_ref[...], wo_ref[0:W, :], preferred_element_type=F32)
         + jnp.dot(ob_ref[...], wo_ref[W:2 * W, :], preferred_element_type=F32))
    _residual_tail(x_ref[...], y, mod_ref, xo_ref, hqt_ref)


def _dn_out_kernel(of_ref, or_ref, zg_ref, onorm_ref, x_ref, mod_ref, wo_ref, xo_ref, hqt_ref):
    parts = []
    for h in range(DN_HEADS):
        hs = slice(h * LANES, (h + 1) * LANES)
        o = of_ref[:, hs] + or_ref[:, hs]
        parts.append((_rms_rows(o) * onorm_ref[...] * _silu(zg_ref[:, hs])).astype(BF16))
    y = jnp.dot(jnp.concatenate(parts, axis=-1), wo_ref[...], preferred_element_type=F32)
    _residual_tail(x_ref[...], y, mod_ref, xo_ref, hqt_ref)


def _mixer_out(geo, kernel_fn, token_args, small_args, x, mod, wo):
    B, nblk, tm, mrow = geo
    n, d = x.shape
    tokspec = lambda a, col: pl.BlockSpec((tm, a[1]), lambda b, i: (b * nblk + i, col))
    in_specs = [tokspec((a, w), col) for (a, w, col) in token_args]
    in_specs += [pl.BlockSpec(a.shape, lambda b, i: (0,) * a.ndim) for a in small_args]
    in_specs += [pl.BlockSpec((tm, d), lambda b, i: (b * nblk + i, 0)),
                 pl.BlockSpec((1, 6, d), lambda b, i: (mrow(b, i), 0, 0)),
                 pl.BlockSpec(wo.shape, lambda b, i: (0, 0))]
    return pl.pallas_call(
        kernel_fn, grid=(B, nblk), in_specs=in_specs,
        out_specs=[pl.BlockSpec((tm, d), lambda b, i: (b * nblk + i, 0)),
                   pl.BlockSpec((d, tm), lambda b, i: (0, b * nblk + i))],
        out_shape=[jax.ShapeDtypeStruct((n, d), F32), jax.ShapeDtypeStruct((d, n), BF16)],
        compiler_params=_cparams(("parallel", "parallel")))(
            *[a for (a, _, _) in token_args], *small_args, x, mod, wo)


def _peer_select_kernel(hqt_ref, wqt_ref, k1_ref, k2_ref, a1_ref, e1_ref, s2_ref, e2_ref, tau_ref, work_ref, top_ref,
                        cand_ref):
    K, H = PEER_TOPK, PEER_HEADS
    qt = jnp.dot(wqt_ref[...], hqt_ref[...], preferred_element_type=F32)
    dk = k1_ref.shape[1]
    for h in range(H):
        q1 = qt[(2 * h) * dk:(2 * h + 1) * dk, :].astype(BF16)
        q2 = qt[(2 * h + 1) * dk:(2 * h + 2) * dk, :].astype(BF16)
        s1 = jnp.dot(k1_ref[...], q1, preferred_element_type=F32)
        s2 = jnp.dot(k2_ref[...], q2, preferred_element_type=F32)
        a1_ref[h] = s1
        s2_ref[h] = s2
        work_ref[2 * h] = s1
        work_ref[2 * h + 1] = s2

    def extract(ref, count):
        s = [ref[a] for a in range(count)]
        m = [jnp.max(x, axis=0, keepdims=True) for x in s]
        for a in range(count):
            ref[a] = jnp.where(s[a] == m[a], -jnp.inf, s[a])
        return m

    def top_round(r, carry):
        for a, m in enumerate(extract(work_ref, 2 * H)):
            top_ref[a, pl.ds(r, 1), :] = m
        return carry

    lax.fori_loop(0, K, top_round, 0)

    def candidates(h):
        v1, v2 = top_ref[2 * h], top_ref[2 * h + 1]
        row8 = lax.broadcasted_iota(jnp.int32, (8, v1.shape[1]), 0)
        pieces = [v1[0:1, :] + v2, v1[1:2, :] + v2[0:8, :]]
        pieces += [jnp.where(row8 < K // (r1 + 1), v1[r1:r1 + 1, :] + v2[0:8, :], -jnp.inf) for r1 in range(2, 8)]
        pieces.append(v1[8:16, :] + v2[0:1, :])
        return jnp.concatenate(pieces, axis=0)

    for h in range(H):
        cand_ref[h] = candidates(h)
    tau = lax.fori_loop(0, K, lambda r, carry: tuple(extract(cand_ref, H)),
                        tuple(jnp.zeros((1, tau_ref.shape[1]), F32) for _ in range(H)))
    for h in range(H):
        cand = candidates(h)
        m1, m2 = top_ref[2 * h, 0:1, :], top_ref[2 * h + 1, 0:1, :]
        zsum = jnp.sum(jnp.where(cand >= tau[h], jnp.exp(cand - (m1 + m2)), 0.0), axis=0, keepdims=True)
        e1_ref[h] = jnp.exp(a1_ref[h] - m1) / zsum
        e2_ref[h] = jnp.exp(s2_ref[h] - m2)
        tau_ref[h:h + 1, :] = tau[h]


def _peer_select(hqt, wqt, k1, k2):
    d, n = hqt.shape
    tm = TOKEN_BLOCK
    H, NK = PEER_HEADS, PEER_NKEYS
    big = pl.BlockSpec((H, NK, tm), lambda j: (0, 0, j))
    bigshape = jax.ShapeDtypeStruct((H, NK, n), F32)
    full = lambda a: pl.BlockSpec(a.shape, lambda j: (0,) * a.ndim)
    return pl.pallas_call(
        _peer_select_kernel, grid=(n // tm,),
        in_specs=[pl.BlockSpec((d, tm), lambda j: (0, j)), full(wqt), full(k1), full(k2)],
        out_specs=[big, big, big, big, pl.BlockSpec((H, tm), lambda j: (0, j))],
        out_shape=[bigshape] * 4 + [jax.ShapeDtypeStruct((H, n), F32)],
        scratch_shapes=[pltpu.VMEM((2 * H, NK, tm), F32), pltpu.VMEM((2 * H, PEER_TOPK, tm), F32),
                        pltpu.VMEM((H, PEER_CAND_ROWS, tm), F32)],
        compiler_params=_cparams(("parallel",)))(hqt, wqt, k1, k2)


def _gelu(a):
    return 0.5 * a * (1.0 + lax.erf(a * np.float32(math.sqrt(0.5))))


def _peer_dense_kernel(hqt_ref, u_ref, vt_ref, a1_ref, e1_ref, s2_ref, e2_ref, tau_ref, y_ref, acc_ref, act_ref,
                       wt_ref, row_ref):
    c = pl.program_id(1)
    NK = PEER_NKEYS
    n_i = PEER_EXPERT_BLOCK // NK

    @pl.when(c == 0)
    def _():
        acc_ref[...] = jnp.zeros_like(acc_ref)

    H = PEER_HEADS
    strips = [slice(ts * LANES, (ts + 1) * LANES) for ts in range(hqt_ref.shape[1] // LANES)]
    for ii in range(n_i):
        for h in range(H):
            i = c * n_i + ii
            row_ref[0, ii, h:h + 1, :] = a1_ref[h, pl.ds(i, 1), :]
            row_ref[1, ii, h:h + 1, :] = e1_ref[h, pl.ds(i, 1), :]

    def activations(ii):
        rows = slice(ii * NK, (ii + 1) * NK)
        act_ref[rows, :] = _gelu(jnp.dot(u_ref[rows, :], hqt_ref[...], preferred_element_type=F32))

    JB = 64
    half = n_i // 2 * NK
    for ii in range(n_i):
        activations(ii)
    del half
    for tl in strips:
        for jb in range(NK // JB):
            g = [[jnp.zeros((8, LANES), F32) for _ in range(JB // 8)] for _ in range(n_i)]
            for h in range(H):
                tau = jnp.broadcast_to(tau_ref[h:h + 1, tl], (8, LANES))
                a = [jnp.broadcast_to(row_ref[0, ii, h:h + 1, tl], (8, LANES)) for ii in range(n_i)]
                e = [jnp.broadcast_to(row_ref[1, ii, h:h + 1, tl], (8, LANES)) for ii in range(n_i)]
                for jv in range(JB // 8):
                    js = slice(jb * JB + jv * 8, jb * JB + jv * 8 + 8)
                    s2 = s2_ref[h, js, tl]
                    e2 = e2_ref[h, js, tl]
                    for ii in range(n_i):
                        g[ii][jv] = g[ii][jv] + jnp.where((a[ii] + s2) >= tau, e[ii] * e2, 0.0)
            for ii in range(n_i):
                rows = slice(ii * NK + jb * JB, ii * NK + (jb + 1) * JB)
                wt_ref[rows, tl] = (jnp.concatenate(g[ii], axis=0) * act_ref[rows, tl]).astype(BF16)
    acc_ref[...] += jnp.dot(vt_ref[...], wt_ref[...], preferred_element_type=F32)

    @pl.when(c == pl.num_programs(1) - 1)
    def _():
        y_ref[...] = acc_ref[...].T


def _peer_dense(hqt, u, vt, a1, e1, s2, e2, tau):
    d, n = hqt.shape
    tm, te = PEER_TOKEN_BLOCK, PEER_EXPERT_BLOCK
    H, NK = PEER_HEADS, PEER_NKEYS
    big = pl.BlockSpec((H, NK, tm), lambda j, c: (0, 0, j))
    return pl.pallas_call(
        _peer_dense_kernel, grid=(n // tm, u.shape[0] // te),
        in_specs=[pl.BlockSpec((d, tm), lambda j, c: (0, j)),
                  pl.BlockSpec((te, d), lambda j, c: (c, 0)),
                  pl.BlockSpec((d, te), lambda j, c: (0, c)),
                  big, big, big, big, pl.BlockSpec((H, tm), lambda j, c: (0, j))],
        out_specs=pl.BlockSpec((tm, d), lambda j, c: (j, 0)),
        out_shape=jax.ShapeDtypeStruct((n, d), F32),
        scratch_shapes=[pltpu.VMEM((d, tm), F32), pltpu.VMEM((te, tm), F32), pltpu.VMEM((te, tm), BF16),
                        pltpu.VMEM((2, te // NK, H, tm), F32)],
        compiler_params=_cparams(("parallel", "arbitrary")))(hqt, u, vt, a1, e1, s2, e2, tau)


def _peer(hqt, wq, k1, k2, u_tab, v_tab):
    sel = _peer_select(hqt, wq.T.astype(BF16), k1.astype(BF16), k2.astype(BF16))
    return _peer_dense(hqt, u_tab.astype(BF16), v_tab.T.astype(BF16), *sel)


def _dn_prep_kernel(z_ref, prev_ref, next_ref, conv_ref, alog_ref, dtb_ref, ab_ref, q_ref, k_ref, v_ref, gb_ref,
                    ext_ref, *, n_ctx_blk, nblk):
    i = pl.program_id(1)
    tm = z_ref.shape[0]
    halo = prev_ref.shape[0]
    pad = DN_CONV // 2
    has_prev = jnp.logical_and(i != 0, i != n_ctx_blk)
    has_next = jnp.logical_and(i != n_ctx_blk - 1, i != nblk - 1)
    ext_ref[0:halo, :] = jnp.where(has_prev, prev_ref[...], 0.0)
    ext_ref[halo:halo + tm, :] = z_ref[...]
    ext_ref[halo + tm:, :] = jnp.where(has_next, next_ref[...], 0.0)
    nq = DN_HEADS * DN_DK
    for j in range(DN_QKV // LANES):
        cs = slice(j * LANES, (j + 1) * LANES)
        acc = conv_ref[0:1, cs] * ext_ref[halo - pad:halo - pad + tm, cs]
        for t in range(1, DN_CONV):
            acc = acc + conv_ref[t:t + 1, cs] * ext_ref[halo - pad + t:halo - pad + t + tm, cs]
        y = _silu(acc)
        if j * LANES < 2 * nq:
            y = y * lax.rsqrt(jnp.sum(y * y, axis=-1, keepdims=True) + EPS)
        if j * LANES < nq:
            q_ref[:, cs] = y * DN_DK ** -0.5
        elif j * LANES < 2 * nq:
            k_ref[:, slice(j * LANES - nq, (j + 1) * LANES - nq)] = y
        else:
            v_ref[:, slice(j * LANES - 2 * nq, (j + 1) * LANES - 2 * nq)] = y
    ab = ab_ref[...]
    nh = 2 * DN_HEADS
    xa = ab[:, 0:nh] + dtb_ref[...]
    softplus = jnp.maximum(xa, 0.0) + jnp.log(1.0 + jnp.exp(-jnp.abs(xa)))
    gb_ref[:, 0:nh] = -jnp.exp(alog_ref[...]) * softplus
    gb_ref[:, nh:2 * nh] = jax.nn.sigmoid(ab[:, nh:2 * nh])


def _dn_prep(geo, n_ctx_blk, z, conv_w, alog, dtb):
    B, nblk, tm, _ = geo
    n = z.shape[0]
    halo = 8
    r = tm // halo
    nh8 = n // halo
    d = DN_HEADS * DN_DK
    prev = pl.BlockSpec((halo, DN_QKV), lambda b, i: (jnp.maximum((b * nblk + i) * r - 1, 0), 0))
    nxt = pl.BlockSpec((halo, DN_QKV), lambda b, i: (jnp.minimum((b * nblk + i + 1) * r, nh8 - 1), 0))
    ab = z[:, DN_QKV + d:DN_QKV + d + 4 * DN_HEADS]
    tok = pl.BlockSpec((tm, d), lambda b, i: (b * nblk + i, 0))
    gbspec = pl.BlockSpec((tm, 4 * DN_HEADS), lambda b, i: (b * nblk + i, 0))
    full = lambda a: pl.BlockSpec(a.shape, lambda b, i: (0,) * a.ndim)
    return pl.pallas_call(
        functools.partial(_dn_prep_kernel, n_ctx_blk=n_ctx_blk, nblk=nblk), grid=(B, nblk),
        in_specs=[pl.BlockSpec((tm, DN_QKV), lambda b, i: (b * nblk + i, 0)), prev, nxt, full(conv_w),
                  full(alog), full(dtb), gbspec],
        out_specs=[tok, tok, tok, gbspec],
        out_shape=[jax.ShapeDtypeStruct((n, d), F32)] * 3 + [jax.ShapeDtypeStruct((n, 4 * DN_HEADS), F32)],
        scratch_shapes=[pltpu.VMEM((tm + 2 * halo, DN_QKV), F32)],
        compiler_params=_cparams(("parallel", "parallel")))(z, z, z, conv_w, alog, dtb, ab)


DN_GROUP = 4


def _bdot(a, b, dims=None):
    a, b = a.astype(BF16), b.astype(BF16)
    if dims is None:
        return jnp.dot(a, b, preferred_element_type=F32)
    return lax.dot_general(a, b, dims, preferred_element_type=F32)


def _block_diag(x, nblk):
    r, n = x.shape
    w = n // nblk
    tall = jnp.concatenate([x] * nblk, axis=0)
    rb = lax.broadcasted_iota(jnp.int32, tall.shape, 0) // r
    lb = lax.broadcasted_iota(jnp.int32, tall.shape, 1) // w
    return jnp.where(rb == lb, tall, jnp.zeros_like(tall))


def _dn_local_kernel(q_ref, k_ref, v_ref, gb_ref, u_ref, wq_ref, akd_ref, gl_ref):
    C, G, nh = DN_CHUNK, DN_GROUP, DN_HEADS
    d_all = nh * LANES
    ri = lax.broadcasted_iota(jnp.int32, (C, G * C), 0)
    ci = lax.broadcasted_iota(jnp.int32, (C, G * C), 1) % C
    eyecat = (ri == ci).astype(F32)
    ones = jnp.ones((C, C), F32)
    r2 = lax.broadcasted_iota(jnp.int32, (C, C), 0)
    c2 = lax.broadcasted_iota(jnp.int32, (C, C), 1)
    eye128 = (lax.broadcasted_iota(jnp.int32, (LANES, LANES), 0)
              == lax.broadcasted_iota(jnp.int32, (LANES, LANES), 1)).astype(BF16)
    gb = gb_ref[...]
    ncol = gb.shape[1]
    sel_row = lax.broadcasted_iota(jnp.int32, (ncol, d_all), 0)
    sel_head = lax.broadcasted_iota(jnp.int32, (ncol, d_all), 1) // LANES
    cat_row = lax.broadcasted_iota(jnp.int32, (ncol, G * C), 0)
    cat_head = lax.broadcasted_iota(jnp.int32, (ncol, G * C), 1) // C
    q, k, v = q_ref[...], k_ref[...], v_ref[...]
    chains = []
    for d in range(2):
        incl = (ri >= ci) if d == 0 else (ri <= ci)
        strict = (ri > ci) if d == 0 else (ri < ci)
        tri = ((r2 >= c2) if d == 0 else (r2 <= c2)).astype(F32)
        last = C - 1 if d == 0 else 0
        gcs = jnp.dot(tri, gb, precision=HI, preferred_element_type=F32)
        spread = lambda m, first: jnp.dot(m, (sel_row == first + sel_head).astype(F32), precision=HI,
                                          preferred_element_type=F32)
        gc = spread(gcs, d * nh)
        bet = spread(gb, 2 * nh + d * nh)
        eg = jnp.exp(gc)
        glrow = gc[last:last + 1, :]
        kbm = k * bet
        kd = k * jnp.exp(glrow - gc)
        wq_ref[d, C:2 * C, :] = (q * eg).astype(BF16)
        gl_ref[d, 0] = jnp.exp(glrow)
        vb = v * bet
        kbe = kbm * eg
        for g in range(nh // G):
            gs = slice(g * G * LANES, (g + 1) * G * LANES)
            cs = slice(g * G * C, (g + 1) * G * C)
            kq = _bdot(jnp.concatenate([kbm[:, gs], q[:, gs]], axis=0), _block_diag(k[:, gs], G), NT)
            gcol = jnp.dot(gcs, (cat_row == d * nh + g * G + cat_head).astype(F32), precision=HI,
                           preferred_element_type=F32)
            grow = jnp.dot(ones, gcol * eyecat, precision=HI, preferred_element_type=F32)
            decay = jnp.where(incl, jnp.exp(jnp.where(incl, gcol - grow, 0.0)), 0.0)
            nm = jnp.where(strict, -kq[0:C] * decay, 0.0)
            akd_ref[d, 0, 0:C, cs] = jnp.where(incl, kq[C:2 * C] * decay, 0.0).astype(BF16)
            kdstack = jnp.concatenate([kd[:, (g * G + j) * LANES:(g * G + j + 1) * LANES] for j in range(G)],
                                      axis=0)
            akd_ref[d, 0, C:C + DN_DK, cs] = _bdot(eye128, kdstack, NT).astype(BF16)
            chains.append((d, gs, nm, vb[:, gs], kbe[:, gs]))
    tinv = [eyecat + nm for (_, _, nm, _, _) in chains]
    p = [nm for (_, _, nm, _, _) in chains]
    pbd = [_block_diag(x, G) for x in p]
    for _ in range(int(math.log2(C)) - 1):
        p = [_bdot(x, y) for x, y in zip(p, pbd)]
        pbd = [_block_diag(x, G) for x in p]
        tinv = [t + _bdot(t, y) for t, y in zip(tinv, pbd)]
    for t, (d, gs, _, vbg, kbeg) in zip(tinv, chains):
        u_ref[d, :, gs] = _bdot(t, _block_diag(vbg, G))
        wq_ref[d, 0:C, gs] = _bdot(t, _block_diag(kbeg, G)).astype(BF16)


def _dn_local(q, k, v, gb):
    n, d = q.shape
    C = DN_CHUNK
    nc = n // C
    tok = pl.BlockSpec((C, d), lambda j: (j, 0))
    return pl.pallas_call(
        _dn_local_kernel, grid=(nc,),
        in_specs=[tok, tok, tok, pl.BlockSpec((C, gb.shape[1]), lambda j: (j, 0))],
        out_specs=[pl.BlockSpec((2, C, d), lambda j: (0, j, 0)),
                   pl.BlockSpec((2, 2 * C, d), lambda j: (0, j, 0)),
                   pl.BlockSpec((2, 1, C + DN_DK, DN_HEADS * C), lambda j: (0, j, 0, 0)),
                   pl.BlockSpec((2, 1, 1, d), lambda j: (0, j, 0, 0))],
        out_shape=[jax.ShapeDtypeStruct((2, n, d), F32),
                   jax.ShapeDtypeStruct((2, 2 * n, d), BF16),
                   jax.ShapeDtypeStruct((2, nc, C + DN_DK, DN_HEADS * C), BF16),
                   jax.ShapeDtypeStruct((2, nc, 1, d), F32)],
        compiler_params=_cparams(("parallel",)))(q, k, v, gb)


def _dn_scan_kernel(*refs):
    ins, (of_ref, or_ref, s_ref) = refs[:8], refs[8:]
    C = DN_CHUNK

    @pl.when(pl.program_id(1) == 0)
    def _():
        s_ref[...] = jnp.zeros_like(s_ref)

    chains = [(d, p) for d in range(2) for p in range(DN_HEADS // 2)]
    outs = (of_ref, or_ref)
    lanes = lambda p: slice(2 * p * LANES, (2 * p + 2) * LANES)
    s2 = [jnp.concatenate([s_ref[d, 2 * p], s_ref[d, 2 * p + 1]], axis=-1) for d, p in chains]
    r = [_bdot(ins[4 * d + 1][0, :, lanes(p)], _block_diag(s, 2)) for (d, p), s in zip(chains, s2)]
    vn = [ins[4 * d][0, :, lanes(p)] - x[0:C] for (d, p), x in zip(chains, r)]
    r2 = [_bdot(ins[4 * d + 2][0, 0, :, 2 * p * C:(2 * p + 2) * C], _block_diag(x, 2))
          for (d, p), x in zip(chains, vn)]
    for (d, p), s, x, y in zip(chains, s2, r, r2):
        outs[d][:, lanes(p)] = x[C:2 * C] + y[0:C]
        snew = s * ins[4 * d + 3][0, 0, :, lanes(p)] + y[C:C + DN_DK]
        s_ref[d, 2 * p] = snew[:, 0:LANES]
        s_ref[d, 2 * p + 1] = snew[:, LANES:2 * LANES]


def _dn_scan(B, T, ctx_len, u, wq, akd, gl):
    _, n, d = u.shape
    C = DN_CHUNK
    nch, ncc = T // C, ctx_len // C
    rpos = lambda c: jnp.where(c < ncc, ncc - 1 - c, nch - 1 - (c - ncc))
    pos = (lambda b, c: b * nch + c, lambda b, c: b * nch + rpos(c))
    in_specs, args = [], []
    for dd in range(2):
        p = pos[dd]
        in_specs += [pl.BlockSpec((1, C, d), lambda b, c, p=p, dd=dd: (dd, p(b, c), 0)),
                     pl.BlockSpec((1, 2 * C, d), lambda b, c, p=p, dd=dd: (dd, p(b, c), 0)),
                     pl.BlockSpec((1, 1, C + DN_DK, DN_HEADS * C), lambda b, c, p=p, dd=dd: (dd, p(b, c), 0, 0)),
                     pl.BlockSpec((1, 1, 1, d), lambda b, c, p=p, dd=dd: (dd, p(b, c), 0, 0))]
        args += [u, wq, akd, gl]
    return pl.pallas_call(
        _dn_scan_kernel, grid=(B, nch), in_specs=in_specs,
        out_specs=[pl.BlockSpec((C, d), lambda b, c: (pos[0](b, c), 0)),
                   pl.BlockSpec((C, d), lambda b, c: (pos[1](b, c), 0))],
        out_shape=[jax.ShapeDtypeStruct((n, d), F32)] * 2,
        scratch_shapes=[pltpu.VMEM((2, DN_HEADS, DN_DK, LANES), F32)],
        compiler_params=_cparams(("arbitrary", "arbitrary")))(*args)


def _final_kernel(x_ref, y_ref, mod_ref, o_ref):
    o_ref[0] = x_ref[...] + mod_ref[0, 5:6, :] * y_ref[...]


def _final(B, T, ctx_len, x, y, mod):
    n, d = x.shape
    tm = TOKEN_BLOCK
    nblk, ncb = T // tm, ctx_len // tm
    tok = pl.BlockSpec((tm, d), lambda b, i: (b * nblk + ncb + i, 0))
    return pl.pallas_call(
        _final_kernel, grid=(B, nblk - ncb),
        in_specs=[tok, tok, pl.BlockSpec((1, 6, d), lambda b, i: (b, 0, 0))],
        out_specs=pl.BlockSpec((1, tm, d), lambda b, i: (b, i, 0)),
        out_shape=jax.ShapeDtypeStruct((B, T - ctx_len, d), F32),
        compiler_params=_cparams(("parallel", "parallel")))(x, y, mod)


def _rope_tables(seq, ctx_len):
    rows = seq // GRID_W
    r = jnp.broadcast_to(jnp.arange(rows, dtype=F32)[:, None], (rows, GRID_W)).reshape(-1)
    cl = jnp.broadcast_to(jnp.arange(GRID_W, dtype=F32)[None, :], (rows, GRID_W)).reshape(-1)

    def angles(rot_dim):
        nf = rot_dim // 4
        inv = ROPE_BASE ** (-jnp.arange(nf, dtype=F32) / nf)
        ang = jnp.concatenate([r[:, None] * inv, cl[:, None] * inv], axis=-1)
        ang = jnp.concatenate([jnp.zeros((ctx_len, rot_dim // 2), F32), ang], axis=0)
        return jnp.cos(ang), jnp.sin(ang)

    T = seq + ctx_len
    ca, sa = angles(DIFF_DIM)
    z = jnp.zeros_like(sa)
    ta = jnp.stack([jnp.concatenate([ca, ca] * 2, -1), jnp.concatenate([-sa, z] * 2, -1),
                    jnp.concatenate([z, sa] * 2, -1)])
    cb, sb = angles(MLA_ROPE)
    one, zn, zt = jnp.ones((T, MLA_NOPE), F32), jnp.zeros((T, MLA_NOPE), F32), jnp.zeros((T, LANES - MLA_QK), F32)
    zb = jnp.zeros_like(sb)
    tb = jnp.stack([jnp.concatenate([one, cb, cb, 1.0 + zt], -1), jnp.concatenate([zn, -sb, zb, zt], -1),
                    jnp.concatenate([zn, zb, sb, zt], -1)])
    return ta, tb


def _pad_heads(w, heads, width):
    lead = w.shape[:-1]
    w = w.reshape(lead + (heads, width))
    return jnp.pad(w, [(0, 0)] * len(lead) + [(0, 0), (0, LANES - width)]).reshape(lead + (heads * LANES,))


def _block_diag_ones(n, group):
    idx = np.arange(n) // group
    return jnp.asarray(idx[:, None] == idx[None, :], dtype=BF16)


def kernel(x, c, ctx, c_ctx, ada_w, ada_b, ev_w_in, ev_w_out, diff_q_norm, diff_k_norm, diff_lam_q1, diff_lam_k1, diff_lam_q2, diff_lam_k2, diff_subln, mla_cq_norm, mla_ckv_norm, mla_w_uq, mla_w_ukv, mla_q_norm, mla_k_norm, dn_w_in, dn_conv, dn_a_log, dn_dt_bias, dn_o_norm, dn_w_out, peer_wq, peer_k1, peer_k2, peer_u, peer_v):
    B, S, D = x.shape
    CTX = ctx.shape[1]
    T = CTX + S
    depth = ada_w.shape[0]
    tm = TOKEN_BLOCK
    assert CTX % tm == 0 and S % tm == 0 and (B * T) % PEER_TOKEN_BLOCK == 0 and B + 1 <= 8
    nblk, ncb = T // tm, CTX // tm
    geo = (B, nblk, tm, lambda b, i: jnp.where(i < ncb, B, b))

    cc = jnp.zeros((8, D), F32).at[:B].set(c).at[B].set(c_ctx)
    mods = _adaln(cc, ada_w, ada_b).reshape(depth, 8, 6, D)
    xs = jnp.concatenate([ctx, x], axis=1).reshape(B * T, D)
    ropea, ropeb = _rope_tables(S, CTX)
    bd64, bd128 = _block_diag_ones(HEAD_W, DIFF_DIM), _block_diag_ones(HEAD_W, LANES)

    y = None
    for l in range(depth):
        i = l // 2
        mod = mods[l]
        gmod = mods[l - 1] if l else None
        if l % 2 == 0:
            lam_init = 0.8 - 0.6 * math.exp(-0.3 * l)
            w_in = ev_w_in[i]
            o = 3 * HEAD_W + MLA_Q_RANK + MLA_KV_RANK
            kr_rep = jnp.pad(jnp.broadcast_to(w_in[:, None, o:o + MLA_ROPE], (D, MLA_HEADS, MLA_ROPE)),
                             ((0, 0), (0, 0), (MLA_NOPE, LANES - MLA_QK))).reshape(D, HEAD_W)
            w_cat = jnp.concatenate([w_in[:, :o], kr_rep], axis=1).astype(BF16)
            res = _modmm(geo, xs, mod, w_cat, y, gmod)
            (xs, z) = res if y is not None else (xs, res)
            ukv = mla_w_ukv[i].reshape(MLA_KV_RANK, MLA_HEADS, MLA_NOPE + MLA_VDIM)
            gains = jnp.stack([jnp.tile(diff_q_norm[i], HEAD_W // DIFF_DIM), jnp.tile(diff_k_norm[i], HEAD_W // DIFF_DIM),
                               jnp.tile(jnp.pad(mla_q_norm[i], (0, LANES - MLA_QK)), MLA_HEADS),
                               jnp.tile(jnp.pad(mla_k_norm[i], (0, LANES - MLA_QK)), MLA_HEADS)])
            qa, ka, va, qb, kb, vb = _even_prep(
                geo, z, ropea, ropeb, gains, mla_cq_norm[i][None, :], mla_ckv_norm[i][None, :],
                _pad_heads(mla_w_uq[i], MLA_HEADS, MLA_QK).astype(BF16),
                _pad_heads(ukv[:, :, :MLA_NOPE].reshape(MLA_KV_RANK, -1), MLA_HEADS, MLA_NOPE).astype(BF16),
                ukv[:, :, MLA_NOPE:].reshape(MLA_KV_RANK, -1).astype(BF16), bd64, bd128)
            lam_vecs = jnp.stack([diff_lam_q1[i], diff_lam_k1[i], diff_lam_q2[i], diff_lam_k2[i]])
            oa = _attention(B, T, CTX, qa, ka, va, lam_vecs, diff_subln[i][None, :], lam_init)
            ob = _attention(B, T, CTX, qb, kb, vb)
            xs, hqt = _mixer_out(geo, _even_out_kernel, [(oa, HEAD_W, 0), (ob, HEAD_W, 0)], [], xs, mod,
                                 ev_w_out[i].astype(BF16))
        else:
            nin = dn_w_in.shape[2]
            w_in = jnp.pad(dn_w_in[i], ((0, 0), (0, -nin % LANES))).astype(BF16)
            res = _modmm(geo, xs, mod, w_in, y, gmod)
            (xs, z) = res if y is not None else (xs, res)
            q, k, v, gb = _dn_prep(geo, ncb, z, dn_conv[i], dn_a_log[i].reshape(1, -1),
                                   dn_dt_bias[i].reshape(1, -1))
            of, orv = _dn_scan(B, T, CTX, *_dn_local(q, k, v, gb))
            xs, hqt = _mixer_out(geo, _dn_out_kernel,
                                 [(of, D, 0), (orv, D, 0), (z, D, DN_QKV // D)], [dn_o_norm[i][None, :]], xs, mod,
                                 dn_w_out[i].astype(BF16))
        y = _peer(hqt, peer_wq[l], peer_k1[l], peer_k2[l], peer_u[l], peer_v[l])
    return _final(B, T, CTX, xs, y, mods[depth - 1])
```

```python
import functools
import math

import jax
import jax.numpy as jnp
import numpy as np
from jax import lax
from jax.experimental import pallas as pl
from jax.experimental.pallas import tpu as pltpu

F32 = jnp.float32
BF16 = jnp.bfloat16
HI = lax.Precision.HIGHEST
NT = (((1,), (1,)), ((), ()))

EPS = 1e-6
LOG2E = math.log2(math.e)
ROPE_BASE = 10000.0
GRID_W = 64
LANES = 128
TOKEN_BLOCK = 256
VMEM_LIMIT = 56 * 1024 * 1024

DIFF_HEADS, DIFF_DIM = 4, 64
MLA_HEADS, MLA_Q_RANK, MLA_KV_RANK, MLA_NOPE, MLA_ROPE, MLA_VDIM = 4, 256, 128, 64, 32, 128
MLA_QK = MLA_NOPE + MLA_ROPE
HEAD_W = DIFF_HEADS * LANES
DN_HEADS, DN_DK, DN_CONV, DN_CHUNK = 8, 128, 5, 64
DN_QKV = 3 * DN_HEADS * DN_DK
PEER_HEADS, PEER_NKEYS, PEER_TOPK = 8, 128, 16
PEER_CAND_ROWS = 80
PEER_TOKEN_BLOCK = 512
PEER_EXPERT_BLOCK = 1024


def _cparams(sem):
    return pltpu.CompilerParams(dimension_semantics=sem, vmem_limit_bytes=VMEM_LIMIT)


def _rms_rows(x):
    return x * lax.rsqrt(jnp.mean(x * x, axis=-1, keepdims=True) + EPS)


def _silu(x):
    return x * jax.nn.sigmoid(x)


def _group_sum(sq, bd):
    hi = sq.astype(BF16)
    lo = (sq - hi.astype(F32)).astype(BF16)
    return jnp.dot(hi, bd, preferred_element_type=F32) + jnp.dot(lo, bd, preferred_element_type=F32)


def _adaln_kernel(c_ref, w_ref, b_ref, o_ref):
    s = _silu(c_ref[...])
    o_ref[0] = jnp.dot(s, w_ref[0], precision=HI, preferred_element_type=F32) + b_ref[0]


def _adaln(cc, ada_w, ada_b):
    depth, d, n = ada_w.shape
    tn = 1536
    return pl.pallas_call(
        _adaln_kernel,
        grid=(depth, n // tn),
        in_specs=[pl.BlockSpec((8, d), lambda l, j: (0, 0)),
                  pl.BlockSpec((1, d, tn), lambda l, j: (l, 0, j)),
                  pl.BlockSpec((1, 1, tn), lambda l, j: (l, 0, j))],
        out_specs=pl.BlockSpec((1, 8, tn), lambda l, j: (l, 0, j)),
        out_shape=jax.ShapeDtypeStruct((depth, 8, n), F32),
        compiler_params=_cparams(("arbitrary", "arbitrary")),
    )(cc, ada_w, ada_b.reshape(depth, 1, n))


def _modmm_kernel(*refs, has_y):
    if has_y:
        x_ref, y_ref, gmod_ref, mod_ref, w_ref, xo_ref, z_ref = refs
        x = x_ref[...] + gmod_ref[0, 5:6, :] * y_ref[...]
        xo_ref[...] = x
    else:
        x_ref, mod_ref, w_ref, z_ref = refs
        x = x_ref[...]
    h = _rms_rows(x) * (1.0 + mod_ref[0, 1:2, :]) + mod_ref[0, 0:1, :]
    z_ref[...] = jnp.dot(h.astype(BF16), w_ref[...], preferred_element_type=F32)


def _modmm(geo, x, mod, w, y=None, gmod=None):
    B, nblk, tm, mrow = geo
    n, d = x.shape
    nout = w.shape[1]
    tok = pl.BlockSpec((tm, d), lambda b, i: (b * nblk + i, 0))
    modspec = pl.BlockSpec((1, 6, d), lambda b, i: (mrow(b, i), 0, 0))
    wspec = pl.BlockSpec((d, nout), lambda b, i: (0, 0))
    zspec = pl.BlockSpec((tm, nout), lambda b, i: (b * nblk + i, 0))
    zshape = jax.ShapeDtypeStruct((n, nout), F32)
    if y is None:
        return pl.pallas_call(
            functools.partial(_modmm_kernel, has_y=False), grid=(B, nblk),
            in_specs=[tok, modspec, wspec], out_specs=zspec, out_shape=zshape,
            compiler_params=_cparams(("parallel", "parallel")))(x, mod, w)
    return pl.pallas_call(
        functools.partial(_modmm_kernel, has_y=True), grid=(B, nblk),
        in_specs=[tok, tok, modspec, modspec, wspec], out_specs=[tok, zspec],
        out_shape=[jax.ShapeDtypeStruct((n, d), F32), zshape],
        compiler_params=_cparams(("parallel", "parallel")))(x, y, gmod, mod, w)


def _rope_lanes(v, tab_ref, half):
    rep = v.shape[-1] // LANES
    c = jnp.concatenate([tab_ref[0]] * rep, axis=-1)
    sm = jnp.concatenate([tab_ref[1]] * rep, axis=-1)
    sp = jnp.concatenate([tab_ref[2]] * rep, axis=-1)
    n = v.shape[-1]
    return v * c + pltpu.roll(v, n - half, 1) * sm + pltpu.roll(v, half, 1) * sp


def _even_prep_kernel(z_ref, ropea_ref, ropeb_ref, gains_ref, cqn_ref, ckvn_ref, wuq_ref, wuk_ref, wuv_ref,
                      bd64_ref, bd128_ref, qa_ref, ka_ref, va_ref, qb_ref, kb_ref, vb_ref):
    W = HEAD_W
    bd64 = bd64_ref[...]
    bd128 = bd128_ref[...]

    def norm_groups(v, bd, width, gain):
        ms = _group_sum(v * v, bd) * (1.0 / width)
        return v * lax.rsqrt(ms + EPS) * gain

    qa = norm_groups(z_ref[:, 0:W], bd64, DIFF_DIM, gains_ref[0:1, :])
    ka = norm_groups(z_ref[:, W:2 * W], bd64, DIFF_DIM, gains_ref[1:2, :])
    qa_ref[...] = (_rope_lanes(qa, ropea_ref, DIFF_DIM // 2) * (DIFF_DIM ** -0.5 * LOG2E)).astype(BF16)
    ka_ref[...] = _rope_lanes(ka, ropea_ref, DIFF_DIM // 2).astype(BF16)
    va_ref[...] = z_ref[:, 2 * W:3 * W].astype(BF16)

    o = 3 * W
    cq = (_rms_rows(z_ref[:, o:o + MLA_Q_RANK]) * cqn_ref[...]).astype(BF16)
    o += MLA_Q_RANK
    ckv = (_rms_rows(z_ref[:, o:o + MLA_KV_RANK]) * ckvn_ref[...]).astype(BF16)
    o += MLA_KV_RANK
    kr = z_ref[:, o:o + W]
    qb = jnp.dot(cq, wuq_ref[...], preferred_element_type=F32)
    kb = jnp.dot(ckv, wuk_ref[...], preferred_element_type=F32) + kr
    qb = norm_groups(qb, bd128, MLA_QK, gains_ref[2:3, :])
    kb = norm_groups(kb, bd128, MLA_QK, gains_ref[3:4, :])
    qb_ref[...] = (_rope_lanes(qb, ropeb_ref, MLA_ROPE // 2) * (MLA_QK ** -0.5 * LOG2E)).astype(BF16)
    kb_ref[...] = _rope_lanes(kb, ropeb_ref, MLA_ROPE // 2).astype(BF16)
    vb_ref[...] = jnp.dot(ckv, wuv_ref[...], preferred_element_type=F32).astype(BF16)


def _even_prep(geo, z, ropea, ropeb, gains, cqn, ckvn, wuq, wuk, wuv, bd64, bd128):
    B, nblk, tm, _ = geo
    n = z.shape[0]
    W = HEAD_W
    full = lambda a: pl.BlockSpec(a.shape, lambda b, i: (0,) * a.ndim)
    rope = pl.BlockSpec((3, tm, LANES), lambda b, i: (0, i, 0))
    out = pl.BlockSpec((tm, W), lambda b, i: (b * nblk + i, 0))
    return pl.pallas_call(
        _even_prep_kernel, grid=(B, nblk),
        in_specs=[pl.BlockSpec((tm, z.shape[1]), lambda b, i: (b * nblk + i, 0)), rope, rope, full(gains),
                  full(cqn), full(ckvn), full(wuq), full(wuk), full(wuv), full(bd64), full(bd128)],
        out_specs=[out] * 6, out_shape=[jax.ShapeDtypeStruct((n, W), BF16)] * 6,
        compiler_params=_cparams(("parallel", "parallel")))(z, ropea, ropeb, gains, cqn, ckvn, wuq, wuk, wuv,
                                                            bd64, bd128)


def _attn_kernel(*refs, diff, tk, row_split, n_ctx_q, ctx_len, n_keys, lam_init):
    if diff:
        q_ref, k_ref, v_ref, lam_ref, subln_ref, o_ref, s_ref = refs
    else:
        q_ref, k_ref, v_ref, o_ref, s_ref = refs
    qi = pl.program_id(2)
    tr = q_ref.shape[0] // row_split
    qs = []
    for r in range(row_split):
        q = q_ref[r * tr:(r + 1) * tr, :]
        if diff:
            lane = lax.broadcasted_iota(jnp.int32, q.shape, 1)
            zero = jnp.zeros_like(q)
            qs += [jnp.where(lane < DIFF_DIM, q, zero), jnp.where(lane >= DIFF_DIM, q, zero)]
        else:
            qs.append(q)
    nsub = len(qs) // row_split

    def scores(slot, start, size):
        k = k_ref[pl.ds(start, size), :]
        s = [lax.dot_general(qq, k, NT, preferred_element_type=F32) for qq in qs]
        for j, x in enumerate(s):
            s_ref[slot, j, :, 0:size] = x
        return tuple(jnp.max(x, axis=-1, keepdims=True) for x in s)

    def update(slot, start, size, mx, carry):
        v = v_ref[pl.ds(start, size), :]
        mn = [jnp.maximum(m, x) for (m, _, _), x in zip(carry, mx)]
        p = [jnp.exp2(s_ref[slot, j, :, 0:size] - y) for j, y in enumerate(mn)]
        alpha = [jnp.exp2(m - y) for (m, _, _), y in zip(carry, mn)]
        pv = [jnp.dot(x.astype(BF16), v, preferred_element_type=F32) for x in p]
        return tuple((y, a * l + jnp.sum(x, axis=-1, keepdims=True), a * acc + z)
                     for (_, l, acc), y, a, x, z in zip(carry, mn, alpha, p, pv))

    init = tuple((jnp.full((tr, 1), -1e30, F32), jnp.zeros((tr, 1), F32), jnp.zeros((tr, LANES), F32))
                 for _ in qs)

    def finish(carry):
        outs = [acc / l for (_, l, acc) in carry]
        for r in range(row_split):
            if diff:
                lam = (jnp.exp(jnp.sum(lam_ref[0:1, :] * lam_ref[1:2, :], axis=-1, keepdims=True))
                       - jnp.exp(jnp.sum(lam_ref[2:3, :] * lam_ref[3:4, :], axis=-1, keepdims=True)) + lam_init)
                o = _rms_rows(outs[r * nsub] - lam * outs[r * nsub + 1]) * subln_ref[...] * (1.0 - lam_init)
            else:
                o = outs[r]
            o_ref[r * tr:(r + 1) * tr, :] = o.astype(o_ref.dtype)

    @pl.when(qi < n_ctx_q)
    def _():
        finish(update(0, 0, ctx_len, scores(0, 0, ctx_len), init))

    @pl.when(qi >= n_ctx_q)
    def _():
        n = n_keys // tk
        at = lambda c: pl.multiple_of(c * tk, tk)

        def pair(j, state):
            carry, mx0 = state
            mx1 = scores(1, at(2 * j + 1), tk)
            carry = update(0, at(2 * j), tk, mx0, carry)
            mx0 = scores(0, at(2 * j + 2), tk)
            carry = update(1, at(2 * j + 1), tk, mx1, carry)
            return carry, mx0

        carry, mx0 = lax.fori_loop(0, (n - 1) // 2, pair, (init, scores(0, 0, tk)))
        if n % 2 == 0:
            mx1 = scores(1, (n - 1) * tk, tk)
            carry = update(0, (n - 2) * tk, tk, mx0, carry)
            carry = update(1, (n - 1) * tk, tk, mx1, carry)
        else:
            carry = update(0, (n - 1) * tk, tk, mx0, carry)
        finish(carry)


def _attention(B, T, ctx_len, q, k, v, lam_vecs=None, subln=None, lam_init=0.0):
    diff = lam_vecs is not None
    tq = TOKEN_BLOCK
    tk = 768 if T % 768 == 0 else TOKEN_BLOCK
    nq = T // tq
    heads = q.shape[1] // LANES
    qspec = pl.BlockSpec((tq, LANES), lambda b, h, i: (b * nq + i, h))
    kvspec = pl.BlockSpec((T, LANES), lambda b, h, i: (b, h))
    in_specs = [qspec, kvspec, kvspec]
    args = [q, k, v]
    if diff:
        in_specs += [pl.BlockSpec(lam_vecs.shape, lambda b, h, i: (0, 0)),
                     pl.BlockSpec(subln.shape, lambda b, h, i: (0, 0))]
        args += [lam_vecs, subln]
    row_split = 1 if diff else 2
    return pl.pallas_call(
        functools.partial(_attn_kernel, diff=diff, tk=tk, row_split=row_split, n_ctx_q=ctx_len // tq,
                          ctx_len=ctx_len, n_keys=T, lam_init=lam_init),
        grid=(B, heads, nq), in_specs=in_specs, out_specs=qspec,
        out_shape=jax.ShapeDtypeStruct(q.shape, BF16),
        scratch_shapes=[pltpu.VMEM((2, 2, tq // row_split, max(tk, ctx_len)), F32)],
        compiler_params=_cparams(("parallel", "parallel", "arbitrary")))(*args)


def _residual_tail(x, y, mod_ref, xo_ref, hqt_ref):
    xn = x + mod_ref[0, 2:3, :] * y
    xo_ref[...] = xn
    hq = _rms_rows(xn) * (1.0 + mod_ref[0, 4:5, :]) + mod_ref[0, 3:4, :]
    hqt_ref[...] = hq.T.astype(BF16)


def _even_out_kernel(oa_ref, ob_ref, x_ref, mod_ref, wo_ref, xo_ref, hqt_ref):
    W = HEAD_W
    y = (jnp.dot(oa_ref[...], wo_ref[0:W, :], preferred_element_type=F32)
         + jnp.dot(ob_ref[...], wo_ref[W:2 * W, :], preferred_element_type=F32))
    _residual_tail(x_ref[...], y, mod_ref, xo_ref, hqt_ref)


def _dn_out_kernel(of_ref, or_ref, zg_ref, onorm_ref, x_ref, mod_ref, wo_ref, xo_ref, hqt_ref):
    parts = []
    for h in range(DN_HEADS):
        hs = slice(h * LANES, (h + 1) * LANES)
        o = of_ref[:, hs] + or_ref[:, hs]
        parts.append((_rms_rows(o) * onorm_ref[...] * _silu(zg_ref[:, hs])).astype(BF16))
    y = jnp.dot(jnp.concatenate(parts, axis=-1), wo_ref[...], preferred_element_type=F32)
    _residual_tail(x_ref[...], y, mod_ref, xo_ref, hqt_ref)


def _mixer_out(geo, kernel_fn, token_args, small_args, x, mod, wo):
    B, nblk, tm, mrow = geo
    n, d = x.shape
    tokspec = lambda a, col: pl.BlockSpec((tm, a[1]), lambda b, i: (b * nblk + i, col))
    in_specs = [tokspec((a, w), col) for (a, w, col) in token_args]
    in_specs += [pl.BlockSpec(a.shape, lambda b, i: (0,) * a.ndim) for a in small_args]
    in_specs += [pl.BlockSpec((tm, d), lambda b, i: (b * nblk + i, 0)),
                 pl.BlockSpec((1, 6, d), lambda b, i: (mrow(b, i), 0, 0)),
                 pl.BlockSpec(wo.shape, lambda b, i: (0, 0))]
    return pl.pallas_call(
        kernel_fn, grid=(B, nblk), in_specs=in_specs,
        out_specs=[pl.BlockSpec((tm, d), lambda b, i: (b * nblk + i, 0)),
                   pl.BlockSpec((d, tm), lambda b, i: (0, b * nblk + i))],
        out_shape=[jax.ShapeDtypeStruct((n, d), F32), jax.ShapeDtypeStruct((d, n), BF16)],
        compiler_params=_cparams(("parallel", "parallel")))(
            *[a for (a, _, _) in token_args], *small_args, x, mod, wo)


def _peer_select_kernel(hqt_ref, wqt_ref, k1_ref, k2_ref, a1_ref, e1_ref, s2_ref, e2_ref, tau_ref, work_ref, top_ref,
                        cand_ref):
    K, H = PEER_TOPK, PEER_HEADS
    qt = jnp.dot(wqt_ref[...], hqt_ref[...], preferred_element_type=F32)
    dk = k1_ref.shape[1]
    for h in range(H):
        q1 = qt[(2 * h) * dk:(2 * h + 1) * dk, :].astype(BF16)
        q2 = qt[(2 * h + 1) * dk:(2 * h + 2) * dk, :].astype(BF16)
        s1 = jnp.dot(k1_ref[...], q1, preferred_element_type=F32)
        s2 = jnp.dot(k2_ref[...], q2, preferred_element_type=F32)
        a1_ref[h] = s1
        s2_ref[h] = s2
        work_ref[2 * h] = s1
        work_ref[2 * h + 1] = s2

    def extract(ref, count):
        s = [ref[a] for a in range(count)]
        m = [jnp.max(x, axis=0, keepdims=True) for x in s]
        for a in range(count):
            ref[a] = jnp.where(s[a] == m[a], -jnp.inf, s[a])
        return m

    def top_round(r, carry):
        for a, m in enumerate(extract(work_ref, 2 * H)):
            top_ref[a, pl.ds(r, 1), :] = m
        return carry

    lax.fori_loop(0, K, top_round, 0)

    def candidates(h):
        v1, v2 = top_ref[2 * h], top_ref[2 * h + 1]
        row8 = lax.broadcasted_iota(jnp.int32, (8, v1.shape[1]), 0)
        pieces = [v1[0:1, :] + v2, v1[1:2, :] + v2[0:8, :]]
        pieces += [jnp.where(row8 < K // (r1 + 1), v1[r1:r1 + 1, :] + v2[0:8, :], -jnp.inf) for r1 in range(2, 8)]
        pieces.append(v1[8:16, :] + v2[0:1, :])
        return jnp.concatenate(pieces, axis=0)

    for h in range(H):
        cand_ref[h] = candidates(h)
    tau = lax.fori_loop(0, K, lambda r, carry: tuple(extract(cand_ref, H)),
                        tuple(jnp.zeros((1, tau_ref.shape[1]), F32) for _ in range(H)))
    for h in range(H):
        cand = candidates(h)
        m1, m2 = top_ref[2 * h, 0:1, :], top_ref[2 * h + 1, 0:1, :]
        zsum = jnp.sum(jnp.where(cand >= tau[h], jnp.exp(cand - (m1 + m2)), 0.0), axis=0, keepdims=True)
        e1_ref[h] = jnp.exp(a1_ref[h] - m1) / zsum
        e2_ref[h] = jnp.exp(s2_ref[h] - m2)
        tau_ref[h:h + 1, :] = tau[h]


def _peer_select(hqt, wqt, k1, k2):
    d, n = hqt.shape
    tm = TOKEN_BLOCK
    H, NK = PEER_HEADS, PEER_NKEYS
    big = pl.BlockSpec((H, NK, tm), lambda j: (0, 0, j))
    bigshape = jax.ShapeDtypeStruct((H, NK, n), F32)
    full = lambda a: pl.BlockSpec(a.shape, lambda j: (0,) * a.ndim)
    return pl.pallas_call(
        _peer_select_kernel, grid=(n // tm,),
        in_specs=[pl.BlockSpec((d, tm), lambda j: (0, j)), full(wqt), full(k1), full(k2)],
        out_specs=[big, big, big, big, pl.BlockSpec((H, tm), lambda j: (0, j))],
        out_shape=[bigshape] * 4 + [jax.ShapeDtypeStruct((H, n), F32)],
        scratch_shapes=[pltpu.VMEM((2 * H, NK, tm), F32), pltpu.VMEM((2 * H, PEER_TOPK, tm), F32),
                        pltpu.VMEM((H, PEER_CAND_ROWS, tm), F32)],
        compiler_params=_cparams(("parallel",)))(hqt, wqt, k1, k2)


def _gelu(a):
    return 0.5 * a * (1.0 + lax.erf(a * np.float32(math.sqrt(0.5))))


def _peer_dense_kernel(hqt_ref, u_ref, vt_ref, a1_ref, e1_ref, s2_ref, e2_ref, tau_ref, y_ref, acc_ref, act_ref,
                       wt_ref, row_ref):
    c = pl.program_id(1)
    NK = PEER_NKEYS
    n_i = PEER_EXPERT_BLOCK // NK

    @pl.when(c == 0)
    def _():
        acc_ref[...] = jnp.zeros_like(acc_ref)

    H = PEER_HEADS
    strips = [slice(ts * LANES, (ts + 1) * LANES) for ts in range(hqt_ref.shape[1] // LANES)]
    for ii in range(n_i):
        for h in range(H):
            i = c * n_i + ii
            row_ref[0, ii, h:h + 1, :] = a1_ref[h, pl.ds(i, 1), :]
            row_ref[1, ii, h:h + 1, :] = e1_ref[h, pl.ds(i, 1), :]

    act_ref[...] = _gelu(jnp.dot(u_ref[...], hqt_ref[...], preferred_element_type=F32))
    for tl in strips:
        for ii in range(n_i):
            g = jnp.zeros((NK, LANES), F32)
            for h in range(H):
                sel = (row_ref[0, ii, h:h + 1, tl] + s2_ref[h, :, tl]) >= tau_ref[h:h + 1, tl]
                g = g + jnp.where(sel, row_ref[1, ii, h:h + 1, tl] * e2_ref[h, :, tl], 0.0)
            rows = slice(ii * NK, (ii + 1) * NK)
            wt_ref[rows, tl] = (g * act_ref[rows, tl]).astype(BF16)
    acc_ref[...] += jnp.dot(vt_ref[...], wt_ref[...], preferred_element_type=F32)

    @pl.when(c == pl.num_programs(1) - 1)
    def _():
        y_ref[...] = acc_ref[...].T


def _peer_dense(hqt, u, vt, a1, e1, s2, e2, tau):
    d, n = hqt.shape
    tm, te = PEER_TOKEN_BLOCK, PEER_EXPERT_BLOCK
    H, NK = PEER_HEADS, PEER_NKEYS
    big = pl.BlockSpec((H, NK, tm), lambda j, c: (0, 0, j))
    return pl.pallas_call(
        _peer_dense_kernel, grid=(n // tm, u.shape[0] // te),
        in_specs=[pl.BlockSpec((d, tm), lambda j, c: (0, j)),
                  pl.BlockSpec((te, d), lambda j, c: (c, 0)),
                  pl.BlockSpec((d, te), lambda j, c: (0, c)),
                  big, big, big, big, pl.BlockSpec((H, tm), lambda j, c: (0, j))],
        out_specs=pl.BlockSpec((tm, d), lambda j, c: (j, 0)),
        out_shape=jax.ShapeDtypeStruct((n, d), F32),
        scratch_shapes=[pltpu.VMEM((d, tm), F32), pltpu.VMEM((te, tm), F32), pltpu.VMEM((te, tm), BF16),
                        pltpu.VMEM((2, te // NK, H, tm), F32)],
        compiler_params=_cparams(("parallel", "arbitrary")))(hqt, u, vt, a1, e1, s2, e2, tau)


def _peer(hqt, wq, k1, k2, u_tab, v_tab):
    sel = _peer_select(hqt, wq.T.astype(BF16), k1.astype(BF16), k2.astype(BF16))
    return _peer_dense(hqt, u_tab.astype(BF16), v_tab.T.astype(BF16), *sel)


def _dn_prep_kernel(z_ref, prev_ref, next_ref, conv_ref, alog_ref, dtb_ref, ab_ref, q_ref, k_ref, v_ref, gb_ref,
                    ext_ref, *, n_ctx_blk, nblk):
    i = pl.program_id(1)
    tm = z_ref.shape[0]
    halo = prev_ref.shape[0]
    pad = DN_CONV // 2
    has_prev = jnp.logical_and(i != 0, i != n_ctx_blk)
    has_next = jnp.logical_and(i != n_ctx_blk - 1, i != nblk - 1)
    ext_ref[0:halo, :] = jnp.where(has_prev, prev_ref[...], 0.0)
    ext_ref[halo:halo + tm, :] = z_ref[...]
    ext_ref[halo + tm:, :] = jnp.where(has_next, next_ref[...], 0.0)
    nq = DN_HEADS * DN_DK
    for j in range(DN_QKV // LANES):
        cs = slice(j * LANES, (j + 1) * LANES)
        acc = conv_ref[0:1, cs] * ext_ref[halo - pad:halo - pad + tm, cs]
        for t in range(1, DN_CONV):
            acc = acc + conv_ref[t:t + 1, cs] * ext_ref[halo - pad + t:halo - pad + t + tm, cs]
        y = _silu(acc)
        if j * LANES < 2 * nq:
            y = y * lax.rsqrt(jnp.sum(y * y, axis=-1, keepdims=True) + EPS)
        if j * LANES < nq:
            q_ref[:, cs] = y * DN_DK ** -0.5
        elif j * LANES < 2 * nq:
            k_ref[:, slice(j * LANES - nq, (j + 1) * LANES - nq)] = y
        else:
            v_ref[:, slice(j * LANES - 2 * nq, (j + 1) * LANES - 2 * nq)] = y
    ab = ab_ref[...]
    nh = 2 * DN_HEADS
    xa = ab[:, 0:nh] + dtb_ref[...]
    softplus = jnp.maximum(xa, 0.0) + jnp.log(1.0 + jnp.exp(-jnp.abs(xa)))
    gb_ref[:, 0:nh] = -jnp.exp(alog_ref[...]) * softplus
    gb_ref[:, nh:2 * nh] = jax.nn.sigmoid(ab[:, nh:2 * nh])


def _dn_prep(geo, n_ctx_blk, z, conv_w, alog, dtb):
    B, nblk, tm, _ = geo
    n = z.shape[0]
    halo = 8
    r = tm // halo
    nh8 = n // halo
    d = DN_HEADS * DN_DK
    prev = pl.BlockSpec((halo, DN_QKV), lambda b, i: (jnp.maximum((b * nblk + i) * r - 1, 0), 0))
    nxt = pl.BlockSpec((halo, DN_QKV), lambda b, i: (jnp.minimum((b * nblk + i + 1) * r, nh8 - 1), 0))
    ab = z[:, DN_QKV + d:DN_QKV + d + 4 * DN_HEADS]
    tok = pl.BlockSpec((tm, d), lambda b, i: (b * nblk + i, 0))
    gbspec = pl.BlockSpec((tm, 4 * DN_HEADS), lambda b, i: (b * nblk + i, 0))
    full = lambda a: pl.BlockSpec(a.shape, lambda b, i: (0,) * a.ndim)
    return pl.pallas_call(
        functools.partial(_dn_prep_kernel, n_ctx_blk=n_ctx_blk, nblk=nblk), grid=(B, nblk),
        in_specs=[pl.BlockSpec((tm, DN_QKV), lambda b, i: (b * nblk + i, 0)), prev, nxt, full(conv_w),
                  full(alog), full(dtb), gbspec],
        out_specs=[tok, tok, tok, gbspec],
        out_shape=[jax.ShapeDtypeStruct((n, d), F32)] * 3 + [jax.ShapeDtypeStruct((n, 4 * DN_HEADS), F32)],
        scratch_shapes=[pltpu.VMEM((tm + 2 * halo, DN_QKV), F32)],
        compiler_params=_cparams(("parallel", "parallel")))(z, z, z, conv_w, alog, dtb, ab)


DN_GROUP = 4


def _bdot(a, b, dims=None):
    a, b = a.astype(BF16), b.astype(BF16)
    if dims is None:
        return jnp.dot(a, b, preferred_element_type=F32)
    return lax.dot_general(a, b, dims, preferred_element_type=F32)


def _block_diag(x, nblk):
    r, n = x.shape
    w = n // nblk
    tall = jnp.concatenate([x] * nblk, axis=0)
    rb = lax.broadcasted_iota(jnp.int32, tall.shape, 0) // r
    lb = lax.broadcasted_iota(jnp.int32, tall.shape, 1) // w
    return jnp.where(rb == lb, tall, jnp.zeros_like(tall))


def _dn_local_kernel(q_ref, k_ref, v_ref, gb_ref, u_ref, wq_ref, akd_ref, gl_ref):
    C, G, nh = DN_CHUNK, DN_GROUP, DN_HEADS
    d_all = nh * LANES
    ri = lax.broadcasted_iota(jnp.int32, (C, G * C), 0)
    ci = lax.broadcasted_iota(jnp.int32, (C, G * C), 1) % C
    eyecat = (ri == ci).astype(F32)
    ones = jnp.ones((C, C), F32)
    r2 = lax.broadcasted_iota(jnp.int32, (C, C), 0)
    c2 = lax.broadcasted_iota(jnp.int32, (C, C), 1)
    eye128 = (lax.broadcasted_iota(jnp.int32, (LANES, LANES), 0)
              == lax.broadcasted_iota(jnp.int32, (LANES, LANES), 1)).astype(BF16)
    gb = gb_ref[...]
    ncol = gb.shape[1]
    sel_row = lax.broadcasted_iota(jnp.int32, (ncol, d_all), 0)
    sel_head = lax.broadcasted_iota(jnp.int32, (ncol, d_all), 1) // LANES
    cat_row = lax.broadcasted_iota(jnp.int32, (ncol, G * C), 0)
    cat_head = lax.broadcasted_iota(jnp.int32, (ncol, G * C), 1) // C
    q, k, v = q_ref[...], k_ref[...], v_ref[...]
    chains = []
    for d in range(2):
        incl = (ri >= ci) if d == 0 else (ri <= ci)
        strict = (ri > ci) if d == 0 else (ri < ci)
        tri = ((r2 >= c2) if d == 0 else (r2 <= c2)).astype(F32)
        last = C - 1 if d == 0 else 0
        gcs = jnp.dot(tri, gb, precision=HI, preferred_element_type=F32)
        spread = lambda m, first: jnp.dot(m, (sel_row == first + sel_head).astype(F32), precision=HI,
                                          preferred_element_type=F32)
        gc = spread(gcs, d * nh)
        bet = spread(gb, 2 * nh + d * nh)
        eg = jnp.exp(gc)
        glrow = gc[last:last + 1, :]
        kbm = k * bet
        kd = k * jnp.exp(glrow - gc)
        wq_ref[d, C:2 * C, :] = (q * eg).astype(BF16)
        gl_ref[d, 0] = jnp.exp(glrow)
        vb = v * bet
        kbe = kbm * eg
        for g in range(nh // G):
            gs = slice(g * G * LANES, (g + 1) * G * LANES)
            cs = slice(g * G * C, (g + 1) * G * C)
            kq = _bdot(jnp.concatenate([kbm[:, gs], q[:, gs]], axis=0), _block_diag(k[:, gs], G), NT)
            gcol = jnp.dot(gcs, (cat_row == d * nh + g * G + cat_head).astype(F32), precision=HI,
                           preferred_element_type=F32)
            grow = jnp.dot(ones, gcol * eyecat, precision=HI, preferred_element_type=F32)
            decay = jnp.where(incl, jnp.exp(jnp.where(incl, gcol - grow, 0.0)), 0.0)
            nm = jnp.where(strict, -kq[0:C] * decay, 0.0)
            akd_ref[d, 0, 0:C, cs] = jnp.where(incl, kq[C:2 * C] * decay, 0.0).astype(BF16)
            kdstack = jnp.concatenate([kd[:, (g * G + j) * LANES:(g * G + j + 1) * LANES] for j in range(G)],
                                      axis=0)
            akd_ref[d, 0, C:C + DN_DK, cs] = _bdot(eye128, kdstack, NT).astype(BF16)
            chains.append((d, gs, nm, vb[:, gs], kbe[:, gs]))
    tinv = [eyecat + nm for (_, _, nm, _, _) in chains]
    p = [nm for (_, _, nm, _, _) in chains]
    pbd = [_block_diag(x, G) for x in p]
    for _ in range(int(math.log2(C)) - 1):
        p = [_bdot(x, y) for x, y in zip(p, pbd)]
        pbd = [_block_diag(x, G) for x in p]
        tinv = [t + _bdot(t, y) for t, y in zip(tinv, pbd)]
    for t, (d, gs, _, vbg, kbeg) in zip(tinv, chains):
        u_ref[d, :, gs] = _bdot(t, _block_diag(vbg, G))
        wq_ref[d, 0:C, gs] = _bdot(t, _block_diag(kbeg, G)).astype(BF16)


def _dn_local(q, k, v, gb):
    n, d = q.shape
    C = DN_CHUNK
    nc = n // C
    tok = pl.BlockSpec((C, d), lambda j: (j, 0))
    return pl.pallas_call(
        _dn_local_kernel, grid=(nc,),
        in_specs=[tok, tok, tok, pl.BlockSpec((C, gb.shape[1]), lambda j: (j, 0))],
        out_specs=[pl.BlockSpec((2, C, d), lambda j: (0, j, 0)),
                   pl.BlockSpec((2, 2 * C, d), lambda j: (0, j, 0)),
                   pl.BlockSpec((2, 1, C + DN_DK, DN_HEADS * C), lambda j: (0, j, 0, 0)),
                   pl.BlockSpec((2, 1, 1, d), lambda j: (0, j, 0, 0))],
        out_shape=[jax.ShapeDtypeStruct((2, n, d), F32),
                   jax.ShapeDtypeStruct((2, 2 * n, d), BF16),
                   jax.ShapeDtypeStruct((2, nc, C + DN_DK, DN_HEADS * C), BF16),
                   jax.ShapeDtypeStruct((2, nc, 1, d), F32)],
        compiler_params=_cparams(("parallel",)))(q, k, v, gb)


def _dn_scan_kernel(*refs):
    ins, (of_ref, or_ref, s_ref) = refs[:8], refs[8:]
    C = DN_CHUNK

    @pl.when(pl.program_id(1) == 0)
    def _():
        s_ref[...] = jnp.zeros_like(s_ref)

    chains = [(d, p) for d in range(2) for p in range(DN_HEADS // 2)]
    outs = (of_ref, or_ref)
    lanes = lambda p: slice(2 * p * LANES, (2 * p + 2) * LANES)
    s2 = [jnp.concatenate([s_ref[d, 2 * p], s_ref[d, 2 * p + 1]], axis=-1) for d, p in chains]
    r = [_bdot(ins[4 * d + 1][0, :, lanes(p)], _block_diag(s, 2)) for (d, p), s in zip(chains, s2)]
    vn = [ins[4 * d][0, :, lanes(p)] - x[0:C] for (d, p), x in zip(chains, r)]
    r2 = [_bdot(ins[4 * d + 2][0, 0, :, 2 * p * C:(2 * p + 2) * C], _block_diag(x, 2))
          for (d, p), x in zip(chains, vn)]
    for (d, p), s, x, y in zip(chains, s2, r, r2):
        outs[d][:, lanes(p)] = x[C:2 * C] + y[0:C]
        snew = s * ins[4 * d + 3][0, 0, :, lanes(p)] + y[C:C + DN_DK]
        s_ref[d, 2 * p] = snew[:, 0:LANES]
        s_ref[d, 2 * p + 1] = snew[:, LANES:2 * LANES]


def _dn_scan(B, T, ctx_len, u, wq, akd, gl):
    _, n, d = u.shape
    C = DN_CHUNK
    nch, ncc = T // C, ctx_len // C
    rpos = lambda c: jnp.where(c < ncc, ncc - 1 - c, nch - 1 - (c - ncc))
    pos = (lambda b, c: b * nch + c, lambda b, c: b * nch + rpos(c))
    in_specs, args = [], []
    for dd in range(2):
        p = pos[dd]
        in_specs += [pl.BlockSpec((1, C, d), lambda b, c, p=p, dd=dd: (dd, p(b, c), 0)),
                     pl.BlockSpec((1, 2 * C, d), lambda b, c, p=p, dd=dd: (dd, p(b, c), 0)),
                     pl.BlockSpec((1, 1, C + DN_DK, DN_HEADS * C), lambda b, c, p=p, dd=dd: (dd, p(b, c), 0, 0)),
                     pl.BlockSpec((1, 1, 1, d), lambda b, c, p=p, dd=dd: (dd, p(b, c), 0, 0))]
        args += [u, wq, akd, gl]
    return pl.pallas_call(
        _dn_scan_kernel, grid=(B, nch), in_specs=in_specs,
        out_specs=[pl.BlockSpec((C, d), lambda b, c: (pos[0](b, c), 0)),
                   pl.BlockSpec((C, d), lambda b, c: (pos[1](b, c), 0))],
        out_shape=[jax.ShapeDtypeStruct((n, d), F32)] * 2,
        scratch_shapes=[pltpu.VMEM((2, DN_HEADS, DN_DK, LANES), F32)],
        compiler_params=_cparams(("arbitrary", "arbitrary")))(*args)


def _final_kernel(x_ref, y_ref, mod_ref, o_ref):
    o_ref[0] = x_ref[...] + mod_ref[0, 5:6, :] * y_ref[...]


def _final(B, T, ctx_len, x, y, mod):
    n, d = x.shape
    tm = TOKEN_BLOCK
    nblk, ncb = T // tm, ctx_len // tm
    tok = pl.BlockSpec((tm, d), lambda b, i: (b * nblk + ncb + i, 0))
    return pl.pallas_call(
        _final_kernel, grid=(B, nblk - ncb),
        in_specs=[tok, tok, pl.BlockSpec((1, 6, d), lambda b, i: (b, 0, 0))],
        out_specs=pl.BlockSpec((1, tm, d), lambda b, i: (b, i, 0)),
        out_shape=jax.ShapeDtypeStruct((B, T - ctx_len, d), F32),
        compiler_params=_cparams(("parallel", "parallel")))(x, y, mod)


def _rope_tables(seq, ctx_len):
    rows = seq // GRID_W
    r = jnp.broadcast_to(jnp.arange(rows, dtype=F32)[:, None], (rows, GRID_W)).reshape(-1)
    cl = jnp.broadcast_to(jnp.arange(GRID_W, dtype=F32)[None, :], (rows, GRID_W)).reshape(-1)

    def angles(rot_dim):
        nf = rot_dim // 4
        inv = ROPE_BASE ** (-jnp.arange(nf, dtype=F32) / nf)
        ang = jnp.concatenate([r[:, None] * inv, cl[:, None] * inv], axis=-1)
        ang = jnp.concatenate([jnp.zeros((ctx_len, rot_dim // 2), F32), ang], axis=0)
        return jnp.cos(ang), jnp.sin(ang)

    T = seq + ctx_len
    ca, sa = angles(DIFF_DIM)
    z = jnp.zeros_like(sa)
    ta = jnp.stack([jnp.concatenate([ca, ca] * 2, -1), jnp.concatenate([-sa, z] * 2, -1),
                    jnp.concatenate([z, sa] * 2, -1)])
    cb, sb = angles(MLA_ROPE)
    one, zn, zt = jnp.ones((T, MLA_NOPE), F32), jnp.zeros((T, MLA_NOPE), F32), jnp.zeros((T, LANES - MLA_QK), F32)
    zb = jnp.zeros_like(sb)
    tb = jnp.stack([jnp.concatenate([one, cb, cb, 1.0 + zt], -1), jnp.concatenate([zn, -sb, zb, zt], -1),
                    jnp.concatenate([zn, zb, sb, zt], -1)])
    return ta, tb


def _pad_heads(w, heads, width):
    lead = w.shape[:-1]
    w = w.reshape(lead + (heads, width))
    return jnp.pad(w, [(0, 0)] * len(lead) + [(0, 0), (0, LANES - width)]).reshape(lead + (heads * LANES,))


def _block_diag_ones(n, group):
    idx = np.arange(n) // group
    return jnp.asarray(idx[:, None] == idx[None, :], dtype=BF16)


def kernel(x, c, ctx, c_ctx, ada_w, ada_b, ev_w_in, ev_w_out, diff_q_norm, diff_k_norm, diff_lam_q1, diff_lam_k1, diff_lam_q2, diff_lam_k2, diff_subln, mla_cq_norm, mla_ckv_norm, mla_w_uq, mla_w_ukv, mla_q_norm, mla_k_norm, dn_w_in, dn_conv, dn_a_log, dn_dt_bias, dn_o_norm, dn_w_out, peer_wq, peer_k1, peer_k2, peer_u, peer_v):
    B, S, D = x.shape
    CTX = ctx.shape[1]
    T = CTX + S
    depth = ada_w.shape[0]
    tm = TOKEN_BLOCK
    assert CTX % tm == 0 and S % tm == 0 and (B * T) % PEER_TOKEN_BLOCK == 0 and B + 1 <= 8
    nblk, ncb = T // tm, CTX // tm
    geo = (B, nblk, tm, lambda b, i: jnp.where(i < ncb, B, b))

    cc = jnp.zeros((8, D), F32).at[:B].set(c).at[B].set(c_ctx)
    mods = _adaln(cc, ada_w, ada_b).reshape(depth, 8, 6, D)
    xs = jnp.concatenate([ctx, x], axis=1).reshape(B * T, D)
    ropea, ropeb = _rope_tables(S, CTX)
    bd64, bd128 = _block_diag_ones(HEAD_W, DIFF_DIM), _block_diag_ones(HEAD_W, LANES)

    y = None
    for l in range(depth):
        i = l // 2
        mod = mods[l]
        gmod = mods[l - 1] if l else None
        if l % 2 == 0:
            lam_init = 0.8 - 0.6 * math.exp(-0.3 * l)
            w_in = ev_w_in[i]
            o = 3 * HEAD_W + MLA_Q_RANK + MLA_KV_RANK
            kr_rep = jnp.pad(jnp.broadcast_to(w_in[:, None, o:o + MLA_ROPE], (D, MLA_HEADS, MLA_ROPE)),
                             ((0, 0), (0, 0), (MLA_NOPE, LANES - MLA_QK))).reshape(D, HEAD_W)
            w_cat = jnp.concatenate([w_in[:, :o], kr_rep], axis=1).astype(BF16)
            res = _modmm(geo, xs, mod, w_cat, y, gmod)
            (xs, z) = res if y is not None else (xs, res)
            ukv = mla_w_ukv[i].reshape(MLA_KV_RANK, MLA_HEADS, MLA_NOPE + MLA_VDIM)
            gains = jnp.stack([jnp.tile(diff_q_norm[i], HEAD_W // DIFF_DIM), jnp.tile(diff_k_norm[i], HEAD_W // DIFF_DIM),
                               jnp.tile(jnp.pad(mla_q_norm[i], (0, LANES - MLA_QK)), MLA_HEADS),
                               jnp.tile(jnp.pad(mla_k_norm[i], (0, LANES - MLA_QK)), MLA_HEADS)])
            qa, ka, va, qb, kb, vb = _even_prep(
                geo, z, ropea, ropeb, gains, mla_cq_norm[i][None, :], mla_ckv_norm[i][None, :],
                _pad_heads(mla_w_uq[i], MLA_HEADS, MLA_QK).astype(BF16),
                _pad_heads(ukv[:, :, :MLA_NOPE].reshape(MLA_KV_RANK, -1), MLA_HEADS, MLA_NOPE).astype(BF16),
                ukv[:, :, MLA_NOPE:].reshape(MLA_KV_RANK, -1).astype(BF16), bd64, bd128)
            lam_vecs = jnp.stack([diff_lam_q1[i], diff_lam_k1[i], diff_lam_q2[i], diff_lam_k2[i]])
            oa = _attention(B, T, CTX, qa, ka, va, lam_vecs, diff_subln[i][None, :], lam_init)
            ob = _attention(B, T, CTX, qb, kb, vb)
            xs, hqt = _mixer_out(geo, _even_out_kernel, [(oa, HEAD_W, 0), (ob, HEAD_W, 0)], [], xs, mod,
                                 ev_w_out[i].astype(BF16))
        else:
            nin = dn_w_in.shape[2]
            w_in = jnp.pad(dn_w_in[i], ((0, 0), (0, -nin % LANES))).astype(BF16)
            res = _modmm(geo, xs, mod, w_in, y, gmod)
            (xs, z) = res if y is not None else (xs, res)
            q, k, v, gb = _dn_prep(geo, ncb, z, dn_conv[i], dn_a_log[i].reshape(1, -1),
                                   dn_dt_bias[i].reshape(1, -1))
            of, orv = _dn_scan(B, T, CTX, *_dn_local(q, k, v, gb))
            xs, hqt = _mixer_out(geo, _dn_out_kernel,
                                 [(of, D, 0), (orv, D, 0), (z, D, DN_QKV // D)], [dn_o_norm[i][None, :]], xs, mod,
                                 dn_w_out[i].astype(BF16))
        y = _peer(hqt, peer_wq[l], peer_k1[l], peer_k2[l], peer_u[l], peer_v[l])
    return _final(B, T, CTX, xs, y, mods[depth - 1])
```

```python
import functools
import math

import jax
import jax.numpy as jnp
import numpy as np
from jax import lax
from jax.experimental import pallas as pl
from jax.experimental.pallas import tpu as pltpu

F32 = jnp.float32
BF16 = jnp.bfloat16
HI = lax.Precision.HIGHEST
NT = (((1,), (1,)), ((), ()))

EPS = 1e-6
LOG2E = math.log2(math.e)
ROPE_BASE = 10000.0
GRID_W = 64
LANES = 128
TOKEN_BLOCK = 256
VMEM_LIMIT = 56 * 1024 * 1024

DIFF_HEADS, DIFF_DIM = 4, 64
MLA_HEADS, MLA_Q_RANK, MLA_KV_RANK, MLA_NOPE, MLA_ROPE, MLA_VDIM = 4, 256, 128, 64, 32, 128
MLA_QK = MLA_NOPE + MLA_ROPE
HEAD_W = DIFF_HEADS * LANES
DN_HEADS, DN_DK, DN_CONV, DN_CHUNK = 8, 128, 5, 64
DN_QKV = 3 * DN_HEADS * DN_DK
PEER_HEADS, PEER_NKEYS, PEER_TOPK = 8, 128, 16
PEER_CAND_ROWS = 80
PEER_TOKEN_BLOCK = 512
PEER_EXPERT_BLOCK = 1024


def _cparams(sem):
    return pltpu.CompilerParams(dimension_semantics=sem, vmem_limit_bytes=VMEM_LIMIT)


def _rms_rows(x):
    return x * lax.rsqrt(jnp.mean(x * x, axis=-1, keepdims=True) + EPS)


def _silu(x):
    return x * jax.nn.sigmoid(x)


def _group_sum(sq, bd):
    hi = sq.astype(BF16)
    lo = (sq - hi.astype(F32)).astype(BF16)
    return jnp.dot(hi, bd, preferred_element_type=F32) + jnp.dot(lo, bd, preferred_element_type=F32)


def _adaln_kernel(c_ref, w_ref, b_ref, o_ref):
    s = _silu(c_ref[...])
    o_ref[0] = jnp.dot(s, w_ref[0], precision=HI, preferred_element_type=F32) + b_ref[0]


def _adaln(cc, ada_w, ada_b):
    depth, d, n = ada_w.shape
    tn = 1536
    return pl.pallas_call(
        _adaln_kernel,
        grid=(depth, n // tn),
        in_specs=[pl.BlockSpec((8, d), lambda l, j: (0, 0)),
                  pl.BlockSpec((1, d, tn), lambda l, j: (l, 0, j)),
                  pl.BlockSpec((1, 1, tn), lambda l, j: (l, 0, j))],
        out_specs=pl.BlockSpec((1, 8, tn), lambda l, j: (l, 0, j)),
        out_shape=jax.ShapeDtypeStruct((depth, 8, n), F32),
        compiler_params=_cparams(("arbitrary", "arbitrary")),
    )(cc, ada_w, ada_b.reshape(depth, 1, n))


def _modmm_kernel(*refs, has_y):
    if has_y:
        x_ref, y_ref, gmod_ref, mod_ref, w_ref, xo_ref, z_ref = refs
        x = x_ref[...] + gmod_ref[0, 5:6, :] * y_ref[...]
        xo_ref[...] = x
    else:
        x_ref, mod_ref, w_ref, z_ref = refs
        x = x_ref[...]
    h = _rms_rows(x) * (1.0 + mod_ref[0, 1:2, :]) + mod_ref[0, 0:1, :]
    z_ref[...] = jnp.dot(h.astype(BF16), w_ref[...], preferred_element_type=F32)


def _modmm(geo, x, mod, w, y=None, gmod=None):
    B, nblk, tm, mrow = geo
    n, d = x.shape
    nout = w.shape[1]
    tok = pl.BlockSpec((tm, d), lambda b, i: (b * nblk + i, 0))
    modspec = pl.BlockSpec((1, 6, d), lambda b, i: (mrow(b, i), 0, 0))
    wspec = pl.BlockSpec((d, nout), lambda b, i: (0, 0))
    zspec = pl.BlockSpec((tm, nout), lambda b, i: (b * nblk + i, 0))
    zshape = jax.ShapeDtypeStruct((n, nout), F32)
    if y is None:
        return pl.pallas_call(
            functools.partial(_modmm_kernel, has_y=False), grid=(B, nblk),
            in_specs=[tok, modspec, wspec], out_specs=zspec, out_shape=zshape,
            compiler_params=_cparams(("parallel", "parallel")))(x, mod, w)
    return pl.pallas_call(
        functools.partial(_modmm_kernel, has_y=True), grid=(B, nblk),
        in_specs=[tok, tok, modspec, modspec, wspec], out_specs=[tok, zspec],
        out_shape=[jax.ShapeDtypeStruct((n, d), F32), zshape],
        compiler_params=_cparams(("parallel", "parallel")))(x, y, gmod, mod, w)


def _rope_lanes(v, tab_ref, half):
    rep = v.shape[-1] // LANES
    c = jnp.concatenate([tab_ref[0]] * rep, axis=-1)
    sm = jnp.concatenate([tab_ref[1]] * rep, axis=-1)
    sp = jnp.concatenate([tab_ref[2]] * rep, axis=-1)
    n = v.shape[-1]
    return v * c + pltpu.roll(v, n - half, 1) * sm + pltpu.roll(v, half, 1) * sp


def _even_prep_kernel(z_ref, ropea_ref, ropeb_ref, gains_ref, cqn_ref, ckvn_ref, wuq_ref, wuk_ref, wuv_ref,
                      bd64_ref, bd128_ref, qa_ref, ka_ref, va_ref, qb_ref, kb_ref, vb_ref):
    W = HEAD_W
    bd64 = bd64_ref[...]
    bd128 = bd128_ref[...]

    def norm_groups(v, bd, width, gain):
        ms = _group_sum(v * v, bd) * (1.0 / width)
        return v * lax.rsqrt(ms + EPS) * gain

    qa = norm_groups(z_ref[:, 0:W], bd64, DIFF_DIM, gains_ref[0:1, :])
    ka = norm_groups(z_ref[:, W:2 * W], bd64, DIFF_DIM, gains_ref[1:2, :])
    qa_ref[...] = (_rope_lanes(qa, ropea_ref, DIFF_DIM // 2) * (DIFF_DIM ** -0.5 * LOG2E)).astype(BF16)
    ka_ref[...] = _rope_lanes(ka, ropea_ref, DIFF_DIM // 2).astype(BF16)
    va_ref[...] = z_ref[:, 2 * W:3 * W].astype(BF16)

    o = 3 * W
    cq = (_rms_rows(z_ref[:, o:o + MLA_Q_RANK]) * cqn_ref[...]).astype(BF16)
    o += MLA_Q_RANK
    ckv = (_rms_rows(z_ref[:, o:o + MLA_KV_RANK]) * ckvn_ref[...]).astype(BF16)
    o += MLA_KV_RANK
    kr = z_ref[:, o:o + W]
    qb = jnp.dot(cq, wuq_ref[...], preferred_element_type=F32)
    kb = jnp.dot(ckv, wuk_ref[...], preferred_element_type=F32) + kr
    qb = norm_groups(qb, bd128, MLA_QK, gains_ref[2:3, :])
    kb = norm_groups(kb, bd128, MLA_QK, gains_ref[3:4, :])
    qb_ref[...] = (_rope_lanes(qb, ropeb_ref, MLA_ROPE // 2) * (MLA_QK ** -0.5 * LOG2E)).astype(BF16)
    kb_ref[...] = _rope_lanes(kb, ropeb_ref, MLA_ROPE // 2).astype(BF16)
    vb_ref[...] = jnp.dot(ckv, wuv_ref[...], preferred_element_type=F32).astype(BF16)


def _even_prep(geo, z, ropea, ropeb, gains, cqn, ckvn, wuq, wuk, wuv, bd64, bd128):
    B, nblk, tm, _ = geo
    n = z.shape[0]
    W = HEAD_W
    full = lambda a: pl.BlockSpec(a.shape, lambda b, i: (0,) * a.ndim)
    rope = pl.BlockSpec((3, tm, LANES), lambda b, i: (0, i, 0))
    out = pl.BlockSpec((tm, W), lambda b, i: (b * nblk + i, 0))
    return pl.pallas_call(
        _even_prep_kernel, grid=(B, nblk),
        in_specs=[pl.BlockSpec((tm, z.shape[1]), lambda b, i: (b * nblk + i, 0)), rope, rope, full(gains),
                  full(cqn), full(ckvn), full(wuq), full(wuk), full(wuv), full(bd64), full(bd128)],
        out_specs=[out] * 6, out_shape=[jax.ShapeDtypeStruct((n, W), BF16)] * 6,
        compiler_params=_cparams(("parallel", "parallel")))(z, ropea, ropeb, gains, cqn, ckvn, wuq, wuk, wuv,
                                                            bd64, bd128)


def _attn_kernel(*refs, diff, tk, row_split, n_ctx_q, ctx_len, n_keys, lam_init):
    if diff:
        q_ref, k_ref, v_ref, lam_ref, subln_ref, o_ref, s_ref = refs
    else:
        q_ref, k_ref, v_ref, o_ref, s_ref = refs
    qi = pl.program_id(2)
    tr = q_ref.shape[0] // row_split
    qs = []
    for r in range(row_split):
        q = q_ref[r * tr:(r + 1) * tr, :]
        if diff:
            lane = lax.broadcasted_iota(jnp.int32, q.shape, 1)
            zero = jnp.zeros_like(q)
            qs += [jnp.where(lane < DIFF_DIM, q, zero), jnp.where(lane >= DIFF_DIM, q, zero)]
        else:
            qs.append(q)
    nsub = len(qs) // row_split

    def scores(slot, start, size):
        k = k_ref[pl.ds(start, size), :]
        s = [lax.dot_general(qq, k, NT, preferred_element_type=F32) for qq in qs]
        for j, x in enumerate(s):
            s_ref[slot, j, :, 0:size] = x
        return tuple(jnp.max(x, axis=-1, keepdims=True) for x in s)

    def update(slot, start, size, mx, carry):
        v = v_ref[pl.ds(start, size), :]
        mn = [jnp.maximum(m, x) for (m, _, _), x in zip(carry, mx)]
        p = [jnp.exp2(s_ref[slot, j, :, 0:size] - y) for j, y in enumerate(mn)]
        alpha = [jnp.exp2(m - y) for (m, _, _), y in zip(carry, mn)]
        pv = [jnp.dot(x.astype(BF16), v, preferred_element_type=F32) for x in p]
        return tuple((y, a * l + jnp.sum(x, axis=-1, keepdims=True), a * acc + z)
                     for (_, l, acc), y, a, x, z in zip(carry, mn, alpha, p, pv))

    init = tuple((jnp.full((tr, 1), -1e30, F32), jnp.zeros((tr, 1), F32), jnp.zeros((tr, LANES), F32))
                 for _ in qs)

    def finish(carry):
        outs = [acc / l for (_, l, acc) in carry]
        for r in range(row_split):
            if diff:
                lam = (jnp.exp(jnp.sum(lam_ref[0:1, :] * lam_ref[1:2, :], axis=-1, keepdims=True))
                       - jnp.exp(jnp.sum(lam_ref[2:3, :] * lam_ref[3:4, :], axis=-1, keepdims=True)) + lam_init)
                o = _rms_rows(outs[r * nsub] - lam * outs[r * nsub + 1]) * subln_ref[...] * (1.0 - lam_init)
            else:
                o = outs[r]
            o_ref[r * tr:(r + 1) * tr, :] = o.astype(o_ref.dtype)

    @pl.when(qi < n_ctx_q)
    def _():
        finish(update(0, 0, ctx_len, scores(0, 0, ctx_len), init))

    @pl.when(qi >= n_ctx_q)
    def _():
        n = n_keys // tk
        at = lambda c: pl.multiple_of(c * tk, tk)

        def pair(j, state):
            carry, mx0 = state
            mx1 = scores(1, at(2 * j + 1), tk)
            carry = update(0, at(2 * j), tk, mx0, carry)
            mx0 = scores(0, at(2 * j + 2), tk)
            carry = update(1, at(2 * j + 1), tk, mx1, carry)
            return carry, mx0

        carry, mx0 = lax.fori_loop(0, (n - 1) // 2, pair, (init, scores(0, 0, tk)))
        if n % 2 == 0:
            mx1 = scores(1, (n - 1) * tk, tk)
            carry = update(0, (n - 2) * tk, tk, mx0, carry)
            carry = update(1, (n - 1) * tk, tk, mx1, carry)
        else:
            carry = update(0, (n - 1) * tk, tk, mx0, carry)
        finish(carry)


def _attention(B, T, ctx_len, q, k, v, lam_vecs=None, subln=None, lam_init=0.0):
    diff = lam_vecs is not None
    tq = TOKEN_BLOCK
    tk = next(c for c in (1408, 768, TOKEN_BLOCK) if T % c == 0)
    nq = T // tq
    heads = q.shape[1] // LANES
    qspec = pl.BlockSpec((tq, LANES), lambda b, h, i: (b * nq + i, h))
    kvspec = pl.BlockSpec((T, LANES), lambda b, h, i: (b, h))
    in_specs = [qspec, kvspec, kvspec]
    args = [q, k, v]
    if diff:
        in_specs += [pl.BlockSpec(lam_vecs.shape, lambda b, h, i: (0, 0)),
                     pl.BlockSpec(subln.shape, lambda b, h, i: (0, 0))]
        args += [lam_vecs, subln]
    row_split = 1 if diff else 2
    return pl.pallas_call(
        functools.partial(_attn_kernel, diff=diff, tk=tk, row_split=row_split, n_ctx_q=ctx_len // tq,
                          ctx_len=ctx_len, n_keys=T, lam_init=lam_init),
        grid=(B, heads, nq), in_specs=in_specs, out_specs=qspec,
        out_shape=jax.ShapeDtypeStruct(q.shape, BF16),
        scratch_shapes=[pltpu.VMEM((2, 2, tq // row_split, max(tk, ctx_len)), F32)],
        compiler_params=_cparams(("parallel", "parallel", "arbitrary")))(*args)


def _residual_tail(x, y, mod_ref, xo_ref, hqt_ref):
    xn = x + mod_ref[0, 2:3, :] * y
    xo_ref[...] = xn
    hq = _rms_rows(xn) * (1.0 + mod_ref[0, 4:5, :]) + mod_ref[0, 3:4, :]
    hqt_ref[...] = hq.T.astype(BF16)


def _even_out_kernel(oa_ref, ob_ref, x_ref, mod_ref, wo_ref, xo_ref, hqt_ref):
    W = HEAD_W
    y = (jnp.dot(oa_ref[...], wo_ref[0:W, :], preferred_element_type=F32)
         + jnp.dot(ob_ref[...], wo_ref[W:2 * W, :], preferred_element_type=F32))
    _residual_tail(x_ref[...], y, mod_ref, xo_ref, hqt_ref)


def _dn_out_kernel(of_ref, or_ref, zg_ref, onorm_ref, x_ref, mod_ref, wo_ref, xo_ref, hqt_ref):
    parts = []
    for h in range(DN_HEADS):
        hs = slice(h * LANES, (h + 1) * LANES)
        o = of_ref[:, hs] + or_ref[:, hs]
        parts.append((_rms_rows(o) * onorm_ref[...] * _silu(zg_ref[:, hs])).astype(BF16))
    y = jnp.dot(jnp.concatenate(parts, axis=-1), wo_ref[...], preferred_element_type=F32)
    _residual_tail(x_ref[...], y, mod_ref, xo_ref, hqt_ref)


def _mixer_out(geo, kernel_fn, token_args, small_args, x, mod, wo):
    B, nblk, tm, mrow = geo
    n, d = x.shape
    tokspec = lambda a, col: pl.BlockSpec((tm, a[1]), lambda b, i: (b * nblk + i, col))
    in_specs = [tokspec((a, w), col) for (a, w, col) in token_args]
    in_specs += [pl.BlockSpec(a.shape, lambda b, i: (0,) * a.ndim) for a in small_args]
    in_specs += [pl.BlockSpec((tm, d), lambda b, i: (b * nblk + i, 0)),
                 pl.BlockSpec((1, 6, d), lambda b, i: (mrow(b, i), 0, 0)),
                 pl.BlockSpec(wo.shape, lambda b, i: (0, 0))]
    return pl.pallas_call(
        kernel_fn, grid=(B, nblk), in_specs=in_specs,
        out_specs=[pl.BlockSpec((tm, d), lambda b, i: (b * nblk + i, 0)),
                   pl.BlockSpec((d, tm), lambda b, i: (0, b * nblk + i))],
        out_shape=[jax.ShapeDtypeStruct((n, d), F32), jax.ShapeDtypeStruct((d, n), BF16)],
        compiler_params=_cparams(("parallel", "parallel")))(
            *[a for (a, _, _) in token_args], *small_args, x, mod, wo)


def _peer_select_kernel(hqt_ref, wqt_ref, k1_ref, k2_ref, a1_ref, e1_ref, s2_ref, e2_ref, tau_ref, work_ref, top_ref,
                        cand_ref):
    K, H = PEER_TOPK, PEER_HEADS
    qt = jnp.dot(wqt_ref[...], hqt_ref[...], preferred_element_type=F32)
    dk = k1_ref.shape[1]
    for h in range(H):
        q1 = qt[(2 * h) * dk:(2 * h + 1) * dk, :].astype(BF16)
        q2 = qt[(2 * h + 1) * dk:(2 * h + 2) * dk, :].astype(BF16)
        s1 = jnp.dot(k1_ref[...], q1, preferred_element_type=F32)
        s2 = jnp.dot(k2_ref[...], q2, preferred_element_type=F32)
        a1_ref[h] = s1
        s2_ref[h] = s2
        work_ref[2 * h] = s1
        work_ref[2 * h + 1] = s2

    def extract(ref, count):
        s = [ref[a] for a in range(count)]
        m = [jnp.max(x, axis=0, keepdims=True) for x in s]
        for a in range(count):
            ref[a] = jnp.where(s[a] == m[a], -jnp.inf, s[a])
        return m

    def top_round(r, carry):
        for a, m in enumerate(extract(work_ref, 2 * H)):
            top_ref[a, pl.ds(r, 1), :] = m
        return carry

    lax.fori_loop(0, K, top_round, 0)

    def candidates(h):
        v1, v2 = top_ref[2 * h], top_ref[2 * h + 1]
        row8 = lax.broadcasted_iota(jnp.int32, (8, v1.shape[1]), 0)
        pieces = [v1[0:1, :] + v2, v1[1:2, :] + v2[0:8, :]]
        pieces += [jnp.where(row8 < K // (r1 + 1), v1[r1:r1 + 1, :] + v2[0:8, :], -jnp.inf) for r1 in range(2, 8)]
        pieces.append(v1[8:16, :] + v2[0:1, :])
        return jnp.concatenate(pieces, axis=0)

    for h in range(H):
        cand_ref[h] = candidates(h)
    tau = lax.fori_loop(0, K, lambda r, carry: tuple(extract(cand_ref, H)),
                        tuple(jnp.zeros((1, tau_ref.shape[1]), F32) for _ in range(H)))
    for h in range(H):
        cand = candidates(h)
        m1, m2 = top_ref[2 * h, 0:1, :], top_ref[2 * h + 1, 0:1, :]
        zsum = jnp.sum(jnp.where(cand >= tau[h], jnp.exp(cand - (m1 + m2)), 0.0), axis=0, keepdims=True)
        e1_ref[h] = jnp.exp(a1_ref[h] - m1) / zsum
        e2_ref[h] = jnp.exp(s2_ref[h] - m2)
        tau_ref[h:h + 1, :] = tau[h]


def _peer_select(hqt, wqt, k1, k2):
    d, n = hqt.shape
    tm = TOKEN_BLOCK
    H, NK = PEER_HEADS, PEER_NKEYS
    big = pl.BlockSpec((H, NK, tm), lambda j: (0, 0, j))
    bigshape = jax.ShapeDtypeStruct((H, NK, n), F32)
    full = lambda a: pl.BlockSpec(a.shape, lambda j: (0,) * a.ndim)
    return pl.pallas_call(
        _peer_select_kernel, grid=(n // tm,),
        in_specs=[pl.BlockSpec((d, tm), lambda j: (0, j)), full(wqt), full(k1), full(k2)],
        out_specs=[big, big, big, big, pl.BlockSpec((H, tm), lambda j: (0, j))],
        out_shape=[bigshape] * 4 + [jax.ShapeDtypeStruct((H, n), F32)],
        scratch_shapes=[pltpu.VMEM((2 * H, NK, tm), F32), pltpu.VMEM((2 * H, PEER_TOPK, tm), F32),
                        pltpu.VMEM((H, PEER_CAND_ROWS, tm), F32)],
        compiler_params=_cparams(("parallel",)))(hqt, wqt, k1, k2)


def _gelu(a):
    return 0.5 * a * (1.0 + lax.erf(a * np.float32(math.sqrt(0.5))))


def _peer_dense_kernel(hqt_ref, u_ref, vt_ref, a1_ref, e1_ref, s2_ref, e2_ref, tau_ref, zero_ref, y_ref, acc_ref,
                       act_ref, wt_ref, row_ref):
    c = pl.program_id(1)
    NK = PEER_NKEYS
    n_i = PEER_EXPERT_BLOCK // NK

    @pl.when(c == 0)
    def _():
        acc_ref[...] = jnp.zeros_like(acc_ref)

    H = PEER_HEADS
    strips = [slice(ts * LANES, (ts + 1) * LANES) for ts in range(hqt_ref.shape[1] // LANES)]
    for ii in range(n_i):
        for h in range(H):
            i = c * n_i + ii
            row_ref[0, ii, h:h + 1, :] = a1_ref[h, pl.ds(i, 1), :]
            row_ref[1, ii, h:h + 1, :] = e1_ref[h, pl.ds(i, 1), :]

    act_ref[...] = _gelu(jnp.dot(u_ref[...], hqt_ref[...], preferred_element_type=F32))
    JB, RUNS = 16, 8
    zero = zero_ref[0:JB, :]
    for tl in strips:
        for i0 in range(0, n_i, RUNS):
            for jb in range(NK // JB):
                js = slice(jb * JB, (jb + 1) * JB)
                g = [jnp.zeros((JB, LANES), F32) for _ in range(RUNS)]
                for h in range(H):
                    s2 = s2_ref[h, js, tl] + zero
                    e2 = e2_ref[h, js, tl] + zero
                    tau = tau_ref[h:h + 1, tl]
                    for r in range(RUNS):
                        sel = (row_ref[0, i0 + r, h:h + 1, tl] + s2) >= tau
                        g[r] = g[r] + jnp.where(sel, row_ref[1, i0 + r, h:h + 1, tl] * e2, 0.0)
                for r in range(RUNS):
                    rows = slice((i0 + r) * NK + jb * JB, (i0 + r) * NK + (jb + 1) * JB)
                    wt_ref[rows, tl] = (g[r] * act_ref[rows, tl]).astype(BF16)
    acc_ref[...] += jnp.dot(vt_ref[...], wt_ref[...], preferred_element_type=F32)

    @pl.when(c == pl.num_programs(1) - 1)
    def _():
        y_ref[...] = acc_ref[...].T


def _peer_dense(hqt, u, vt, a1, e1, s2, e2, tau):
    d, n = hqt.shape
    tm, te = PEER_TOKEN_BLOCK, PEER_EXPERT_BLOCK
    H, NK = PEER_HEADS, PEER_NKEYS
    big = pl.BlockSpec((H, NK, tm), lambda j, c: (0, 0, j))
    return pl.pallas_call(
        _peer_dense_kernel, grid=(n // tm, u.shape[0] // te),
        in_specs=[pl.BlockSpec((d, tm), lambda j, c: (0, j)),
                  pl.BlockSpec((te, d), lambda j, c: (c, 0)),
                  pl.BlockSpec((d, te), lambda j, c: (0, c)),
                  big, big, big, big, pl.BlockSpec((H, tm), lambda j, c: (0, j)),
                  pl.BlockSpec((NK, LANES), lambda j, c: (0, 0))],
        out_specs=pl.BlockSpec((tm, d), lambda j, c: (j, 0)),
        out_shape=jax.ShapeDtypeStruct((n, d), F32),
        scratch_shapes=[pltpu.VMEM((d, tm), F32), pltpu.VMEM((te, tm), F32), pltpu.VMEM((te, tm), BF16),
                        pltpu.VMEM((2, te // NK, H, tm), F32)],
        compiler_params=_cparams(("parallel", "arbitrary")))(hqt, u, vt, a1, e1, s2, e2, tau,
                                                             jnp.zeros((NK, LANES), F32))


def _peer(hqt, wq, k1, k2, u_tab, v_tab):
    sel = _peer_select(hqt, wq.T.astype(BF16), k1.astype(BF16), k2.astype(BF16))
    return _peer_dense(hqt, u_tab.astype(BF16), v_tab.T.astype(BF16), *sel)


def _dn_prep_kernel(z_ref, prev_ref, next_ref, conv_ref, alog_ref, dtb_ref, ab_ref, q_ref, k_ref, v_ref, gb_ref,
                    ext_ref, *, n_ctx_blk, nblk):
    i = pl.program_id(1)
    tm = z_ref.shape[0]
    halo = prev_ref.shape[0]
    pad = DN_CONV // 2
    has_prev = jnp.logical_and(i != 0, i != n_ctx_blk)
    has_next = jnp.logical_and(i != n_ctx_blk - 1, i != nblk - 1)
    ext_ref[0:halo, :] = jnp.where(has_prev, prev_ref[...], 0.0)
    ext_ref[halo:halo + tm, :] = z_ref[...]
    ext_ref[halo + tm:, :] = jnp.where(has_next, next_ref[...], 0.0)
    nq = DN_HEADS * DN_DK
    for j in range(DN_QKV // LANES):
        cs = slice(j * LANES, (j + 1) * LANES)
        acc = conv_ref[0:1, cs] * ext_ref[halo - pad:halo - pad + tm, cs]
        for t in range(1, DN_CONV):
            acc = acc + conv_ref[t:t + 1, cs] * ext_ref[halo - pad + t:halo - pad + t + tm, cs]
        y = _silu(acc)
        if j * LANES < 2 * nq:
            y = y * lax.rsqrt(jnp.sum(y * y, axis=-1, keepdims=True) + EPS)
        if j * LANES < nq:
            q_ref[:, cs] = y * DN_DK ** -0.5
        elif j * LANES < 2 * nq:
            k_ref[:, slice(j * LANES - nq, (j + 1) * LANES - nq)] = y
        else:
            v_ref[:, slice(j * LANES - 2 * nq, (j + 1) * LANES - 2 * nq)] = y
    ab = ab_ref[...]
    nh = 2 * DN_HEADS
    xa = ab[:, 0:nh] + dtb_ref[...]
    softplus = jnp.maximum(xa, 0.0) + jnp.log(1.0 + jnp.exp(-jnp.abs(xa)))
    gb_ref[:, 0:nh] = -jnp.exp(alog_ref[...]) * softplus
    gb_ref[:, nh:2 * nh] = jax.nn.sigmoid(ab[:, nh:2 * nh])


def _dn_prep(geo, n_ctx_blk, z, conv_w, alog, dtb):
    B, nblk, tm, _ = geo
    n = z.shape[0]
    halo = 8
    r = tm // halo
    nh8 = n // halo
    d = DN_HEADS * DN_DK
    prev = pl.BlockSpec((halo, DN_QKV), lambda b, i: (jnp.maximum((b * nblk + i) * r - 1, 0), 0))
    nxt = pl.BlockSpec((halo, DN_QKV), lambda b, i: (jnp.minimum((b * nblk + i + 1) * r, nh8 - 1), 0))
    ab = z[:, DN_QKV + d:DN_QKV + d + 4 * DN_HEADS]
    tok = pl.BlockSpec((tm, d), lambda b, i: (b * nblk + i, 0))
    gbspec = pl.BlockSpec((tm, 4 * DN_HEADS), lambda b, i: (b * nblk + i, 0))
    full = lambda a: pl.BlockSpec(a.shape, lambda b, i: (0,) * a.ndim)
    return pl.pallas_call(
        functools.partial(_dn_prep_kernel, n_ctx_blk=n_ctx_blk, nblk=nblk), grid=(B, nblk),
        in_specs=[pl.BlockSpec((tm, DN_QKV), lambda b, i: (b * nblk + i, 0)), prev, nxt, full(conv_w),
                  full(alog), full(dtb), gbspec],
        out_specs=[tok, tok, tok, gbspec],
        out_shape=[jax.ShapeDtypeStruct((n, d), F32)] * 3 + [jax.ShapeDtypeStruct((n, 4 * DN_HEADS), F32)],
        scratch_shapes=[pltpu.VMEM((tm + 2 * halo, DN_QKV), F32)],
        compiler_params=_cparams(("parallel", "parallel")))(z, z, z, conv_w, alog, dtb, ab)


DN_GROUP = 4


def _bdot(a, b, dims=None):
    a, b = a.astype(BF16), b.astype(BF16)
    if dims is None:
        return jnp.dot(a, b, preferred_element_type=F32)
    return lax.dot_general(a, b, dims, preferred_element_type=F32)


def _block_diag(x, nblk):
    r, n = x.shape
    w = n // nblk
    tall = jnp.concatenate([x] * nblk, axis=0)
    rb = lax.broadcasted_iota(jnp.int32, tall.shape, 0) // r
    lb = lax.broadcasted_iota(jnp.int32, tall.shape, 1) // w
    return jnp.where(rb == lb, tall, jnp.zeros_like(tall))


def _dn_local_kernel(q_ref, k_ref, v_ref, gb_ref, u_ref, wq_ref, akd_ref, gl_ref):
    C, G, nh = DN_CHUNK, DN_GROUP, DN_HEADS
    d_all = nh * LANES
    ri = lax.broadcasted_iota(jnp.int32, (C, G * C), 0)
    ci = lax.broadcasted_iota(jnp.int32, (C, G * C), 1) % C
    eyecat = (ri == ci).astype(F32)
    ones = jnp.ones((C, C), F32)
    r2 = lax.broadcasted_iota(jnp.int32, (C, C), 0)
    c2 = lax.broadcasted_iota(jnp.int32, (C, C), 1)
    eye128 = (lax.broadcasted_iota(jnp.int32, (LANES, LANES), 0)
              == lax.broadcasted_iota(jnp.int32, (LANES, LANES), 1)).astype(BF16)
    gb = gb_ref[...]
    ncol = gb.shape[1]
    sel_row = lax.broadcasted_iota(jnp.int32, (ncol, d_all), 0)
    sel_head = lax.broadcasted_iota(jnp.int32, (ncol, d_all), 1) // LANES
    cat_row = lax.broadcasted_iota(jnp.int32, (ncol, G * C), 0)
    cat_head = lax.broadcasted_iota(jnp.int32, (ncol, G * C), 1) // C
    q, k, v = q_ref[...], k_ref[...], v_ref[...]
    hdot = lambda a, b: jnp.dot(a, b, precision=HI, preferred_element_type=F32)
    dirs, groups = (0, 1), [(d, g) for d in (0, 1) for g in range(nh // G)]
    gsl = lambda g: slice(g * G * LANES, (g + 1) * G * LANES)
    csl = lambda g: slice(g * G * C, (g + 1) * G * C)
    incl = [(ri >= ci), (ri <= ci)]
    strict = [(ri > ci), (ri < ci)]
    last = [C - 1, 0]
    gcs = [hdot(((r2 >= c2) if d == 0 else (r2 <= c2)).astype(F32), gb) for d in dirs]
    spread = lambda m, first: hdot(m, (sel_row == first + sel_head).astype(F32))
    gc = [spread(gcs[d], d * nh) for d in dirs]
    bet = [spread(gb, 2 * nh + d * nh) for d in dirs]
    gcol = [hdot(gcs[d], (cat_row == d * nh + g * G + cat_head).astype(F32)) for d, g in groups]
    grow = [hdot(ones, x * eyecat) for x in gcol]
    kbm = [k * bet[d] for d in dirs]
    kq = [_bdot(jnp.concatenate([kbm[d][:, gsl(g)], q[:, gsl(g)]], axis=0), _block_diag(k[:, gsl(g)], G), NT)
          for d, g in groups]
    eg = [jnp.exp(gc[d]) for d in dirs]
    glrow = [gc[d][last[d]:last[d] + 1, :] for d in dirs]
    kd = [k * jnp.exp(glrow[d] - gc[d]) for d in dirs]
    kdt = [_bdot(eye128, jnp.concatenate([kd[d][:, (g * G + j) * LANES:(g * G + j + 1) * LANES] for j in range(G)],
                                         axis=0), NT) for d, g in groups]
    decay = [jnp.where(incl[d], jnp.exp(jnp.where(incl[d], x - y, 0.0)), 0.0)
             for (d, g), x, y in zip(groups, gcol, grow)]
    chains = []
    for (d, g), x, dec, t in zip(groups, kq, decay, kdt):
        akd_ref[d, 0, 0:C, csl(g)] = jnp.where(incl[d], x[C:2 * C] * dec, 0.0).astype(BF16)
        akd_ref[d, 0, C:C + DN_DK, csl(g)] = t.astype(BF16)
        chains.append((d, gsl(g), jnp.where(strict[d], -x[0:C] * dec, 0.0), (v * bet[d])[:, gsl(g)],
                       (kbm[d] * eg[d])[:, gsl(g)]))
    for d in dirs:
        wq_ref[d, C:2 * C, :] = (q * eg[d]).astype(BF16)
        gl_ref[d, 0] = jnp.exp(glrow[d])
    tinv = [eyecat + nm for (_, _, nm, _, _) in chains]
    p = [nm for (_, _, nm, _, _) in chains]
    pbd = [_block_diag(x, G) for x in p]
    for _ in range(int(math.log2(C)) - 1):
        p = [_bdot(x, y) for x, y in zip(p, pbd)]
        pbd = [_block_diag(x, G) for x in p]
        tinv = [t + _bdot(t, y) for t, y in zip(tinv, pbd)]
    for t, (d, gs, _, vbg, kbeg) in zip(tinv, chains):
        u_ref[d, :, gs] = _bdot(t, _block_diag(vbg, G))
        wq_ref[d, 0:C, gs] = _bdot(t, _block_diag(kbeg, G)).astype(BF16)


def _dn_local(q, k, v, gb):
    n, d = q.shape
    C = DN_CHUNK
    nc = n // C
    tok = pl.BlockSpec((C, d), lambda j: (j, 0))
    return pl.pallas_call(
        _dn_local_kernel, grid=(nc,),
        in_specs=[tok, tok, tok, pl.BlockSpec((C, gb.shape[1]), lambda j: (j, 0))],
        out_specs=[pl.BlockSpec((2, C, d), lambda j: (0, j, 0)),
                   pl.BlockSpec((2, 2 * C, d), lambda j: (0, j, 0)),
                   pl.BlockSpec((2, 1, C + DN_DK, DN_HEADS * C), lambda j: (0, j, 0, 0)),
                   pl.BlockSpec((2, 1, 1, d), lambda j: (0, j, 0, 0))],
        out_shape=[jax.ShapeDtypeStruct((2, n, d), F32),
                   jax.ShapeDtypeStruct((2, 2 * n, d), BF16),
                   jax.ShapeDtypeStruct((2, nc, C + DN_DK, DN_HEADS * C), BF16),
                   jax.ShapeDtypeStruct((2, nc, 1, d), F32)],
        compiler_params=_cparams(("parallel",)))(q, k, v, gb)


def _dn_scan_kernel(*refs):
    ins, (of_ref, or_ref, s_ref) = refs[:8], refs[8:]
    C = DN_CHUNK

    @pl.when(pl.program_id(1) == 0)
    def _():
        s_ref[...] = jnp.zeros_like(s_ref)

    chains = [(d, p) for d in range(2) for p in range(DN_HEADS // 2)]
    outs = (of_ref, or_ref)
    lanes = lambda p: slice(2 * p * LANES, (2 * p + 2) * LANES)
    s2 = [jnp.concatenate([s_ref[d, 2 * p], s_ref[d, 2 * p + 1]], axis=-1) for d, p in chains]
    r = [_bdot(ins[4 * d + 1][0, :, lanes(p)], _block_diag(s, 2)) for (d, p), s in zip(chains, s2)]
    vn = [ins[4 * d][0, :, lanes(p)] - x[0:C] for (d, p), x in zip(chains, r)]
    r2 = [_bdot(ins[4 * d + 2][0, 0, :, 2 * p * C:(2 * p + 2) * C], _block_diag(x, 2))
          for (d, p), x in zip(chains, vn)]
    for (d, p), s, x, y in zip(chains, s2, r, r2):
        outs[d][:, lanes(p)] = x[C:2 * C] + y[0:C]
        snew = s * ins[4 * d + 3][0, 0, :, lanes(p)] + y[C:C + DN_DK]
        s_ref[d, 2 * p] = snew[:, 0:LANES]
        s_ref[d, 2 * p + 1] = snew[:, LANES:2 * LANES]


def _dn_scan(B, T, ctx_len, u, wq, akd, gl):
    _, n, d = u.shape
    C = DN_CHUNK
    nch, ncc = T // C, ctx_len // C
    rpos = lambda c: jnp.where(c < ncc, ncc - 1 - c, nch - 1 - (c - ncc))
    pos = (lambda b, c: b * nch + c, lambda b, c: b * nch + rpos(c))
    in_specs, args = [], []
    for dd in range(2):
        p = pos[dd]
        in_specs += [pl.BlockSpec((1, C, d), lambda b, c, p=p, dd=dd: (dd, p(b, c), 0)),
                     pl.BlockSpec((1, 2 * C, d), lambda b, c, p=p, dd=dd: (dd, p(b, c), 0)),
                     pl.BlockSpec((1, 1, C + DN_DK, DN_HEADS * C), lambda b, c, p=p, dd=dd: (dd, p(b, c), 0, 0)),
                     pl.BlockSpec((1, 1, 1, d), lambda b, c, p=p, dd=dd: (dd, p(b, c), 0, 0))]
        args += [u, wq, akd, gl]
    return pl.pallas_call(
        _dn_scan_kernel, grid=(B, nch), in_specs=in_specs,
        out_specs=[pl.BlockSpec((C, d), lambda b, c: (pos[0](b, c), 0)),
                   pl.BlockSpec((C, d), lambda b, c: (pos[1](b, c), 0))],
        out_shape=[jax.ShapeDtypeStruct((n, d), F32)] * 2,
        scratch_shapes=[pltpu.VMEM((2, DN_HEADS, DN_DK, LANES), F32)],
        compiler_params=_cparams(("arbitrary", "arbitrary")))(*args)


def _final_kernel(x_ref, y_ref, mod_ref, o_ref):
    o_ref[0] = x_ref[...] + mod_ref[0, 5:6, :] * y_ref[...]


def _final(B, T, ctx_len, x, y, mod):
    n, d = x.shape
    tm = TOKEN_BLOCK
    nblk, ncb = T // tm, ctx_len // tm
    tok = pl.BlockSpec((tm, d), lambda b, i: (b * nblk + ncb + i, 0))
    return pl.pallas_call(
        _final_kernel, grid=(B, nblk - ncb),
        in_specs=[tok, tok, pl.BlockSpec((1, 6, d), lambda b, i: (b, 0, 0))],
        out_specs=pl.BlockSpec((1, tm, d), lambda b, i: (b, i, 0)),
        out_shape=jax.ShapeDtypeStruct((B, T - ctx_len, d), F32),
        compiler_params=_cparams(("parallel", "parallel")))(x, y, mod)


def _rope_tables(seq, ctx_len):
    rows = seq // GRID_W
    r = jnp.broadcast_to(jnp.arange(rows, dtype=F32)[:, None], (rows, GRID_W)).reshape(-1)
    cl = jnp.broadcast_to(jnp.arange(GRID_W, dtype=F32)[None, :], (rows, GRID_W)).reshape(-1)

    def angles(rot_dim):
        nf = rot_dim // 4
        inv = ROPE_BASE ** (-jnp.arange(nf, dtype=F32) / nf)
        ang = jnp.concatenate([r[:, None] * inv, cl[:, None] * inv], axis=-1)
        ang = jnp.concatenate([jnp.zeros((ctx_len, rot_dim // 2), F32), ang], axis=0)
        return jnp.cos(ang), jnp.sin(ang)

    T = seq + ctx_len
    ca, sa = angles(DIFF_DIM)
    z = jnp.zeros_like(sa)
    ta = jnp.stack([jnp.concatenate([ca, ca] * 2, -1), jnp.concatenate([-sa, z] * 2, -1),
                    jnp.concatenate([z, sa] * 2, -1)])
    cb, sb = angles(MLA_ROPE)
    one, zn, zt = jnp.ones((T, MLA_NOPE), F32), jnp.zeros((T, MLA_NOPE), F32), jnp.zeros((T, LANES - MLA_QK), F32)
    zb = jnp.zeros_like(sb)
    tb = jnp.stack([jnp.concatenate([one, cb, cb, 1.0 + zt], -1), jnp.concatenate([zn, -sb, zb, zt], -1),
                    jnp.concatenate([zn, zb, sb, zt], -1)])
    return ta, tb


def _pad_heads(w, heads, width):
    lead = w.shape[:-1]
    w = w.reshape(lead + (heads, width))
    return jnp.pad(w, [(0, 0)] * len(lead) + [(0, 0), (0, LANES - width)]).reshape(lead + (heads * LANES,))


def _block_diag_ones(n, group):
    idx = np.arange(n) // group
    return jnp.asarray(idx[:, None] == idx[None, :], dtype=BF16)


def kernel(x, c, ctx, c_ctx, ada_w, ada_b, ev_w_in, ev_w_out, diff_q_norm, diff_k_norm, diff_lam_q1, diff_lam_k1, diff_lam_q2, diff_lam_k2, diff_subln, mla_cq_norm, mla_ckv_norm, mla_w_uq, mla_w_ukv, mla_q_norm, mla_k_norm, dn_w_in, dn_conv, dn_a_log, dn_dt_bias, dn_o_norm, dn_w_out, peer_wq, peer_k1, peer_k2, peer_u, peer_v):
    B, S, D = x.shape
    CTX = ctx.shape[1]
    T = CTX + S
    depth = ada_w.shape[0]
    tm = TOKEN_BLOCK
    assert CTX % tm == 0 and S % tm == 0 and (B * T) % PEER_TOKEN_BLOCK == 0 and B + 1 <= 8
    nblk, ncb = T // tm, CTX // tm
    geo = (B, nblk, tm, lambda b, i: jnp.where(i < ncb, B, b))

    cc = jnp.zeros((8, D), F32).at[:B].set(c).at[B].set(c_ctx)
    mods = _adaln(cc, ada_w, ada_b).reshape(depth, 8, 6, D)
    xs = jnp.concatenate([ctx, x], axis=1).reshape(B * T, D)
    ropea, ropeb = _rope_tables(S, CTX)
    bd64, bd128 = _block_diag_ones(HEAD_W, DIFF_DIM), _block_diag_ones(HEAD_W, LANES)

    y = None
    for l in range(depth):
        i = l // 2
        mod = mods[l]
        gmod = mods[l - 1] if l else None
        if l % 2 == 0:
            lam_init = 0.8 - 0.6 * math.exp(-0.3 * l)
            w_in = ev_w_in[i]
            o = 3 * HEAD_W + MLA_Q_RANK + MLA_KV_RANK
            kr_rep = jnp.pad(jnp.broadcast_to(w_in[:, None, o:o + MLA_ROPE], (D, MLA_HEADS, MLA_ROPE)),
                             ((0, 0), (0, 0), (MLA_NOPE, LANES - MLA_QK))).reshape(D, HEAD_W)
            w_cat = jnp.concatenate([w_in[:, :o], kr_rep], axis=1).astype(BF16)
            res = _modmm(geo, xs, mod, w_cat, y, gmod)
            (xs, z) = res if y is not None else (xs, res)
            ukv = mla_w_ukv[i].reshape(MLA_KV_RANK, MLA_HEADS, MLA_NOPE + MLA_VDIM)
            gains = jnp.stack([jnp.tile(diff_q_norm[i], HEAD_W // DIFF_DIM), jnp.tile(diff_k_norm[i], HEAD_W // DIFF_DIM),
                               jnp.tile(jnp.pad(mla_q_norm[i], (0, LANES - MLA_QK)), MLA_HEADS),
                               jnp.tile(jnp.pad(mla_k_norm[i], (0, LANES - MLA_QK)), MLA_HEADS)])
            qa, ka, va, qb, kb, vb = _even_prep(
                geo, z, ropea, ropeb, gains, mla_cq_norm[i][None, :], mla_ckv_norm[i][None, :],
                _pad_heads(mla_w_uq[i], MLA_HEADS, MLA_QK).astype(BF16),
                _pad_heads(ukv[:, :, :MLA_NOPE].reshape(MLA_KV_RANK, -1), MLA_HEADS, MLA_NOPE).astype(BF16),
                ukv[:, :, MLA_NOPE:].reshape(MLA_KV_RANK, -1).astype(BF16), bd64, bd128)
            lam_vecs = jnp.stack([diff_lam_q1[i], diff_lam_k1[i], diff_lam_q2[i], diff_lam_k2[i]])
            oa = _attention(B, T, CTX, qa, ka, va, lam_vecs, diff_subln[i][None, :], lam_init)
            ob = _attention(B, T, CTX, qb, kb, vb)
            xs, hqt = _mixer_out(geo, _even_out_kernel, [(oa, HEAD_W, 0), (ob, HEAD_W, 0)], [], xs, mod,
                                 ev_w_out[i].astype(BF16))
        else:
            nin = dn_w_in.shape[2]
            w_in = jnp.pad(dn_w_in[i], ((0, 0), (0, -nin % LANES))).astype(BF16)
            res = _modmm(geo, xs, mod, w_in, y, gmod)
            (xs, z) = res if y is not None else (xs, res)
            q, k, v, gb = _dn_prep(geo, ncb, z, dn_conv[i], dn_a_log[i].reshape(1, -1),
                                   dn_dt_bias[i].reshape(1, -1))
            of, orv = _dn_scan(B, T, CTX, *_dn_local(q, k, v, gb))
            xs, hqt = _mixer_out(geo, _dn_out_kernel,
                                 [(of, D, 0), (orv, D, 0), (z, D, DN_QKV // D)], [dn_o_norm[i][None, :]], xs, mod,
                                 dn_w_out[i].astype(BF16))
        y = _peer(hqt, peer_wq[l], peer_k1[l], peer_k2[l], peer_u[l], peer_v[l])
    return _final(B, T, CTX, xs, y, mods[depth - 1])
```

```python
import functools
import math

import jax
import jax.numpy as jnp
import numpy as np
from jax import lax
from jax.experimental import pallas as pl
from jax.experimental.pallas import tpu as pltpu

F32 = jnp.float32
BF16 = jnp.bfloat16
HI = lax.Precision.HIGHEST
NT = (((1,), (1,)), ((), ()))

EPS = 1e-6
LOG2E = math.log2(math.e)
ROPE_BASE = 10000.0
GRID_W = 64
LANES = 128
TOKEN_BLOCK = 256
VMEM_LIMIT = 56 * 1024 * 1024

DIFF_HEADS, DIFF_DIM = 4, 64
MLA_HEADS, MLA_Q_RANK, MLA_KV_RANK, MLA_NOPE, MLA_ROPE, MLA_VDIM = 4, 256, 128, 64, 32, 128
MLA_QK = MLA_NOPE + MLA_ROPE
HEAD_W = DIFF_HEADS * LANES
DN_HEADS, DN_DK, DN_CONV, DN_CHUNK = 8, 128, 5, 64
DN_QKV = 3 * DN_HEADS * DN_DK
PEER_HEADS, PEER_NKEYS, PEER_TOPK = 8, 128, 16
PEER_CAND_ROWS = 80
PEER_TOKEN_BLOCK = 512
PEER_EXPERT_BLOCK = 1024


def _cparams(sem):
    return pltpu.CompilerParams(dimension_semantics=sem, vmem_limit_bytes=VMEM_LIMIT)


def _rms_rows(x):
    return x * lax.rsqrt(jnp.mean(x * x, axis=-1, keepdims=True) + EPS)


def _silu(x):
    return x * jax.nn.sigmoid(x)


def _group_sum(sq, bd):
    hi = sq.astype(BF16)
    lo = (sq - hi.astype(F32)).astype(BF16)
    return jnp.dot(hi, bd, preferred_element_type=F32) + jnp.dot(lo, bd, preferred_element_type=F32)


def _adaln_kernel(c_ref, w_ref, b_ref, o_ref):
    s = _silu(c_ref[...])
    o_ref[0] = jnp.dot(s, w_ref[0], precision=HI, preferred_element_type=F32) + b_ref[0]


def _adaln(cc, ada_w, ada_b):
    depth, d, n = ada_w.shape
    tn = 1536
    return pl.pallas_call(
        _adaln_kernel,
        grid=(depth, n // tn),
        in_specs=[pl.BlockSpec((8, d), lambda l, j: (0, 0)),
                  pl.BlockSpec((1, d, tn), lambda l, j: (l, 0, j)),
                  pl.BlockSpec((1, 1, tn), lambda l, j: (l, 0, j))],
        out_specs=pl.BlockSpec((1, 8, tn), lambda l, j: (l, 0, j)),
        out_shape=jax.ShapeDtypeStruct((depth, 8, n), F32),
        compiler_params=_cparams(("arbitrary", "arbitrary")),
    )(cc, ada_w, ada_b.reshape(depth, 1, n))


def _modmm_kernel(*refs, has_y):
    if has_y:
        x_ref, y_ref, gmod_ref, mod_ref, w_ref, xo_ref, z_ref = refs
        x = x_ref[...] + gmod_ref[0, 5:6, :] * y_ref[...]
        xo_ref[...] = x
    else:
        x_ref, mod_ref, w_ref, z_ref = refs
        x = x_ref[...]
    h = _rms_rows(x) * (1.0 + mod_ref[0, 1:2, :]) + mod_ref[0, 0:1, :]
    z_ref[...] = jnp.dot(h.astype(BF16), w_ref[...], preferred_element_type=F32)


def _modmm(geo, x, mod, w, y=None, gmod=None):
    B, nblk, tm, mrow = geo
    n, d = x.shape
    nout = w.shape[1]
    tok = pl.BlockSpec((tm, d), lambda b, i: (b * nblk + i, 0))
    modspec = pl.BlockSpec((1, 6, d), lambda b, i: (mrow(b, i), 0, 0))
    wspec = pl.BlockSpec((d, nout), lambda b, i: (0, 0))
    zspec = pl.BlockSpec((tm, nout), lambda b, i: (b * nblk + i, 0))
    zshape = jax.ShapeDtypeStruct((n, nout), F32)
    if y is None:
        return pl.pallas_call(
            functools.partial(_modmm_kernel, has_y=False), grid=(B, nblk),
            in_specs=[tok, modspec, wspec], out_specs=zspec, out_shape=zshape,
            compiler_params=_cparams(("parallel", "parallel")))(x, mod, w)
    return pl.pallas_call(
        functools.partial(_modmm_kernel, has_y=True), grid=(B, nblk),
        in_specs=[tok, tok, modspec, modspec, wspec], out_specs=[tok, zspec],
        out_shape=[jax.ShapeDtypeStruct((n, d), F32), zshape],
        compiler_params=_cparams(("parallel", "parallel")))(x, y, gmod, mod, w)


def _rope_lanes(v, tab_ref, half):
    rep = v.shape[-1] // LANES
    c = jnp.concatenate([tab_ref[0]] * rep, axis=-1)
    sm = jnp.concatenate([tab_ref[1]] * rep, axis=-1)
    sp = jnp.concatenate([tab_ref[2]] * rep, axis=-1)
    n = v.shape[-1]
    return v * c + pltpu.roll(v, n - half, 1) * sm + pltpu.roll(v, half, 1) * sp


def _even_prep_kernel(z_ref, ropea_ref, ropeb_ref, gains_ref, cqn_ref, ckvn_ref, wuq_ref, wuk_ref, wuv_ref,
                      bd64_ref, bd128_ref, qa_ref, ka_ref, va_ref, qb_ref, kb_ref, vb_ref):
    W = HEAD_W
    bd64 = bd64_ref[...]
    bd128 = bd128_ref[...]

    def norm_groups(v, bd, width, gain):
        ms = _group_sum(v * v, bd) * (1.0 / width)
        return v * lax.rsqrt(ms + EPS) * gain

    qa = norm_groups(z_ref[:, 0:W], bd64, DIFF_DIM, gains_ref[0:1, :])
    ka = norm_groups(z_ref[:, W:2 * W], bd64, DIFF_DIM, gains_ref[1:2, :])
    qa_ref[...] = (_rope_lanes(qa, ropea_ref, DIFF_DIM // 2) * (DIFF_DIM ** -0.5 * LOG2E)).astype(BF16)
    ka_ref[...] = _rope_lanes(ka, ropea_ref, DIFF_DIM // 2).astype(BF16)
    va_ref[...] = z_ref[:, 2 * W:3 * W].astype(BF16)

    o = 3 * W
    cq = (_rms_rows(z_ref[:, o:o + MLA_Q_RANK]) * cqn_ref[...]).astype(BF16)
    o += MLA_Q_RANK
    ckv = (_rms_rows(z_ref[:, o:o + MLA_KV_RANK]) * ckvn_ref[...]).astype(BF16)
    o += MLA_KV_RANK
    kr = z_ref[:, o:o + W]
    qb = jnp.dot(cq, wuq_ref[...], preferred_element_type=F32)
    kb = jnp.dot(ckv, wuk_ref[...], preferred_element_type=F32) + kr
    qb = norm_groups(qb, bd128, MLA_QK, gains_ref[2:3, :])
    kb = norm_groups(kb, bd128, MLA_QK, gains_ref[3:4, :])
    qb_ref[...] = (_rope_lanes(qb, ropeb_ref, MLA_ROPE // 2) * (MLA_QK ** -0.5 * LOG2E)).astype(BF16)
    kb_ref[...] = _rope_lanes(kb, ropeb_ref, MLA_ROPE // 2).astype(BF16)
    vb_ref[...] = jnp.dot(ckv, wuv_ref[...], preferred_element_type=F32).astype(BF16)


def _even_prep(geo, z, ropea, ropeb, gains, cqn, ckvn, wuq, wuk, wuv, bd64, bd128):
    B, nblk, tm, _ = geo
    n = z.shape[0]
    W = HEAD_W
    full = lambda a: pl.BlockSpec(a.shape, lambda b, i: (0,) * a.ndim)
    rope = pl.BlockSpec((3, tm, LANES), lambda b, i: (0, i, 0))
    out = pl.BlockSpec((tm, W), lambda b, i: (b * nblk + i, 0))
    return pl.pallas_call(
        _even_prep_kernel, grid=(B, nblk),
        in_specs=[pl.BlockSpec((tm, z.shape[1]), lambda b, i: (b * nblk + i, 0)), rope, rope, full(gains),
                  full(cqn), full(ckvn), full(wuq), full(wuk), full(wuv), full(bd64), full(bd128)],
        out_specs=[out] * 6, out_shape=[jax.ShapeDtypeStruct((n, W), BF16)] * 6,
        compiler_params=_cparams(("parallel", "parallel")))(z, ropea, ropeb, gains, cqn, ckvn, wuq, wuk, wuv,
                                                            bd64, bd128)


def _attn_kernel(*refs, diff, tk, row_split, n_ctx_q, ctx_len, n_keys, lam_init):
    if diff:
        q_ref, k_ref, v_ref, lam_ref, subln_ref, o_ref, s_ref = refs
    else:
        q_ref, k_ref, v_ref, o_ref, s_ref = refs
    qi = pl.program_id(2)
    tr = q_ref.shape[0] // row_split
    qs = []
    for r in range(row_split):
        q = q_ref[r * tr:(r + 1) * tr, :]
        if diff:
            lane = lax.broadcasted_iota(jnp.int32, q.shape, 1)
            zero = jnp.zeros_like(q)
            qs += [jnp.where(lane < DIFF_DIM, q, zero), jnp.where(lane >= DIFF_DIM, q, zero)]
        else:
            qs.append(q)
    nsub = len(qs) // row_split

    def scores(slot, start, size):
        k = k_ref[pl.ds(start, size), :]
        s = [lax.dot_general(qq, k, NT, preferred_element_type=F32) for qq in qs]
        for j, x in enumerate(s):
            s_ref[slot, j, :, 0:size] = x
        return tuple(jnp.max(x, axis=-1, keepdims=True) for x in s)

    def update(slot, start, size, mx, carry):
        v = v_ref[pl.ds(start, size), :]
        mn = [jnp.maximum(m, x) for (m, _, _), x in zip(carry, mx)]
        p = [jnp.exp2(s_ref[slot, j, :, 0:size] - y) for j, y in enumerate(mn)]
        alpha = [jnp.exp2(m - y) for (m, _, _), y in zip(carry, mn)]
        pv = [jnp.dot(x.astype(BF16), v, preferred_element_type=F32) for x in p]
        return tuple((y, a * l + jnp.sum(x, axis=-1, keepdims=True), a * acc + z)
                     for (_, l, acc), y, a, x, z in zip(carry, mn, alpha, p, pv))

    init = tuple((jnp.full((tr, 1), -1e30, F32), jnp.zeros((tr, 1), F32), jnp.zeros((tr, LANES), F32))
                 for _ in qs)

    def finish(carry):
        outs = [acc / l for (_, l, acc) in carry]
        for r in range(row_split):
            if diff:
                lam = (jnp.exp(jnp.sum(lam_ref[0:1, :] * lam_ref[1:2, :], axis=-1, keepdims=True))
                       - jnp.exp(jnp.sum(lam_ref[2:3, :] * lam_ref[3:4, :], axis=-1, keepdims=True)) + lam_init)
                o = _rms_rows(outs[r * nsub] - lam * outs[r * nsub + 1]) * subln_ref[...] * (1.0 - lam_init)
            else:
                o = outs[r]
            o_ref[r * tr:(r + 1) * tr, :] = o.astype(o_ref.dtype)

    @pl.when(qi < n_ctx_q)
    def _():
        finish(update(0, 0, ctx_len, scores(0, 0, ctx_len), init))

    @pl.when(qi >= n_ctx_q)
    def _():
        n = n_keys // tk
        at = lambda c: pl.multiple_of(c * tk, tk)

        def pair(j, state):
            carry, mx0 = state
            mx1 = scores(1, at(2 * j + 1), tk)
            carry = update(0, at(2 * j), tk, mx0, carry)
            mx0 = scores(0, at(2 * j + 2), tk)
            carry = update(1, at(2 * j + 1), tk, mx1, carry)
            return carry, mx0

        carry, mx0 = lax.fori_loop(0, (n - 1) // 2, pair, (init, scores(0, 0, tk)))
        if n % 2 == 0:
            mx1 = scores(1, (n - 1) * tk, tk)
            carry = update(0, (n - 2) * tk, tk, mx0, carry)
            carry = update(1, (n - 1) * tk, tk, mx1, carry)
        else:
            carry = update(0, (n - 1) * tk, tk, mx0, carry)
        finish(carry)


def _attention(B, T, ctx_len, q, k, v, lam_vecs=None, subln=None, lam_init=0.0):
    diff = lam_vecs is not None
    tq = TOKEN_BLOCK
    tk = next(c for c in (1408, 768, TOKEN_BLOCK) if T % c == 0)
    nq = T // tq
    heads = q.shape[1] // LANES
    qspec = pl.BlockSpec((tq, LANES), lambda b, h, i: (b * nq + i, h))
    kvspec = pl.BlockSpec((T, LANES), lambda b, h, i: (b, h))
    in_specs = [qspec, kvspec, kvspec]
    args = [q, k, v]
    if diff:
        in_specs += [pl.BlockSpec(lam_vecs.shape, lambda b, h, i: (0, 0)),
                     pl.BlockSpec(subln.shape, lambda b, h, i: (0, 0))]
        args += [lam_vecs, subln]
    row_split = 1 if diff else 2
    chains = row_split * (2 if diff else 1)
    return pl.pallas_call(
        functools.partial(_attn_kernel, diff=diff, tk=tk, row_split=row_split, n_ctx_q=ctx_len // tq,
                          ctx_len=ctx_len, n_keys=T, lam_init=lam_init),
        grid=(B, heads, nq), in_specs=in_specs, out_specs=qspec,
        out_shape=jax.ShapeDtypeStruct(q.shape, BF16),
        scratch_shapes=[pltpu.VMEM((2, chains, tq // row_split, max(tk, ctx_len)), F32)],
        compiler_params=_cparams(("parallel", "parallel", "arbitrary")))(*args)


def _residual_tail(x, y, mod_ref, xo_ref, hqt_ref):
    xn = x + mod_ref[0, 2:3, :] * y
    xo_ref[...] = xn
    hq = _rms_rows(xn) * (1.0 + mod_ref[0, 4:5, :]) + mod_ref[0, 3:4, :]
    hqt_ref[...] = hq.T.astype(BF16)


def _even_out_kernel(oa_ref, ob_ref, x_ref, mod_ref, wo_ref, xo_ref, hqt_ref):
    W = HEAD_W
    y = (jnp.dot(oa_ref[...], wo_ref[0:W, :], preferred_element_type=F32)
         + jnp.dot(ob_ref[...], wo_ref[W:2 * W, :], preferred_element_type=F32))
    _residual_tail(x_ref[...], y, mod_ref, xo_ref, hqt_ref)


def _dn_out_kernel(of_ref, or_ref, zg_ref, onorm_ref, x_ref, mod_ref, wo_ref, xo_ref, hqt_ref):
    parts = []
    for h in range(DN_HEADS):
        hs = slice(h * LANES, (h + 1) * LANES)
        o = of_ref[:, hs] + or_ref[:, hs]
        parts.append((_rms_rows(o) * onorm_ref[...] * _silu(zg_ref[:, hs])).astype(BF16))
    y = jnp.dot(jnp.concatenate(parts, axis=-1), wo_ref[...], preferred_element_type=F32)
    _residual_tail(x_ref[...], y, mod_ref, xo_ref, hqt_ref)


def _mixer_out(geo, kernel_fn, token_args, small_args, x, mod, wo):
    B, nblk, tm, mrow = geo
    n, d = x.shape
    tokspec = lambda a, col: pl.BlockSpec((tm, a[1]), lambda b, i: (b * nblk + i, col))
    in_specs = [tokspec((a, w), col) for (a, w, col) in token_args]
    in_specs += [pl.BlockSpec(a.shape, lambda b, i: (0,) * a.ndim) for a in small_args]
    in_specs += [pl.BlockSpec((tm, d), lambda b, i: (b * nblk + i, 0)),
                 pl.BlockSpec((1, 6, d), lambda b, i: (mrow(b, i), 0, 0)),
                 pl.BlockSpec(wo.shape, lambda b, i: (0, 0))]
    return pl.pallas_call(
        kernel_fn, grid=(B, nblk), in_specs=in_specs,
        out_specs=[pl.BlockSpec((tm, d), lambda b, i: (b * nblk + i, 0)),
                   pl.BlockSpec((d, tm), lambda b, i: (0, b * nblk + i))],
        out_shape=[jax.ShapeDtypeStruct((n, d), F32), jax.ShapeDtypeStruct((d, n), BF16)],
        compiler_params=_cparams(("parallel", "parallel")))(
            *[a for (a, _, _) in token_args], *small_args, x, mod, wo)


def _peer_select_kernel(hqt_ref, wqt_ref, k1_ref, k2_ref, th_ref, e1_ref, s2_ref, e2_ref, work_ref, top_ref, cand_ref):
    K, H = PEER_TOPK, PEER_HEADS
    qt = jnp.dot(wqt_ref[...], hqt_ref[...], preferred_element_type=F32)
    dk = k1_ref.shape[1]
    for h in range(H):
        q1 = qt[(2 * h) * dk:(2 * h + 1) * dk, :].astype(BF16)
        q2 = qt[(2 * h + 1) * dk:(2 * h + 2) * dk, :].astype(BF16)
        s1 = jnp.dot(k1_ref[...], q1, preferred_element_type=F32)
        s2 = jnp.dot(k2_ref[...], q2, preferred_element_type=F32)
        th_ref[h] = s1
        s2_ref[h] = s2
        work_ref[2 * h] = s1
        work_ref[2 * h + 1] = s2

    def extract(ref, count):
        s = [ref[a] for a in range(count)]
        m = [jnp.max(x, axis=0, keepdims=True) for x in s]
        for a in range(count):
            ref[a] = jnp.where(s[a] == m[a], -jnp.inf, s[a])
        return m

    def top_round(r, carry):
        for a, m in enumerate(extract(work_ref, 2 * H)):
            top_ref[a, pl.ds(r, 1), :] = m
        return carry

    lax.fori_loop(0, K, top_round, 0)

    def candidates(h):
        v1, v2 = top_ref[2 * h], top_ref[2 * h + 1]
        row8 = lax.broadcasted_iota(jnp.int32, (8, v1.shape[1]), 0)
        pieces = [v1[0:1, :] + v2, v1[1:2, :] + v2[0:8, :]]
        pieces += [jnp.where(row8 < K // (r1 + 1), v1[r1:r1 + 1, :] + v2[0:8, :], -jnp.inf) for r1 in range(2, 8)]
        pieces.append(v1[8:16, :] + v2[0:1, :])
        return jnp.concatenate(pieces, axis=0)

    for h in range(H):
        cand_ref[h] = candidates(h)
    tau = lax.fori_loop(0, K, lambda r, carry: tuple(extract(cand_ref, H)),
                        tuple(jnp.zeros((1, hqt_ref.shape[1]), F32) for _ in range(H)))
    for h in range(H):
        cand = candidates(h)
        m1, m2 = top_ref[2 * h, 0:1, :], top_ref[2 * h + 1, 0:1, :]
        zsum = jnp.sum(jnp.where(cand >= tau[h], jnp.exp(cand - (m1 + m2)), 0.0), axis=0, keepdims=True)
        s1 = th_ref[h]
        e1_ref[h] = jnp.exp(s1 - m1) / zsum
        e2_ref[h] = jnp.exp(s2_ref[h] - m2)
        v2 = top_ref[2 * h + 1]
        theta = jnp.full(s1.shape, jnp.inf, F32)
        for r in range(K):
            theta = jnp.where((s1 + v2[r:r + 1, :]) >= tau[h], v2[r:r + 1, :], theta)
        th_ref[h] = theta


def _peer_select(hqt, wqt, k1, k2):
    d, n = hqt.shape
    tm = TOKEN_BLOCK
    H, NK = PEER_HEADS, PEER_NKEYS
    big = pl.BlockSpec((H, NK, tm), lambda j: (0, 0, j))
    bigshape = jax.ShapeDtypeStruct((H, NK, n), F32)
    full = lambda a: pl.BlockSpec(a.shape, lambda j: (0,) * a.ndim)
    return pl.pallas_call(
        _peer_select_kernel, grid=(n // tm,),
        in_specs=[pl.BlockSpec((d, tm), lambda j: (0, j)), full(wqt), full(k1), full(k2)],
        out_specs=[big] * 4, out_shape=[bigshape] * 4,
        scratch_shapes=[pltpu.VMEM((2 * H, NK, tm), F32), pltpu.VMEM((2 * H, PEER_TOPK, tm), F32),
                        pltpu.VMEM((H, PEER_CAND_ROWS, tm), F32)],
        compiler_params=_cparams(("parallel",)))(hqt, wqt, k1, k2)


def _gelu(a):
    return 0.5 * a * (1.0 + lax.erf(a * np.float32(math.sqrt(0.5))))


PEER_PIECE = 2 * LANES


def _peer_dense_kernel(hqt_ref, u_ref, vt_ref, th_ref, e1_ref, s2_ref, e2_ref, zero_ref, y_ref, row_ref,
                       *piece_refs):
    c = pl.program_id(1)
    NK = PEER_NKEYS
    n_i = PEER_EXPERT_BLOCK // NK
    pieces = [piece_refs[3 * p:3 * p + 3] for p in range(len(piece_refs) // 3)]

    @pl.when(c == 0)
    def _():
        for acc_ref, _, _ in pieces:
            acc_ref[...] = jnp.zeros_like(acc_ref)

    H = PEER_HEADS
    strips = [slice(ts * LANES, (ts + 1) * LANES) for ts in range(hqt_ref.shape[1] // LANES)]
    for ii in range(n_i):
        for h in range(H):
            i = c * n_i + ii
            row_ref[0, ii, h:h + 1, :] = th_ref[h, pl.ds(i, 1), :]
            row_ref[1, ii, h:h + 1, :] = e1_ref[h, pl.ds(i, 1), :]

    JB, RUNS, PIECE = 16, 8, PEER_PIECE
    zero = zero_ref[0:JB, :]
    for p, (_, act_ref, _) in enumerate(pieces):
        pc = slice(p * PIECE, (p + 1) * PIECE)
        act_ref[...] = _gelu(jnp.dot(u_ref[...], hqt_ref[:, pc], preferred_element_type=F32))
    for p, (acc_ref, act_ref, wt_ref) in enumerate(pieces):
        for ts in range(PIECE // LANES):
            tl = strips[p * PIECE // LANES + ts]
            pl_ = strips[ts]
            for i0 in range(0, n_i, RUNS):
                for jb in range(NK // JB):
                    js = slice(jb * JB, (jb + 1) * JB)
                    g = [jnp.zeros((JB, LANES), F32) for _ in range(RUNS)]
                    for h in range(H):
                        s2 = s2_ref[h, js, tl] + zero
                        e2 = e2_ref[h, js, tl] + zero
                        for r in range(RUNS):
                            sel = s2 >= row_ref[0, i0 + r, h:h + 1, tl]
                            g[r] = g[r] + jnp.where(sel, row_ref[1, i0 + r, h:h + 1, tl] * e2, 0.0)
                    for r in range(RUNS):
                        rows = slice((i0 + r) * NK + jb * JB, (i0 + r) * NK + (jb + 1) * JB)
                        wt_ref[rows, pl_] = (g[r] * act_ref[rows, pl_]).astype(BF16)
        acc_ref[...] += jnp.dot(vt_ref[...], wt_ref[...], preferred_element_type=F32)

    @pl.when(c == pl.num_programs(1) - 1)
    def _():
        for p, (acc_ref, _, _) in enumerate(pieces):
            y_ref[p * PIECE:(p + 1) * PIECE, :] = acc_ref[...].T


def _peer_dense(hqt, u, vt, theta, e1, s2, e2):
    d, n = hqt.shape
    tm, te = PEER_TOKEN_BLOCK, PEER_EXPERT_BLOCK
    H, NK = PEER_HEADS, PEER_NKEYS
    big = pl.BlockSpec((H, NK, tm), lambda j, c: (0, 0, j))
    return pl.pallas_call(
        _peer_dense_kernel, grid=(n // tm, u.shape[0] // te),
        in_specs=[pl.BlockSpec((d, tm), lambda j, c: (0, j)),
                  pl.BlockSpec((te, d), lambda j, c: (c, 0)),
                  pl.BlockSpec((d, te), lambda j, c: (0, c)),
                  big, big, big, big, pl.BlockSpec((NK, LANES), lambda j, c: (0, 0))],
        out_specs=pl.BlockSpec((tm, d), lambda j, c: (j, 0)),
        out_shape=jax.ShapeDtypeStruct((n, d), F32),
        scratch_shapes=[pltpu.VMEM((2, te // NK, H, tm), F32)]
        + [pltpu.VMEM((d, PEER_PIECE), F32), pltpu.VMEM((te, PEER_PIECE), F32),
           pltpu.VMEM((te, PEER_PIECE), BF16)] * (tm // PEER_PIECE),
        compiler_params=_cparams(("parallel", "arbitrary")))(hqt, u, vt, theta, e1, s2, e2,
                                                             jnp.zeros((NK, LANES), F32))


def _peer(hqt, wq, k1, k2, u_tab, v_tab):
    sel = _peer_select(hqt, wq.T.astype(BF16), k1.astype(BF16), k2.astype(BF16))
    return _peer_dense(hqt, u_tab.astype(BF16), v_tab.T.astype(BF16), *sel)


def _dn_prep_kernel(z_ref, prev_ref, next_ref, conv_ref, alog_ref, dtb_ref, ab_ref, q_ref, k_ref, v_ref, gb_ref,
                    ext_ref, *, n_ctx_blk, nblk):
    i = pl.program_id(1)
    tm = z_ref.shape[0]
    halo = prev_ref.shape[0]
    pad = DN_CONV // 2
    has_prev = jnp.logical_and(i != 0, i != n_ctx_blk)
    has_next = jnp.logical_and(i != n_ctx_blk - 1, i != nblk - 1)
    ext_ref[0:halo, :] = jnp.where(has_prev, prev_ref[...], 0.0)
    ext_ref[halo:halo + tm, :] = z_ref[...]
    ext_ref[halo + tm:, :] = jnp.where(has_next, next_ref[...], 0.0)
    nq = DN_HEADS * DN_DK
    for j in range(DN_QKV // LANES):
        cs = slice(j * LANES, (j + 1) * LANES)
        acc = conv_ref[0:1, cs] * ext_ref[halo - pad:halo - pad + tm, cs]
        for t in range(1, DN_CONV):
            acc = acc + conv_ref[t:t + 1, cs] * ext_ref[halo - pad + t:halo - pad + t + tm, cs]
        y = _silu(acc)
        if j * LANES < 2 * nq:
            y = y * lax.rsqrt(jnp.sum(y * y, axis=-1, keepdims=True) + EPS)
        if j * LANES < nq:
            q_ref[:, cs] = y * DN_DK ** -0.5
        elif j * LANES < 2 * nq:
            k_ref[:, slice(j * LANES - nq, (j + 1) * LANES - nq)] = y
        else:
            v_ref[:, slice(j * LANES - 2 * nq, (j + 1) * LANES - 2 * nq)] = y
    ab = ab_ref[...]
    nh = 2 * DN_HEADS
    xa = ab[:, 0:nh] + dtb_ref[...]
    softplus = jnp.maximum(xa, 0.0) + jnp.log(1.0 + jnp.exp(-jnp.abs(xa)))
    gb_ref[:, 0:nh] = -jnp.exp(alog_ref[...]) * softplus
    gb_ref[:, nh:2 * nh] = jax.nn.sigmoid(ab[:, nh:2 * nh])


def _dn_prep(geo, n_ctx_blk, z, conv_w, alog, dtb):
    B, nblk, tm, _ = geo
    n = z.shape[0]
    halo = 8
    r = tm // halo
    nh8 = n // halo
    d = DN_HEADS * DN_DK
    prev = pl.BlockSpec((halo, DN_QKV), lambda b, i: (jnp.maximum((b * nblk + i) * r - 1, 0), 0))
    nxt = pl.BlockSpec((halo, DN_QKV), lambda b, i: (jnp.minimum((b * nblk + i + 1) * r, nh8 - 1), 0))
    ab = z[:, DN_QKV + d:DN_QKV + d + 4 * DN_HEADS]
    tok = pl.BlockSpec((tm, d), lambda b, i: (b * nblk + i, 0))
    gbspec = pl.BlockSpec((tm, 4 * DN_HEADS), lambda b, i: (b * nblk + i, 0))
    full = lambda a: pl.BlockSpec(a.shape, lambda b, i: (0,) * a.ndim)
    return pl.pallas_call(
        functools.partial(_dn_prep_kernel, n_ctx_blk=n_ctx_blk, nblk=nblk), grid=(B, nblk),
        in_specs=[pl.BlockSpec((tm, DN_QKV), lambda b, i: (b * nblk + i, 0)), prev, nxt, full(conv_w),
                  full(alog), full(dtb), gbspec],
        out_specs=[tok, tok, tok, gbspec],
        out_shape=[jax.ShapeDtypeStruct((n, d), F32)] * 3 + [jax.ShapeDtypeStruct((n, 4 * DN_HEADS), F32)],
        scratch_shapes=[pltpu.VMEM((tm + 2 * halo, DN_QKV), F32)],
        compiler_params=_cparams(("parallel", "parallel")))(z, z, z, conv_w, alog, dtb, ab)


DN_GROUP = 4


def _bdot(a, b, dims=None):
    a, b = a.astype(BF16), b.astype(BF16)
    if dims is None:
        return jnp.dot(a, b, preferred_element_type=F32)
    return lax.dot_general(a, b, dims, preferred_element_type=F32)


def _block_diag(x, nblk):
    r, n = x.shape
    w = n // nblk
    tall = jnp.concatenate([x] * nblk, axis=0)
    rb = lax.broadcasted_iota(jnp.int32, tall.shape, 0) // r
    lb = lax.broadcasted_iota(jnp.int32, tall.shape, 1) // w
    return jnp.where(rb == lb, tall, jnp.zeros_like(tall))


def _dn_local_kernel(q_ref, k_ref, v_ref, gb_ref, u_ref, wq_ref, akd_ref, gl_ref):
    C, G, nh = DN_CHUNK, DN_GROUP, DN_HEADS
    d_all = nh * LANES
    ri = lax.broadcasted_iota(jnp.int32, (C, G * C), 0)
    ci = lax.broadcasted_iota(jnp.int32, (C, G * C), 1) % C
    eyecat = (ri == ci).astype(F32)
    ones = jnp.ones((C, C), F32)
    r2 = lax.broadcasted_iota(jnp.int32, (C, C), 0)
    c2 = lax.broadcasted_iota(jnp.int32, (C, C), 1)
    eye128 = (lax.broadcasted_iota(jnp.int32, (LANES, LANES), 0)
              == lax.broadcasted_iota(jnp.int32, (LANES, LANES), 1)).astype(BF16)
    ncol = gb_ref.shape[1]
    sel_row = lax.broadcasted_iota(jnp.int32, (ncol, d_all), 0)
    sel_head = lax.broadcasted_iota(jnp.int32, (ncol, d_all), 1) // LANES
    cat_row = lax.broadcasted_iota(jnp.int32, (ncol, G * C), 0)
    cat_head = lax.broadcasted_iota(jnp.int32, (ncol, G * C), 1) // C
    hdot = lambda a, b: jnp.dot(a, b, precision=HI, preferred_element_type=F32)
    nblk = q_ref.shape[0] // C
    rows = lambda b: slice(b * C, (b + 1) * C)
    q, k, v, gb = ([ref[rows(b), :] for b in range(nblk)] for ref in (q_ref, k_ref, v_ref, gb_ref))
    dirs = [(b, d) for b in range(nblk) for d in (0, 1)]
    groups = [(b, d, g) for b, d in dirs for g in range(nh // G)]
    gsl = lambda g: slice(g * G * LANES, (g + 1) * G * LANES)
    csl = lambda g: slice(g * G * C, (g + 1) * G * C)
    incl = [(ri >= ci), (ri <= ci)]
    strict = [(ri > ci), (ri < ci)]
    last = [C - 1, 0]
    tri = [(r2 >= c2).astype(F32), (r2 <= c2).astype(F32)]
    gcs = {(b, d): hdot(tri[d], gb[b]) for b, d in dirs}
    spread = lambda m, first: hdot(m, (sel_row == first + sel_head).astype(F32))
    gc = {(b, d): spread(gcs[b, d], d * nh) for b, d in dirs}
    bet = {(b, d): spread(gb[b], 2 * nh + d * nh) for b, d in dirs}
    gcol = {(b, d, g): hdot(gcs[b, d], (cat_row == d * nh + g * G + cat_head).astype(F32)) for b, d, g in groups}
    grow = {key: hdot(ones, x * eyecat) for key, x in gcol.items()}
    kbm = {(b, d): k[b] * bet[b, d] for b, d in dirs}
    kq = {(b, d, g): _bdot(jnp.concatenate([kbm[b, d][:, gsl(g)], q[b][:, gsl(g)]], axis=0),
                           _block_diag(k[b][:, gsl(g)], G), NT) for b, d, g in groups}
    eg = {key: jnp.exp(x) for key, x in gc.items()}
    glrow = {(b, d): gc[b, d][last[d]:last[d] + 1, :] for b, d in dirs}
    kd = {(b, d): k[b] * jnp.exp(glrow[b, d] - gc[b, d]) for b, d in dirs}
    kdt = {(b, d, g): _bdot(eye128, jnp.concatenate(
        [kd[b, d][:, (g * G + j) * LANES:(g * G + j + 1) * LANES] for j in range(G)], axis=0), NT)
        for b, d, g in groups}
    decay = {(b, d, g): jnp.where(incl[d], jnp.exp(jnp.where(incl[d], gcol[b, d, g] - grow[b, d, g], 0.0)), 0.0)
             for b, d, g in groups}
    nm, vbg, kbeg = {}, {}, {}
    for b, d, g in groups:
        x, dec = kq[b, d, g], decay[b, d, g]
        akd_ref[d, b, 0:C, csl(g)] = jnp.where(incl[d], x[C:2 * C] * dec, 0.0).astype(BF16)
        akd_ref[d, b, C:C + DN_DK, csl(g)] = kdt[b, d, g].astype(BF16)
        nm[b, d, g] = jnp.where(strict[d], -x[0:C] * dec, 0.0)
        vbg[b, d, g] = (v[b] * bet[b, d])[:, gsl(g)]
        kbeg[b, d, g] = (kbm[b, d] * eg[b, d])[:, gsl(g)]
    for b, d in dirs:
        wq_ref[d, (2 * b + 1) * C:(2 * b + 2) * C, :] = (q[b] * eg[b, d]).astype(BF16)
        gl_ref[d, b] = jnp.exp(glrow[b, d])
    tinv = {key: eyecat + x for key, x in nm.items()}
    p = dict(nm)
    pbd = {key: _block_diag(x, G) for key, x in p.items()}
    for _ in range(int(math.log2(C)) - 1):
        p = {key: _bdot(x, pbd[key]) for key, x in p.items()}
        pbd = {key: _block_diag(x, G) for key, x in p.items()}
        tinv = {key: t + _bdot(t, pbd[key]) for key, t in tinv.items()}
    for b, d, g in groups:
        t = tinv[b, d, g]
        u_ref[d, rows(b), gsl(g)] = _bdot(t, _block_diag(vbg[b, d, g], G))
        wq_ref[d, 2 * b * C:(2 * b + 1) * C, gsl(g)] = _bdot(t, _block_diag(kbeg[b, d, g], G)).astype(BF16)


def _dn_local(q, k, v, gb):
    n, d = q.shape
    C = DN_CHUNK
    nc = n // C
    nb = 2 if nc % 2 == 0 else 1
    tok = pl.BlockSpec((nb * C, d), lambda j: (j, 0))
    return pl.pallas_call(
        _dn_local_kernel, grid=(nc // nb,),
        in_specs=[tok, tok, tok, pl.BlockSpec((nb * C, gb.shape[1]), lambda j: (j, 0))],
        out_specs=[pl.BlockSpec((2, nb * C, d), lambda j: (0, j, 0)),
                   pl.BlockSpec((2, nb * 2 * C, d), lambda j: (0, j, 0)),
                   pl.BlockSpec((2, nb, C + DN_DK, DN_HEADS * C), lambda j: (0, j, 0, 0)),
                   pl.BlockSpec((2, nb, 1, d), lambda j: (0, j, 0, 0))],
        out_shape=[jax.ShapeDtypeStruct((2, n, d), F32),
                   jax.ShapeDtypeStruct((2, 2 * n, d), BF16),
                   jax.ShapeDtypeStruct((2, nc, C + DN_DK, DN_HEADS * C), BF16),
                   jax.ShapeDtypeStruct((2, nc, 1, d), F32)],
        compiler_params=_cparams(("parallel",)))(q, k, v, gb)


def _dn_scan_kernel(*refs):
    ins, (of_ref, or_ref, s_ref) = refs[:8], refs[8:]
    C = DN_CHUNK

    @pl.when(pl.program_id(1) == 0)
    def _():
        s_ref[...] = jnp.zeros_like(s_ref)

    chains = [(d, p) for d in range(2) for p in range(DN_HEADS // 2)]
    outs = (of_ref, or_ref)
    lanes = lambda p: slice(2 * p * LANES, (2 * p + 2) * LANES)
    s2 = [jnp.concatenate([s_ref[d, 2 * p], s_ref[d, 2 * p + 1]], axis=-1) for d, p in chains]
    r = [_bdot(ins[4 * d + 1][0, :, lanes(p)], _block_diag(s, 2)) for (d, p), s in zip(chains, s2)]
    vn = [ins[4 * d][0, :, lanes(p)] - x[0:C] for (d, p), x in zip(chains, r)]
    r2 = [_bdot(ins[4 * d + 2][0, 0, :, 2 * p * C:(2 * p + 2) * C], _block_diag(x, 2))
          for (d, p), x in zip(chains, vn)]
    for (d, p), s, x, y in zip(chains, s2, r, r2):
        outs[d][:, lanes(p)] = x[C:2 * C] + y[0:C]
        snew = s * ins[4 * d + 3][0, 0, :, lanes(p)] + y[C:C + DN_DK]
        s_ref[d, 2 * p] = snew[:, 0:LANES]
        s_ref[d, 2 * p + 1] = snew[:, LANES:2 * LANES]


def _dn_scan(B, T, ctx_len, u, wq, akd, gl):
    _, n, d = u.shape
    C = DN_CHUNK
    nch, ncc = T // C, ctx_len // C
    rpos = lambda c: jnp.where(c < ncc, ncc - 1 - c, nch - 1 - (c - ncc))
    pos = (lambda b, c: b * nch + c, lambda b, c: b * nch + rpos(c))
    in_specs, args = [], []
    for dd in range(2):
        p = pos[dd]
        in_specs += [pl.BlockSpec((1, C, d), lambda b, c, p=p, dd=dd: (dd, p(b, c), 0)),
                     pl.BlockSpec((1, 2 * C, d), lambda b, c, p=p, dd=dd: (dd, p(b, c), 0)),
                     pl.BlockSpec((1, 1, C + DN_DK, DN_HEADS * C), lambda b, c, p=p, dd=dd: (dd, p(b, c), 0, 0)),
                     pl.BlockSpec((1, 1, 1, d), lambda b, c, p=p, dd=dd: (dd, p(b, c), 0, 0))]
        args += [u, wq, akd, gl]
    return pl.pallas_call(
        _dn_scan_kernel, grid=(B, nch), in_specs=in_specs,
        out_specs=[pl.BlockSpec((C, d), lambda b, c: (pos[0](b, c), 0)),
                   pl.BlockSpec((C, d), lambda b, c: (pos[1](b, c), 0))],
        out_shape=[jax.ShapeDtypeStruct((n, d), F32)] * 2,
        scratch_shapes=[pltpu.VMEM((2, DN_HEADS, DN_DK, LANES), F32)],
        compiler_params=_cparams(("arbitrary", "arbitrary")))(*args)


def _final_kernel(x_ref, y_ref, mod_ref, o_ref):
    o_ref[0] = x_ref[...] + mod_ref[0, 5:6, :] * y_ref[...]


def _final(B, T, ctx_len, x, y, mod):
    n, d = x.shape
    tm = TOKEN_BLOCK
    nblk, ncb = T // tm, ctx_len // tm
    tok = pl.BlockSpec((tm, d), lambda b, i: (b * nblk + ncb + i, 0))
    return pl.pallas_call(
        _final_kernel, grid=(B, nblk - ncb),
        in_specs=[tok, tok, pl.BlockSpec((1, 6, d), lambda b, i: (b, 0, 0))],
        out_specs=pl.BlockSpec((1, tm, d), lambda b, i: (b, i, 0)),
        out_shape=jax.ShapeDtypeStruct((B, T - ctx_len, d), F32),
        compiler_params=_cparams(("parallel", "parallel")))(x, y, mod)


def _rope_tables(seq, ctx_len):
    rows = seq // GRID_W
    r = jnp.broadcast_to(jnp.arange(rows, dtype=F32)[:, None], (rows, GRID_W)).reshape(-1)
    cl = jnp.broadcast_to(jnp.arange(GRID_W, dtype=F32)[None, :], (rows, GRID_W)).reshape(-1)

    def angles(rot_dim):
        nf = rot_dim // 4
        inv = ROPE_BASE ** (-jnp.arange(nf, dtype=F32) / nf)
        ang = jnp.concatenate([r[:, None] * inv, cl[:, None] * inv], axis=-1)
        ang = jnp.concatenate([jnp.zeros((ctx_len, rot_dim // 2), F32), ang], axis=0)
        return jnp.cos(ang), jnp.sin(ang)

    T = seq + ctx_len
    ca, sa = angles(DIFF_DIM)
    z = jnp.zeros_like(sa)
    ta = jnp.stack([jnp.concatenate([ca, ca] * 2, -1), jnp.concatenate([-sa, z] * 2, -1),
                    jnp.concatenate([z, sa] * 2, -1)])
    cb, sb = angles(MLA_ROPE)
    one, zn, zt = jnp.ones((T, MLA_NOPE), F32), jnp.zeros((T, MLA_NOPE), F32), jnp.zeros((T, LANES - MLA_QK), F32)
    zb = jnp.zeros_like(sb)
    tb = jnp.stack([jnp.concatenate([one, cb, cb, 1.0 + zt], -1), jnp.concatenate([zn, -sb, zb, zt], -1),
                    jnp.concatenate([zn, zb, sb, zt], -1)])
    return ta, tb


def _pad_heads(w, heads, width):
    lead = w.shape[:-1]
    w = w.reshape(lead + (heads, width))
    return jnp.pad(w, [(0, 0)] * len(lead) + [(0, 0), (0, LANES - width)]).reshape(lead + (heads * LANES,))


def _block_diag_ones(n, group):
    idx = np.arange(n) // group
    return jnp.asarray(idx[:, None] == idx[None, :], dtype=BF16)


def kernel(x, c, ctx, c_ctx, ada_w, ada_b, ev_w_in, ev_w_out, diff_q_norm, diff_k_norm, diff_lam_q1, diff_lam_k1, diff_lam_q2, diff_lam_k2, diff_subln, mla_cq_norm, mla_ckv_norm, mla_w_uq, mla_w_ukv, mla_q_norm, mla_k_norm, dn_w_in, dn_conv, dn_a_log, dn_dt_bias, dn_o_norm, dn_w_out, peer_wq, peer_k1, peer_k2, peer_u, peer_v):
    B, S, D = x.shape
    CTX = ctx.shape[1]
    T = CTX + S
    depth = ada_w.shape[0]
    tm = TOKEN_BLOCK
    assert CTX % tm == 0 and S % tm == 0 and (B * T) % PEER_TOKEN_BLOCK == 0 and B + 1 <= 8
    nblk, ncb = T // tm, CTX // tm
    geo = (B, nblk, tm, lambda b, i: jnp.where(i < ncb, B, b))

    cc = jnp.zeros((8, D), F32).at[:B].set(c).at[B].set(c_ctx)
    mods = _adaln(cc, ada_w, ada_b).reshape(depth, 8, 6, D)
    xs = jnp.concatenate([ctx, x], axis=1).reshape(B * T, D)
    ropea, ropeb = _rope_tables(S, CTX)
    bd64, bd128 = _block_diag_ones(HEAD_W, DIFF_DIM), _block_diag_ones(HEAD_W, LANES)

    y = None
    for l in range(depth):
        i = l // 2
        mod = mods[l]
        gmod = mods[l - 1] if l else None
        if l % 2 == 0:
            lam_init = 0.8 - 0.6 * math.exp(-0.3 * l)
            w_in = ev_w_in[i]
            o = 3 * HEAD_W + MLA_Q_RANK + MLA_KV_RANK
            kr_rep = jnp.pad(jnp.broadcast_to(w_in[:, None, o:o + MLA_ROPE], (D, MLA_HEADS, MLA_ROPE)),
                             ((0, 0), (0, 0), (MLA_NOPE, LANES - MLA_QK))).reshape(D, HEAD_W)
            w_cat = jnp.concatenate([w_in[:, :o], kr_rep], axis=1).astype(BF16)
            res = _modmm(geo, xs, mod, w_cat, y, gmod)
            (xs, z) = res if y is not None else (xs, res)
            ukv = mla_w_ukv[i].reshape(MLA_KV_RANK, MLA_HEADS, MLA_NOPE + MLA_VDIM)
            gains = jnp.stack([jnp.tile(diff_q_norm[i], HEAD_W // DIFF_DIM), jnp.tile(diff_k_norm[i], HEAD_W // DIFF_DIM),
                               jnp.tile(jnp.pad(mla_q_norm[i], (0, LANES - MLA_QK)), MLA_HEADS),
                               jnp.tile(jnp.pad(mla_k_norm[i], (0, LANES - MLA_QK)), MLA_HEADS)])
            qa, ka, va, qb, kb, vb = _even_prep(
                geo, z, ropea, ropeb, gains, mla_cq_norm[i][None, :], mla_ckv_norm[i][None, :],
                _pad_heads(mla_w_uq[i], MLA_HEADS, MLA_QK).astype(BF16),
                _pad_heads(ukv[:, :, :MLA_NOPE].reshape(MLA_KV_RANK, -1), MLA_HEADS, MLA_NOPE).astype(BF16),
                ukv[:, :, MLA_NOPE:].reshape(MLA_KV_RANK, -1).astype(BF16), bd64, bd128)
            lam_vecs = jnp.stack([diff_lam_q1[i], diff_lam_k1[i], diff_lam_q2[i], diff_lam_k2[i]])
            oa = _attention(B, T, CTX, qa, ka, va, lam_vecs, diff_subln[i][None, :], lam_init)
            ob = _attention(B, T, CTX, qb, kb, vb)
            xs, hqt = _mixer_out(geo, _even_out_kernel, [(oa, HEAD_W, 0), (ob, HEAD_W, 0)], [], xs, mod,
                                 ev_w_out[i].astype(BF16))
        else:
            nin = dn_w_in.shape[2]
            w_in = jnp.pad(dn_w_in[i], ((0, 0), (0, -nin % LANES))).astype(BF16)
            res = _modmm(geo, xs, mod, w_in, y, gmod)
            (xs, z) = res if y is not None else (xs, res)
            q, k, v, gb = _dn_prep(geo, ncb, z, dn_conv[i], dn_a_log[i].reshape(1, -1),
                                   dn_dt_bias[i].reshape(1, -1))
            of, orv = _dn_scan(B, T, CTX, *_dn_local(q, k, v, gb))
            xs, hqt = _mixer_out(geo, _dn_out_kernel,
                                 [(of, D, 0), (orv, D, 0), (z, D, DN_QKV // D)], [dn_o_norm[i][None, :]], xs, mod,
                                 dn_w_out[i].astype(BF16))
        y = _peer(hqt, peer_wq[l], peer_k1[l], peer_k2[l], peer_u[l], peer_v[l])
    return _final(B, T, CTX, xs, y, mods[depth - 1])
```

```python
import functools
import math

import jax
import jax.numpy as jnp
import numpy as np
from jax import lax
from jax.experimental import pallas as pl
from jax.experimental.pallas import tpu as pltpu

F32 = jnp.float32
BF16 = jnp.bfloat16
HI = lax.Precision.HIGHEST
NT = (((1,), (1,)), ((), ()))

EPS = 1e-6
LOG2E = math.log2(math.e)
ROPE_BASE = 10000.0
GRID_W = 64
LANES = 128
TOKEN_BLOCK = 256
VMEM_LIMIT = 56 * 1024 * 1024

DIFF_HEADS, DIFF_DIM = 4, 64
MLA_HEADS, MLA_Q_RANK, MLA_KV_RANK, MLA_NOPE, MLA_ROPE, MLA_VDIM = 4, 256, 128, 64, 32, 128
MLA_QK = MLA_NOPE + MLA_ROPE
HEAD_W = DIFF_HEADS * LANES
DN_HEADS, DN_DK, DN_CONV, DN_CHUNK = 8, 128, 5, 64
DN_QKV = 3 * DN_HEADS * DN_DK
PEER_HEADS, PEER_NKEYS, PEER_TOPK = 8, 128, 16
PEER_CAND_ROWS = 80
PEER_TOKEN_BLOCK = 512
PEER_EXPERT_BLOCK = 2048


def _cparams(sem):
    return pltpu.CompilerParams(dimension_semantics=sem, vmem_limit_bytes=VMEM_LIMIT)


def _rms_rows(x):
    return x * lax.rsqrt(jnp.mean(x * x, axis=-1, keepdims=True) + EPS)


def _silu(x):
    return x * jax.nn.sigmoid(x)


def _group_sum(sq, bd):
    hi = sq.astype(BF16)
    lo = (sq - hi.astype(F32)).astype(BF16)
    return jnp.dot(hi, bd, preferred_element_type=F32) + jnp.dot(lo, bd, preferred_element_type=F32)


def _adaln_kernel(c_ref, w_ref, b_ref, o_ref):
    s = _silu(c_ref[...])
    o_ref[0] = jnp.dot(s, w_ref[0], precision=HI, preferred_element_type=F32) + b_ref[0]


def _adaln(cc, ada_w, ada_b):
    depth, d, n = ada_w.shape
    tn = 1536
    return pl.pallas_call(
        _adaln_kernel,
        grid=(depth, n // tn),
        in_specs=[pl.BlockSpec((8, d), lambda l, j: (0, 0)),
                  pl.BlockSpec((1, d, tn), lambda l, j: (l, 0, j)),
                  pl.BlockSpec((1, 1, tn), lambda l, j: (l, 0, j))],
        out_specs=pl.BlockSpec((1, 8, tn), lambda l, j: (l, 0, j)),
        out_shape=jax.ShapeDtypeStruct((depth, 8, n), F32),
        compiler_params=_cparams(("arbitrary", "arbitrary")),
    )(cc, ada_w, ada_b.reshape(depth, 1, n))


def _modmm_kernel(*refs, has_y):
    if has_y:
        x_ref, y_ref, gmod_ref, mod_ref, w_ref, xo_ref, z_ref = refs
        x = x_ref[...] + gmod_ref[0, 5:6, :] * y_ref[...]
        xo_ref[...] = x
    else:
        x_ref, mod_ref, w_ref, z_ref = refs
        x = x_ref[...]
    h = _rms_rows(x) * (1.0 + mod_ref[0, 1:2, :]) + mod_ref[0, 0:1, :]
    z_ref[...] = jnp.dot(h.astype(BF16), w_ref[...], preferred_element_type=F32)


def _modmm(geo, x, mod, w, y=None, gmod=None):
    B, nblk, tm, mrow = geo
    n, d = x.shape
    nout = w.shape[1]
    tok = pl.BlockSpec((tm, d), lambda b, i: (b * nblk + i, 0))
    modspec = pl.BlockSpec((1, 6, d), lambda b, i: (mrow(b, i), 0, 0))
    wspec = pl.BlockSpec((d, nout), lambda b, i: (0, 0))
    zspec = pl.BlockSpec((tm, nout), lambda b, i: (b * nblk + i, 0))
    zshape = jax.ShapeDtypeStruct((n, nout), F32)
    if y is None:
        return pl.pallas_call(
            functools.partial(_modmm_kernel, has_y=False), grid=(B, nblk),
            in_specs=[tok, modspec, wspec], out_specs=zspec, out_shape=zshape,
            compiler_params=_cparams(("parallel", "parallel")))(x, mod, w)
    return pl.pallas_call(
        functools.partial(_modmm_kernel, has_y=True), grid=(B, nblk),
        in_specs=[tok, tok, modspec, modspec, wspec], out_specs=[tok, zspec],
        out_shape=[jax.ShapeDtypeStruct((n, d), F32), zshape],
        compiler_params=_cparams(("parallel", "parallel")))(x, y, gmod, mod, w)


def _rope_lanes(v, tab_ref, half):
    rep = v.shape[-1] // LANES
    c = jnp.concatenate([tab_ref[0]] * rep, axis=-1)
    sm = jnp.concatenate([tab_ref[1]] * rep, axis=-1)
    sp = jnp.concatenate([tab_ref[2]] * rep, axis=-1)
    n = v.shape[-1]
    return v * c + pltpu.roll(v, n - half, 1) * sm + pltpu.roll(v, half, 1) * sp


def _even_prep_kernel(z_ref, ropea_ref, ropeb_ref, gains_ref, cqn_ref, ckvn_ref, wuq_ref, wuk_ref, wuv_ref,
                      bd64_ref, bd128_ref, qa_ref, ka_ref, va_ref, qb_ref, kb_ref, vb_ref):
    W = HEAD_W
    bd64 = bd64_ref[...]
    bd128 = bd128_ref[...]

    def norm_groups(v, bd, width, gain):
        ms = _group_sum(v * v, bd) * (1.0 / width)
        return v * lax.rsqrt(ms + EPS) * gain

    qa = norm_groups(z_ref[:, 0:W], bd64, DIFF_DIM, gains_ref[0:1, :])
    ka = norm_groups(z_ref[:, W:2 * W], bd64, DIFF_DIM, gains_ref[1:2, :])
    qa_ref[...] = (_rope_lanes(qa, ropea_ref, DIFF_DIM // 2) * (DIFF_DIM ** -0.5 * LOG2E)).astype(BF16)
    ka_ref[...] = _rope_lanes(ka, ropea_ref, DIFF_DIM // 2).astype(BF16)
    va_ref[...] = z_ref[:, 2 * W:3 * W].astype(BF16)

    o = 3 * W
    cq = (_rms_rows(z_ref[:, o:o + MLA_Q_RANK]) * cqn_ref[...]).astype(BF16)
    o += MLA_Q_RANK
    ckv = (_rms_rows(z_ref[:, o:o + MLA_KV_RANK]) * ckvn_ref[...]).astype(BF16)
    o += MLA_KV_RANK
    kr = z_ref[:, o:o + W]
    qb = jnp.dot(cq, wuq_ref[...], preferred_element_type=F32)
    kb = jnp.dot(ckv, wuk_ref[...], preferred_element_type=F32) + kr
    qb = norm_groups(qb, bd128, MLA_QK, gains_ref[2:3, :])
    kb = norm_groups(kb, bd128, MLA_QK, gains_ref[3:4, :])
    qb_ref[...] = (_rope_lanes(qb, ropeb_ref, MLA_ROPE // 2) * (MLA_QK ** -0.5 * LOG2E)).astype(BF16)
    kb_ref[...] = _rope_lanes(kb, ropeb_ref, MLA_ROPE // 2).astype(BF16)
    vb_ref[...] = jnp.dot(ckv, wuv_ref[...], preferred_element_type=F32).astype(BF16)


def _even_prep(geo, z, ropea, ropeb, gains, cqn, ckvn, wuq, wuk, wuv, bd64, bd128):
    B, nblk, tm, _ = geo
    n = z.shape[0]
    W = HEAD_W
    full = lambda a: pl.BlockSpec(a.shape, lambda b, i: (0,) * a.ndim)
    rope = pl.BlockSpec((3, tm, LANES), lambda b, i: (0, i, 0))
    out = pl.BlockSpec((tm, W), lambda b, i: (b * nblk + i, 0))
    return pl.pallas_call(
        _even_prep_kernel, grid=(B, nblk),
        in_specs=[pl.BlockSpec((tm, z.shape[1]), lambda b, i: (b * nblk + i, 0)), rope, rope, full(gains),
                  full(cqn), full(ckvn), full(wuq), full(wuk), full(wuv), full(bd64), full(bd128)],
        out_specs=[out] * 6, out_shape=[jax.ShapeDtypeStruct((n, W), BF16)] * 6,
        compiler_params=_cparams(("parallel", "parallel")))(z, ropea, ropeb, gains, cqn, ckvn, wuq, wuk, wuv,
                                                            bd64, bd128)


def _attn_kernel(*refs, diff, tk, row_split, n_ctx_q, ctx_len, n_keys, lam_init):
    if diff:
        q_ref, k_ref, v_ref, lam_ref, subln_ref, o_ref, s_ref = refs
    else:
        q_ref, k_ref, v_ref, o_ref, s_ref = refs
    qi = pl.program_id(2)
    tr = q_ref.shape[0] // row_split
    qs = []
    for r in range(row_split):
        q = q_ref[r * tr:(r + 1) * tr, :]
        if diff:
            lane = lax.broadcasted_iota(jnp.int32, q.shape, 1)
            zero = jnp.zeros_like(q)
            qs += [jnp.where(lane < DIFF_DIM, q, zero), jnp.where(lane >= DIFF_DIM, q, zero)]
        else:
            qs.append(q)
    nsub = len(qs) // row_split

    def scores(slot, start, size):
        k = k_ref[pl.ds(start, size), :]
        s = [lax.dot_general(qq, k, NT, preferred_element_type=F32) for qq in qs]
        for j, x in enumerate(s):
            s_ref[slot, j, :, 0:size] = x
        return tuple(jnp.max(x, axis=-1, keepdims=True) for x in s)

    def update(slot, start, size, mx, carry):
        v = v_ref[pl.ds(start, size), :]
        mn = [jnp.maximum(m, x) for (m, _, _), x in zip(carry, mx)]
        p = [jnp.exp2(s_ref[slot, j, :, 0:size] - y) for j, y in enumerate(mn)]
        alpha = [jnp.exp2(m - y) for (m, _, _), y in zip(carry, mn)]
        pv = [jnp.dot(x.astype(BF16), v, preferred_element_type=F32) for x in p]
        return tuple((y, a * l + jnp.sum(x, axis=-1, keepdims=True), a * acc + z)
                     for (_, l, acc), y, a, x, z in zip(carry, mn, alpha, p, pv))

    init = tuple((jnp.full((tr, 1), -1e30, F32), jnp.zeros((tr, 1), F32), jnp.zeros((tr, LANES), F32))
                 for _ in qs)

    def finish(carry):
        outs = [acc / l for (_, l, acc) in carry]
        for r in range(row_split):
            if diff:
                lam = (jnp.exp(jnp.sum(lam_ref[0:1, :] * lam_ref[1:2, :], axis=-1, keepdims=True))
                       - jnp.exp(jnp.sum(lam_ref[2:3, :] * lam_ref[3:4, :], axis=-1, keepdims=True)) + lam_init)
                o = _rms_rows(outs[r * nsub] - lam * outs[r * nsub + 1]) * subln_ref[...] * (1.0 - lam_init)
            else:
                o = outs[r]
            o_ref[r * tr:(r + 1) * tr, :] = o.astype(o_ref.dtype)

    @pl.when(qi < n_ctx_q)
    def _():
        finish(update(0, 0, ctx_len, scores(0, 0, ctx_len), init))

    @pl.when(qi >= n_ctx_q)
    def _():
        n = n_keys // tk
        at = lambda c: pl.multiple_of(c * tk, tk)

        def pair(j, state):
            carry, mx0 = state
            mx1 = scores(1, at(2 * j + 1), tk)
            carry = update(0, at(2 * j), tk, mx0, carry)
            mx0 = scores(0, at(2 * j + 2), tk)
            carry = update(1, at(2 * j + 1), tk, mx1, carry)
            return carry, mx0

        carry, mx0 = lax.fori_loop(0, (n - 1) // 2, pair, (init, scores(0, 0, tk)))
        if n % 2 == 0:
            mx1 = scores(1, (n - 1) * tk, tk)
            carry = update(0, (n - 2) * tk, tk, mx0, carry)
            carry = update(1, (n - 1) * tk, tk, mx1, carry)
        else:
            carry = update(0, (n - 1) * tk, tk, mx0, carry)
        finish(carry)


def _attention(B, T, ctx_len, q, k, v, lam_vecs=None, subln=None, lam_init=0.0):
    diff = lam_vecs is not None
    tq = TOKEN_BLOCK
    tk = next(c for c in (1408, 768, TOKEN_BLOCK) if T % c == 0)
    nq = T // tq
    heads = q.shape[1] // LANES
    qspec = pl.BlockSpec((tq, LANES), lambda b, h, i: (b * nq + i, h))
    kvspec = pl.BlockSpec((T, LANES), lambda b, h, i: (b, h))
    in_specs = [qspec, kvspec, kvspec]
    args = [q, k, v]
    if diff:
        in_specs += [pl.BlockSpec(lam_vecs.shape, lambda b, h, i: (0, 0)),
                     pl.BlockSpec(subln.shape, lambda b, h, i: (0, 0))]
        args += [lam_vecs, subln]
    row_split = 1 if diff else 2
    chains = row_split * (2 if diff else 1)
    return pl.pallas_call(
        functools.partial(_attn_kernel, diff=diff, tk=tk, row_split=row_split, n_ctx_q=ctx_len // tq,
                          ctx_len=ctx_len, n_keys=T, lam_init=lam_init),
        grid=(B, heads, nq), in_specs=in_specs, out_specs=qspec,
        out_shape=jax.ShapeDtypeStruct(q.shape, BF16),
        scratch_shapes=[pltpu.VMEM((2, chains, tq // row_split, max(tk, ctx_len)), F32)],
        compiler_params=_cparams(("parallel", "parallel", "arbitrary")))(*args)


def _residual_tail(x, y, mod_ref, xo_ref, hqt_ref):
    xn = x + mod_ref[0, 2:3, :] * y
    xo_ref[...] = xn
    hq = _rms_rows(xn) * (1.0 + mod_ref[0, 4:5, :]) + mod_ref[0, 3:4, :]
    hqt_ref[...] = hq.T.astype(BF16)


def _even_out_kernel(oa_ref, ob_ref, x_ref, mod_ref, wo_ref, xo_ref, hqt_ref):
    W = HEAD_W
    y = (jnp.dot(oa_ref[...], wo_ref[0:W, :], preferred_element_type=F32)
         + jnp.dot(ob_ref[...], wo_ref[W:2 * W, :], preferred_element_type=F32))
    _residual_tail(x_ref[...], y, mod_ref, xo_ref, hqt_ref)


def _dn_out_kernel(of_ref, or_ref, zg_ref, onorm_ref, x_ref, mod_ref, wo_ref, xo_ref, hqt_ref):
    parts = []
    for h in range(DN_HEADS):
        hs = slice(h * LANES, (h + 1) * LANES)
        o = of_ref[:, hs] + or_ref[:, hs]
        parts.append((_rms_rows(o) * onorm_ref[...] * _silu(zg_ref[:, hs])).astype(BF16))
    y = jnp.dot(jnp.concatenate(parts, axis=-1), wo_ref[...], preferred_element_type=F32)
    _residual_tail(x_ref[...], y, mod_ref, xo_ref, hqt_ref)


def _mixer_out(geo, kernel_fn, token_args, small_args, x, mod, wo):
    B, nblk, tm, mrow = geo
    n, d = x.shape
    tokspec = lambda a, col: pl.BlockSpec((tm, a[1]), lambda b, i: (b * nblk + i, col))
    in_specs = [tokspec((a, w), col) for (a, w, col) in token_args]
    in_specs += [pl.BlockSpec(a.shape, lambda b, i: (0,) * a.ndim) for a in small_args]
    in_specs += [pl.BlockSpec((tm, d), lambda b, i: (b * nblk + i, 0)),
                 pl.BlockSpec((1, 6, d), lambda b, i: (mrow(b, i), 0, 0)),
                 pl.BlockSpec(wo.shape, lambda b, i: (0, 0))]
    return pl.pallas_call(
        kernel_fn, grid=(B, nblk), in_specs=in_specs,
        out_specs=[pl.BlockSpec((tm, d), lambda b, i: (b * nblk + i, 0)),
                   pl.BlockSpec((d, tm), lambda b, i: (0, b * nblk + i))],
        out_shape=[jax.ShapeDtypeStruct((n, d), F32), jax.ShapeDtypeStruct((d, n), BF16)],
        compiler_params=_cparams(("parallel", "parallel")))(
            *[a for (a, _, _) in token_args], *small_args, x, mod, wo)


def _peer_select_kernel(hqt_ref, wqt_ref, k1_ref, k2_ref, th_ref, e1_ref, s2_ref, e2_ref, work_ref, top_ref, cand_ref):
    K, H = PEER_TOPK, PEER_HEADS
    qt = jnp.dot(wqt_ref[...], hqt_ref[...], preferred_element_type=F32)
    dk = k1_ref.shape[1]
    for h in range(H):
        q1 = qt[(2 * h) * dk:(2 * h + 1) * dk, :].astype(BF16)
        q2 = qt[(2 * h + 1) * dk:(2 * h + 2) * dk, :].astype(BF16)
        s1 = jnp.dot(k1_ref[...], q1, preferred_element_type=F32)
        s2 = jnp.dot(k2_ref[...], q2, preferred_element_type=F32)
        th_ref[h] = s1
        s2_ref[h] = s2
        work_ref[2 * h] = s1
        work_ref[2 * h + 1] = s2

    def extract(ref, count):
        s = [ref[a] for a in range(count)]
        m = [jnp.max(x, axis=0, keepdims=True) for x in s]
        for a in range(count):
            ref[a] = jnp.where(s[a] == m[a], -jnp.inf, s[a])
        return m

    def top_round(r, carry):
        for a, m in enumerate(extract(work_ref, 2 * H)):
            top_ref[a, pl.ds(r, 1), :] = m
        return carry

    lax.fori_loop(0, K, top_round, 0)

    def candidates(h):
        v1, v2 = top_ref[2 * h], top_ref[2 * h + 1]
        row8 = lax.broadcasted_iota(jnp.int32, (8, v1.shape[1]), 0)
        pieces = [v1[0:1, :] + v2, v1[1:2, :] + v2[0:8, :]]
        pieces += [jnp.where(row8 < K // (r1 + 1), v1[r1:r1 + 1, :] + v2[0:8, :], -jnp.inf) for r1 in range(2, 8)]
        pieces.append(v1[8:16, :] + v2[0:1, :])
        return jnp.concatenate(pieces, axis=0)

    for h in range(H):
        cand_ref[h] = candidates(h)
    tau = lax.fori_loop(0, K, lambda r, carry: tuple(extract(cand_ref, H)),
                        tuple(jnp.zeros((1, hqt_ref.shape[1]), F32) for _ in range(H)))
    for h in range(H):
        cand = candidates(h)
        m1, m2 = top_ref[2 * h, 0:1, :], top_ref[2 * h + 1, 0:1, :]
        zsum = jnp.sum(jnp.where(cand >= tau[h], jnp.exp(cand - (m1 + m2)), 0.0), axis=0, keepdims=True)
        s1 = th_ref[h]
        e1_ref[h] = jnp.exp(s1 - m1) / zsum
        e2_ref[h] = jnp.exp(s2_ref[h] - m2)
        v2 = top_ref[2 * h + 1]
        theta = jnp.full(s1.shape, jnp.inf, F32)
        for r in range(K):
            theta = jnp.where((s1 + v2[r:r + 1, :]) >= tau[h], v2[r:r + 1, :], theta)
        th_ref[h] = theta


def _peer_select(hqt, wqt, k1, k2):
    d, n = hqt.shape
    tm = PEER_TOKEN_BLOCK
    H, NK = PEER_HEADS, PEER_NKEYS
    big = pl.BlockSpec((H, NK, tm), lambda j: (0, 0, j))
    bigshape = jax.ShapeDtypeStruct((H, NK, n), F32)
    full = lambda a: pl.BlockSpec(a.shape, lambda j: (0,) * a.ndim)
    return pl.pallas_call(
        _peer_select_kernel, grid=(n // tm,),
        in_specs=[pl.BlockSpec((d, tm), lambda j: (0, j)), full(wqt), full(k1), full(k2)],
        out_specs=[big] * 4, out_shape=[bigshape] * 4,
        scratch_shapes=[pltpu.VMEM((2 * H, NK, tm), F32), pltpu.VMEM((2 * H, PEER_TOPK, tm), F32),
                        pltpu.VMEM((H, PEER_CAND_ROWS, tm), F32)],
        compiler_params=_cparams(("parallel",)))(hqt, wqt, k1, k2)


def _gelu(a):
    return 0.5 * a * (1.0 + lax.erf(a * np.float32(math.sqrt(0.5))))


PEER_PIECE = 2 * LANES


def _peer_dense_kernel(hqt_ref, u_ref, vt_ref, th_ref, e1_ref, s2_ref, e2_ref, zero_ref, y_ref, row_ref,
                       *piece_refs):
    c = pl.program_id(1)
    NK = PEER_NKEYS
    n_i = PEER_EXPERT_BLOCK // NK
    pieces = [piece_refs[3 * p:3 * p + 3] for p in range(len(piece_refs) // 3)]

    @pl.when(c == 0)
    def _():
        for acc_ref, _, _ in pieces:
            acc_ref[...] = jnp.zeros_like(acc_ref)

    H = PEER_HEADS
    strips = [slice(ts * LANES, (ts + 1) * LANES) for ts in range(hqt_ref.shape[1] // LANES)]
    for ii in range(n_i):
        for h in range(H):
            i = c * n_i + ii
            row_ref[0, ii, h:h + 1, :] = th_ref[h, pl.ds(i, 1), :]
            row_ref[1, ii, h:h + 1, :] = e1_ref[h, pl.ds(i, 1), :]

    JB, RUNS, PIECE = 16, 8, PEER_PIECE
    zero = zero_ref[0:JB, :]
    for p, (_, act_ref, _) in enumerate(pieces):
        pc = slice(p * PIECE, (p + 1) * PIECE)
        act_ref[...] = _gelu(jnp.dot(u_ref[...], hqt_ref[:, pc], preferred_element_type=F32))
    for p, (acc_ref, act_ref, wt_ref) in enumerate(pieces):
        for ts in range(PIECE // LANES):
            tl = strips[p * PIECE // LANES + ts]
            pl_ = strips[ts]
            for i0 in range(0, n_i, RUNS):
                for jb in range(NK // JB):
                    js = slice(jb * JB, (jb + 1) * JB)
                    g = [jnp.zeros((JB, LANES), F32) for _ in range(RUNS)]
                    for h in range(H):
                        s2 = s2_ref[h, js, tl] + zero
                        e2 = e2_ref[h, js, tl] + zero
                        for r in range(RUNS):
                            sel = s2 >= row_ref[0, i0 + r, h:h + 1, tl]
                            g[r] = g[r] + jnp.where(sel, row_ref[1, i0 + r, h:h + 1, tl] * e2, 0.0)
                    for r in range(RUNS):
                        rows = slice((i0 + r) * NK + jb * JB, (i0 + r) * NK + (jb + 1) * JB)
                        wt_ref[rows, pl_] = (g[r] * act_ref[rows, pl_]).astype(BF16)
        acc_ref[...] += jnp.dot(vt_ref[...], wt_ref[...], preferred_element_type=F32)

    @pl.when(c == pl.num_programs(1) - 1)
    def _():
        for p, (acc_ref, _, _) in enumerate(pieces):
            y_ref[p * PIECE:(p + 1) * PIECE, :] = acc_ref[...].T


def _peer_dense(hqt, u, vt, theta, e1, s2, e2):
    d, n = hqt.shape
    tm, te = PEER_TOKEN_BLOCK, PEER_EXPERT_BLOCK
    H, NK = PEER_HEADS, PEER_NKEYS
    big = pl.BlockSpec((H, NK, tm), lambda j, c: (0, 0, j))
    return pl.pallas_call(
        _peer_dense_kernel, grid=(n // tm, u.shape[0] // te),
        in_specs=[pl.BlockSpec((d, tm), lambda j, c: (0, j)),
                  pl.BlockSpec((te, d), lambda j, c: (c, 0)),
                  pl.BlockSpec((d, te), lambda j, c: (0, c)),
                  big, big, big, big, pl.BlockSpec((NK, LANES), lambda j, c: (0, 0))],
        out_specs=pl.BlockSpec((tm, d), lambda j, c: (j, 0)),
        out_shape=jax.ShapeDtypeStruct((n, d), F32),
        scratch_shapes=[pltpu.VMEM((2, te // NK, H, tm), F32)]
        + [pltpu.VMEM((d, PEER_PIECE), F32), pltpu.VMEM((te, PEER_PIECE), F32),
           pltpu.VMEM((te, PEER_PIECE), BF16)] * (tm // PEER_PIECE),
        compiler_params=_cparams(("parallel", "arbitrary")))(hqt, u, vt, theta, e1, s2, e2,
                                                             jnp.zeros((NK, LANES), F32))


def _peer(hqt, wq, k1, k2, u_tab, v_tab):
    sel = _peer_select(hqt, wq.T.astype(BF16), k1.astype(BF16), k2.astype(BF16))
    return _peer_dense(hqt, u_tab.astype(BF16), v_tab.T.astype(BF16), *sel)


def _dn_prep_kernel(z_ref, prev_ref, next_ref, conv_ref, alog_ref, dtb_ref, ab_ref, q_ref, k_ref, v_ref, gb_ref,
                    ext_ref, *, n_ctx_blk, nblk):
    i = pl.program_id(1)
    tm = z_ref.shape[0]
    halo = prev_ref.shape[0]
    pad = DN_CONV // 2
    has_prev = jnp.logical_and(i != 0, i != n_ctx_blk)
    has_next = jnp.logical_and(i != n_ctx_blk - 1, i != nblk - 1)
    ext_ref[0:halo, :] = jnp.where(has_prev, prev_ref[...], 0.0)
    ext_ref[halo:halo + tm, :] = z_ref[...]
    ext_ref[halo + tm:, :] = jnp.where(has_next, next_ref[...], 0.0)
    nq = DN_HEADS * DN_DK
    for j in range(DN_QKV // LANES):
        cs = slice(j * LANES, (j + 1) * LANES)
        acc = conv_ref[0:1, cs] * ext_ref[halo - pad:halo - pad + tm, cs]
        for t in range(1, DN_CONV):
            acc = acc + conv_ref[t:t + 1, cs] * ext_ref[halo - pad + t:halo - pad + t + tm, cs]
        y = _silu(acc)
        if j * LANES < 2 * nq:
            y = y * lax.rsqrt(jnp.sum(y * y, axis=-1, keepdims=True) + EPS)
        if j * LANES < nq:
            q_ref[:, cs] = y * DN_DK ** -0.5
        elif j * LANES < 2 * nq:
            k_ref[:, slice(j * LANES - nq, (j + 1) * LANES - nq)] = y
        else:
            v_ref[:, slice(j * LANES - 2 * nq, (j + 1) * LANES - 2 * nq)] = y
    ab = ab_ref[...]
    nh = 2 * DN_HEADS
    xa = ab[:, 0:nh] + dtb_ref[...]
    softplus = jnp.maximum(xa, 0.0) + jnp.log(1.0 + jnp.exp(-jnp.abs(xa)))
    gb_ref[:, 0:nh] = -jnp.exp(alog_ref[...]) * softplus
    gb_ref[:, nh:2 * nh] = jax.nn.sigmoid(ab[:, nh:2 * nh])


def _dn_prep(geo, n_ctx_blk, z, conv_w, alog, dtb):
    B, nblk, tm, _ = geo
    n = z.shape[0]
    halo = 8
    r = tm // halo
    nh8 = n // halo
    d = DN_HEADS * DN_DK
    prev = pl.BlockSpec((halo, DN_QKV), lambda b, i: (jnp.maximum((b * nblk + i) * r - 1, 0), 0))
    nxt = pl.BlockSpec((halo, DN_QKV), lambda b, i: (jnp.minimum((b * nblk + i + 1) * r, nh8 - 1), 0))
    ab = z[:, DN_QKV + d:DN_QKV + d + 4 * DN_HEADS]
    tok = pl.BlockSpec((tm, d), lambda b, i: (b * nblk + i, 0))
    gbspec = pl.BlockSpec((tm, 4 * DN_HEADS), lambda b, i: (b * nblk + i, 0))
    full = lambda a: pl.BlockSpec(a.shape, lambda b, i: (0,) * a.ndim)
    return pl.pallas_call(
        functools.partial(_dn_prep_kernel, n_ctx_blk=n_ctx_blk, nblk=nblk), grid=(B, nblk),
        in_specs=[pl.BlockSpec((tm, DN_QKV), lambda b, i: (b * nblk + i, 0)), prev, nxt, full(conv_w),
                  full(alog), full(dtb), gbspec],
        out_specs=[tok, tok, tok, gbspec],
        out_shape=[jax.ShapeDtypeStruct((n, d), F32)] * 3 + [jax.ShapeDtypeStruct((n, 4 * DN_HEADS), F32)],
        scratch_shapes=[pltpu.VMEM((tm + 2 * halo, DN_QKV), F32)],
        compiler_params=_cparams(("parallel", "parallel")))(z, z, z, conv_w, alog, dtb, ab)


DN_GROUP = 4


def _bdot(a, b, dims=None):
    a, b = a.astype(BF16), b.astype(BF16)
    if dims is None:
        return jnp.dot(a, b, preferred_element_type=F32)
    return lax.dot_general(a, b, dims, preferred_element_type=F32)


def _block_diag(x, nblk):
    r, n = x.shape
    w = n // nblk
    tall = jnp.concatenate([x] * nblk, axis=0)
    rb = lax.broadcasted_iota(jnp.int32, tall.shape, 0) // r
    lb = lax.broadcasted_iota(jnp.int32, tall.shape, 1) // w
    return jnp.where(rb == lb, tall, jnp.zeros_like(tall))


def _dn_local_kernel(q_ref, k_ref, v_ref, gb_ref, u_ref, wq_ref, akd_ref, gl_ref):
    C, G, nh = DN_CHUNK, DN_GROUP, DN_HEADS
    d_all = nh * LANES
    ri = lax.broadcasted_iota(jnp.int32, (C, G * C), 0)
    ci = lax.broadcasted_iota(jnp.int32, (C, G * C), 1) % C
    eyecat = (ri == ci).astype(F32)
    ones = jnp.ones((C, C), F32)
    r2 = lax.broadcasted_iota(jnp.int32, (C, C), 0)
    c2 = lax.broadcasted_iota(jnp.int32, (C, C), 1)
    eye128 = (lax.broadcasted_iota(jnp.int32, (LANES, LANES), 0)
              == lax.broadcasted_iota(jnp.int32, (LANES, LANES), 1)).astype(BF16)
    ncol = gb_ref.shape[1]
    sel_row = lax.broadcasted_iota(jnp.int32, (ncol, d_all), 0)
    sel_head = lax.broadcasted_iota(jnp.int32, (ncol, d_all), 1) // LANES
    cat_row = lax.broadcasted_iota(jnp.int32, (ncol, G * C), 0)
    cat_head = lax.broadcasted_iota(jnp.int32, (ncol, G * C), 1) // C
    hdot = lambda a, b: jnp.dot(a, b, precision=HI, preferred_element_type=F32)
    nblk = q_ref.shape[0] // C
    rows = lambda b: slice(b * C, (b + 1) * C)
    q, k, v, gb = ([ref[rows(b), :] for b in range(nblk)] for ref in (q_ref, k_ref, v_ref, gb_ref))
    dirs = [(b, d) for b in range(nblk) for d in (0, 1)]
    groups = [(b, d, g) for b, d in dirs for g in range(nh // G)]
    gsl = lambda g: slice(g * G * LANES, (g + 1) * G * LANES)
    csl = lambda g: slice(g * G * C, (g + 1) * G * C)
    incl = [(ri >= ci), (ri <= ci)]
    strict = [(ri > ci), (ri < ci)]
    last = [C - 1, 0]
    tri = [(r2 >= c2).astype(F32), (r2 <= c2).astype(F32)]
    gcs = {(b, d): hdot(tri[d], gb[b]) for b, d in dirs}
    spread = lambda m, first: hdot(m, (sel_row == first + sel_head).astype(F32))
    gc = {(b, d): spread(gcs[b, d], d * nh) for b, d in dirs}
    bet = {(b, d): spread(gb[b], 2 * nh + d * nh) for b, d in dirs}
    gcol = {(b, d, g): hdot(gcs[b, d], (cat_row == d * nh + g * G + cat_head).astype(F32)) for b, d, g in groups}
    grow = {key: hdot(ones, x * eyecat) for key, x in gcol.items()}
    kbm = {(b, d): k[b] * bet[b, d] for b, d in dirs}
    kq = {(b, d, g): _bdot(jnp.concatenate([kbm[b, d][:, gsl(g)], q[b][:, gsl(g)]], axis=0),
                           _block_diag(k[b][:, gsl(g)], G), NT) for b, d, g in groups}
    eg = {key: jnp.exp(x) for key, x in gc.items()}
    glrow = {(b, d): gc[b, d][last[d]:last[d] + 1, :] for b, d in dirs}
    kd = {(b, d): k[b] * jnp.exp(glrow[b, d] - gc[b, d]) for b, d in dirs}
    kdt = {(b, d, g): _bdot(eye128, jnp.concatenate(
        [kd[b, d][:, (g * G + j) * LANES:(g * G + j + 1) * LANES] for j in range(G)], axis=0), NT)
        for b, d, g in groups}
    decay = {(b, d, g): jnp.where(incl[d], jnp.exp(jnp.where(incl[d], gcol[b, d, g] - grow[b, d, g], 0.0)), 0.0)
             for b, d, g in groups}
    nm, vbg, kbeg = {}, {}, {}
    for b, d, g in groups:
        x, dec = kq[b, d, g], decay[b, d, g]
        akd_ref[d, b, 0:C, csl(g)] = jnp.where(incl[d], x[C:2 * C] * dec, 0.0).astype(BF16)
        akd_ref[d, b, C:C + DN_DK, csl(g)] = kdt[b, d, g].astype(BF16)
        nm[b, d, g] = jnp.where(strict[d], -x[0:C] * dec, 0.0)
        vbg[b, d, g] = (v[b] * bet[b, d])[:, gsl(g)]
        kbeg[b, d, g] = (kbm[b, d] * eg[b, d])[:, gsl(g)]
    for b, d in dirs:
        wq_ref[d, (2 * b + 1) * C:(2 * b + 2) * C, :] = (q[b] * eg[b, d]).astype(BF16)
        gl_ref[d, b] = jnp.exp(glrow[b, d])
    tinv = {key: eyecat + x for key, x in nm.items()}
    p = dict(nm)
    pbd = {key: _block_diag(x, G) for key, x in p.items()}
    for _ in range(int(math.log2(C)) - 1):
        p = {key: _bdot(x, pbd[key]) for key, x in p.items()}
        pbd = {key: _block_diag(x, G) for key, x in p.items()}
        tinv = {key: t + _bdot(t, pbd[key]) for key, t in tinv.items()}
    for b, d, g in groups:
        t = tinv[b, d, g]
        u_ref[d, rows(b), gsl(g)] = _bdot(t, _block_diag(vbg[b, d, g], G))
        wq_ref[d, 2 * b * C:(2 * b + 1) * C, gsl(g)] = _bdot(t, _block_diag(kbeg[b, d, g], G)).astype(BF16)


def _dn_local(q, k, v, gb):
    n, d = q.shape
    C = DN_CHUNK
    nc = n // C
    nb = 2 if nc % 2 == 0 else 1
    tok = pl.BlockSpec((nb * C, d), lambda j: (j, 0))
    return pl.pallas_call(
        _dn_local_kernel, grid=(nc // nb,),
        in_specs=[tok, tok, tok, pl.BlockSpec((nb * C, gb.shape[1]), lambda j: (j, 0))],
        out_specs=[pl.BlockSpec((2, nb * C, d), lambda j: (0, j, 0)),
                   pl.BlockSpec((2, nb * 2 * C, d), lambda j: (0, j, 0)),
                   pl.BlockSpec((2, nb, C + DN_DK, DN_HEADS * C), lambda j: (0, j, 0, 0)),
                   pl.BlockSpec((2, nb, 1, d), lambda j: (0, j, 0, 0))],
        out_shape=[jax.ShapeDtypeStruct((2, n, d), F32),
                   jax.ShapeDtypeStruct((2, 2 * n, d), BF16),
                   jax.ShapeDtypeStruct((2, nc, C + DN_DK, DN_HEADS * C), BF16),
                   jax.ShapeDtypeStruct((2, nc, 1, d), F32)],
        compiler_params=_cparams(("parallel",)))(q, k, v, gb)


def _dn_scan_kernel(*refs):
    ins, (of_ref, or_ref, s_ref) = refs[:8], refs[8:]
    C = DN_CHUNK

    @pl.when(pl.program_id(1) == 0)
    def _():
        s_ref[...] = jnp.zeros_like(s_ref)

    chains = [(d, p) for d in range(2) for p in range(DN_HEADS // 2)]
    outs = (of_ref, or_ref)
    lanes = lambda p: slice(2 * p * LANES, (2 * p + 2) * LANES)
    s2 = [jnp.concatenate([s_ref[d, 2 * p], s_ref[d, 2 * p + 1]], axis=-1) for d, p in chains]
    r = [_bdot(ins[4 * d + 1][0, :, lanes(p)], _block_diag(s, 2)) for (d, p), s in zip(chains, s2)]
    vn = [ins[4 * d][0, :, lanes(p)] - x[0:C] for (d, p), x in zip(chains, r)]
    r2 = [_bdot(ins[4 * d + 2][0, 0, :, 2 * p * C:(2 * p + 2) * C], _block_diag(x, 2))
          for (d, p), x in zip(chains, vn)]
    for (d, p), s, x, y in zip(chains, s2, r, r2):
        outs[d][:, lanes(p)] = x[C:2 * C] + y[0:C]
        snew = s * ins[4 * d + 3][0, 0, :, lanes(p)] + y[C:C + DN_DK]
        s_ref[d, 2 * p] = snew[:, 0:LANES]
        s_ref[d, 2 * p + 1] = snew[:, LANES:2 * LANES]


def _dn_scan(B, T, ctx_len, u, wq, akd, gl):
    _, n, d = u.shape
    C = DN_CHUNK
    nch, ncc = T // C, ctx_len // C
    rpos = lambda c: jnp.where(c < ncc, ncc - 1 - c, nch - 1 - (c - ncc))
    pos = (lambda b, c: b * nch + c, lambda b, c: b * nch + rpos(c))
    in_specs, args = [], []
    for dd in range(2):
        p = pos[dd]
        in_specs += [pl.BlockSpec((1, C, d), lambda b, c, p=p, dd=dd: (dd, p(b, c), 0)),
                     pl.BlockSpec((1, 2 * C, d), lambda b, c, p=p, dd=dd: (dd, p(b, c), 0)),
                     pl.BlockSpec((1, 1, C + DN_DK, DN_HEADS * C), lambda b, c, p=p, dd=dd: (dd, p(b, c), 0, 0)),
                     pl.BlockSpec((1, 1, 1, d), lambda b, c, p=p, dd=dd: (dd, p(b, c), 0, 0))]
        args += [u, wq, akd, gl]
    return pl.pallas_call(
        _dn_scan_kernel, grid=(B, nch), in_specs=in_specs,
        out_specs=[pl.BlockSpec((C, d), lambda b, c: (pos[0](b, c), 0)),
                   pl.BlockSpec((C, d), lambda b, c: (pos[1](b, c), 0))],
        out_shape=[jax.ShapeDtypeStruct((n, d), F32)] * 2,
        scratch_shapes=[pltpu.VMEM((2, DN_HEADS, DN_DK, LANES), F32)],
        compiler_params=_cparams(("arbitrary", "arbitrary")))(*args)


def _final_kernel(x_ref, y_ref, mod_ref, o_ref):
    o_ref[0] = x_ref[...] + mod_ref[0, 5:6, :] * y_ref[...]


def _final(B, T, ctx_len, x, y, mod):
    n, d = x.shape
    tm = TOKEN_BLOCK
    nblk, ncb = T // tm, ctx_len // tm
    tok = pl.BlockSpec((tm, d), lambda b, i: (b * nblk + ncb + i, 0))
    return pl.pallas_call(
        _final_kernel, grid=(B, nblk - ncb),
        in_specs=[tok, tok, pl.BlockSpec((1, 6, d), lambda b, i: (b, 0, 0))],
        out_specs=pl.BlockSpec((1, tm, d), lambda b, i: (b, i, 0)),
        out_shape=jax.ShapeDtypeStruct((B, T - ctx_len, d), F32),
        compiler_params=_cparams(("parallel", "parallel")))(x, y, mod)


def _rope_tables(seq, ctx_len):
    rows = seq // GRID_W
    r = jnp.broadcast_to(jnp.arange(rows, dtype=F32)[:, None], (rows, GRID_W)).reshape(-1)
    cl = jnp.broadcast_to(jnp.arange(GRID_W, dtype=F32)[None, :], (rows, GRID_W)).reshape(-1)

    def angles(rot_dim):
        nf = rot_dim // 4
        inv = ROPE_BASE ** (-jnp.arange(nf, dtype=F32) / nf)
        ang = jnp.concatenate([r[:, None] * inv, cl[:, None] * inv], axis=-1)
        ang = jnp.concatenate([jnp.zeros((ctx_len, rot_dim // 2), F32), ang], axis=0)
        return jnp.cos(ang), jnp.sin(ang)

    T = seq + ctx_len
    ca, sa = angles(DIFF_DIM)
    z = jnp.zeros_like(sa)
    ta = jnp.stack([jnp.concatenate([ca, ca] * 2, -1), jnp.concatenate([-sa, z] * 2, -1),
                    jnp.concatenate([z, sa] * 2, -1)])
    cb, sb = angles(MLA_ROPE)
    one, zn, zt = jnp.ones((T, MLA_NOPE), F32), jnp.zeros((T, MLA_NOPE), F32), jnp.zeros((T, LANES - MLA_QK), F32)
    zb = jnp.zeros_like(sb)
    tb = jnp.stack([jnp.concatenate([one, cb, cb, 1.0 + zt], -1), jnp.concatenate([zn, -sb, zb, zt], -1),
                    jnp.concatenate([zn, zb, sb, zt], -1)])
    return ta, tb


def _pad_heads(w, heads, width):
    lead = w.shape[:-1]
    w = w.reshape(lead + (heads, width))
    return jnp.pad(w, [(0, 0)] * len(lead) + [(0, 0), (0, LANES - width)]).reshape(lead + (heads * LANES,))


def _block_diag_ones(n, group):
    idx = np.arange(n) // group
    return jnp.asarray(idx[:, None] == idx[None, :], dtype=BF16)


def kernel(x, c, ctx, c_ctx, ada_w, ada_b, ev_w_in, ev_w_out, diff_q_norm, diff_k_norm, diff_lam_q1, diff_lam_k1, diff_lam_q2, diff_lam_k2, diff_subln, mla_cq_norm, mla_ckv_norm, mla_w_uq, mla_w_ukv, mla_q_norm, mla_k_norm, dn_w_in, dn_conv, dn_a_log, dn_dt_bias, dn_o_norm, dn_w_out, peer_wq, peer_k1, peer_k2, peer_u, peer_v):
    B, S, D = x.shape
    CTX = ctx.shape[1]
    T = CTX + S
    depth = ada_w.shape[0]
    tm = TOKEN_BLOCK
    assert CTX % tm == 0 and S % tm == 0 and (B * T) % PEER_TOKEN_BLOCK == 0 and B + 1 <= 8
    nblk, ncb = T // tm, CTX // tm
    geo = (B, nblk, tm, lambda b, i: jnp.where(i < ncb, B, b))

    cc = jnp.zeros((8, D), F32).at[:B].set(c).at[B].set(c_ctx)
    mods = _adaln(cc, ada_w, ada_b).reshape(depth, 8, 6, D)
    xs = jnp.concatenate([ctx, x], axis=1).reshape(B * T, D)
    ropea, ropeb = _rope_tables(S, CTX)
    bd64, bd128 = _block_diag_ones(HEAD_W, DIFF_DIM), _block_diag_ones(HEAD_W, LANES)

    y = None
    for l in range(depth):
        i = l // 2
        mod = mods[l]
        gmod = mods[l - 1] if l else None
        if l % 2 == 0:
            lam_init = 0.8 - 0.6 * math.exp(-0.3 * l)
            w_in = ev_w_in[i]
            o = 3 * HEAD_W + MLA_Q_RANK + MLA_KV_RANK
            kr_rep = jnp.pad(jnp.broadcast_to(w_in[:, None, o:o + MLA_ROPE], (D, MLA_HEADS, MLA_ROPE)),
                             ((0, 0), (0, 0), (MLA_NOPE, LANES - MLA_QK))).reshape(D, HEAD_W)
            w_cat = jnp.concatenate([w_in[:, :o], kr_rep], axis=1).astype(BF16)
            res = _modmm(geo, xs, mod, w_cat, y, gmod)
            (xs, z) = res if y is not None else (xs, res)
            ukv = mla_w_ukv[i].reshape(MLA_KV_RANK, MLA_HEADS, MLA_NOPE + MLA_VDIM)
            gains = jnp.stack([jnp.tile(diff_q_norm[i], HEAD_W // DIFF_DIM), jnp.tile(diff_k_norm[i], HEAD_W // DIFF_DIM),
                               jnp.tile(jnp.pad(mla_q_norm[i], (0, LANES - MLA_QK)), MLA_HEADS),
                               jnp.tile(jnp.pad(mla_k_norm[i], (0, LANES - MLA_QK)), MLA_HEADS)])
            qa, ka, va, qb, kb, vb = _even_prep(
                geo, z, ropea, ropeb, gains, mla_cq_norm[i][None, :], mla_ckv_norm[i][None, :],
                _pad_heads(mla_w_uq[i], MLA_HEADS, MLA_QK).astype(BF16),
                _pad_heads(ukv[:, :, :MLA_NOPE].reshape(MLA_KV_RANK, -1), MLA_HEADS, MLA_NOPE).astype(BF16),
                ukv[:, :, MLA_NOPE:].reshape(MLA_KV_RANK, -1).astype(BF16), bd64, bd128)
            lam_vecs = jnp.stack([diff_lam_q1[i], diff_lam_k1[i], diff_lam_q2[i], diff_lam_k2[i]])
            oa = _attention(B, T, CTX, qa, ka, va, lam_vecs, diff_subln[i][None, :], lam_init)
            ob = _attention(B, T, CTX, qb, kb, vb)
            xs, hqt = _mixer_out(geo, _even_out_kernel, [(oa, HEAD_W, 0), (ob, HEAD_W, 0)], [], xs, mod,
                                 ev_w_out[i].astype(BF16))
        else:
            nin = dn_w_in.shape[2]
            w_in = jnp.pad(dn_w_in[i], ((0, 0), (0, -nin % LANES))).astype(BF16)
            res = _modmm(geo, xs, mod, w_in, y, gmod)
            (xs, z) = res if y is not None else (xs, res)
            q, k, v, gb = _dn_prep(geo, ncb, z, dn_conv[i], dn_a_log[i].reshape(1, -1),
                                   dn_dt_bias[i].reshape(1, -1))
            of, orv = _dn_scan(B, T, CTX, *_dn_local(q, k, v, gb))
            xs, hqt = _mixer_out(geo, _dn_out_kernel,
                                 [(of, D, 0), (orv, D, 0), (z, D, DN_QKV // D)], [dn_o_norm[i][None, :]], xs, mod,
                                 dn_w_out[i].astype(BF16))
        y = _peer(hqt, peer_wq[l], peer_k1[l], peer_k2[l], peer_u[l], peer_v[l])
    return _final(B, T, CTX, xs, y, mods[depth - 1])
```

```python
import functools
import math

import jax
import jax.numpy as jnp
import numpy as np
from jax import lax
from jax.experimental import pallas as pl
from jax.experimental.pallas import tpu as pltpu

F32 = jnp.float32
BF16 = jnp.bfloat16
HI = lax.Precision.HIGHEST
NT = (((1,), (1,)), ((), ()))

EPS = 1e-6
LOG2E = math.log2(math.e)
ROPE_BASE = 10000.0
GRID_W = 64
LANES, SUBLANES = 128, 8
MXU_TILE = 256
TOKEN_BLOCK = 256
VMEM_LIMIT = 56 * 1024 * 1024

DIFF_HEADS, DIFF_DIM = 4, 64
MLA_HEADS, MLA_Q_RANK, MLA_KV_RANK, MLA_NOPE, MLA_ROPE, MLA_VDIM = 4, 256, 128, 64, 32, 128
MLA_QK = MLA_NOPE + MLA_ROPE
HEAD_W = DIFF_HEADS * LANES
DN_HEADS, DN_DK, DN_CONV, DN_CHUNK = 8, 128, 5, 64
DN_QKV = 3 * DN_HEADS * DN_DK
PEER_HEADS, PEER_NKEYS, PEER_TOPK = 8, 128, 16
PEER_CAND_ROWS = 80
PEER_TOKEN_BLOCK = 512
PEER_EXPERT_BLOCK = 2048
PEER_PIECE = MXU_TILE


def _cparams(sem):
    return pltpu.CompilerParams(dimension_semantics=sem, vmem_limit_bytes=VMEM_LIMIT)


def _rms_rows(x):
    return x * lax.rsqrt(jnp.mean(x * x, axis=-1, keepdims=True) + EPS)


def _silu(x):
    return x * jax.nn.sigmoid(x)


def _group_sum(sq, bd):
    hi = sq.astype(BF16)
    lo = (sq - hi.astype(F32)).astype(BF16)
    return jnp.dot(hi, bd, preferred_element_type=F32) + jnp.dot(lo, bd, preferred_element_type=F32)


def _adaln_kernel(c_ref, w_ref, b_ref, o_ref):
    s = _silu(c_ref[...])
    o_ref[0] = jnp.dot(s, w_ref[0], precision=HI, preferred_element_type=F32) + b_ref[0]


def _adaln(cc, ada_w, ada_b):
    depth, d, n = ada_w.shape
    rows = cc.shape[0]
    tn = n // 4
    return pl.pallas_call(
        _adaln_kernel,
        grid=(depth, n // tn),
        in_specs=[pl.BlockSpec((rows, d), lambda l, j: (0, 0)),
                  pl.BlockSpec((1, d, tn), lambda l, j: (l, 0, j)),
                  pl.BlockSpec((1, 1, tn), lambda l, j: (l, 0, j))],
        out_specs=pl.BlockSpec((1, rows, tn), lambda l, j: (l, 0, j)),
        out_shape=jax.ShapeDtypeStruct((depth, rows, n), F32),
        compiler_params=_cparams(("arbitrary", "arbitrary")),
    )(cc, ada_w, ada_b.reshape(depth, 1, n))


def _modmm_kernel(*refs, has_y):
    if has_y:
        x_ref, y_ref, gmod_ref, mod_ref, w_ref, xo_ref, z_ref = refs
        x = x_ref[...] + gmod_ref[0, 5:6, :] * y_ref[...]
        xo_ref[...] = x
    else:
        x_ref, mod_ref, w_ref, z_ref = refs
        x = x_ref[...]
    h = _rms_rows(x) * (1.0 + mod_ref[0, 1:2, :]) + mod_ref[0, 0:1, :]
    z_ref[...] = jnp.dot(h.astype(BF16), w_ref[...], preferred_element_type=F32)


def _modmm(geo, x, mod, w, y=None, gmod=None):
    B, nblk, tm, mrow = geo
    n, d = x.shape
    nout = w.shape[1]
    tok = pl.BlockSpec((tm, d), lambda b, i: (b * nblk + i, 0))
    modspec = pl.BlockSpec((1, 6, d), lambda b, i: (mrow(b, i), 0, 0))
    wspec = pl.BlockSpec((d, nout), lambda b, i: (0, 0))
    zspec = pl.BlockSpec((tm, nout), lambda b, i: (b * nblk + i, 0))
    zshape = jax.ShapeDtypeStruct((n, nout), F32)
    if y is None:
        return pl.pallas_call(
            functools.partial(_modmm_kernel, has_y=False), grid=(B, nblk),
            in_specs=[tok, modspec, wspec], out_specs=zspec, out_shape=zshape,
            compiler_params=_cparams(("parallel", "parallel")))(x, mod, w)
    return pl.pallas_call(
        functools.partial(_modmm_kernel, has_y=True), grid=(B, nblk),
        in_specs=[tok, tok, modspec, modspec, wspec], out_specs=[tok, zspec],
        out_shape=[jax.ShapeDtypeStruct((n, d), F32), zshape],
        compiler_params=_cparams(("parallel", "parallel")))(x, y, gmod, mod, w)


def _rope_lanes(v, tab_ref, half):
    rep = v.shape[-1] // LANES
    c = jnp.concatenate([tab_ref[0]] * rep, axis=-1)
    sm = jnp.concatenate([tab_ref[1]] * rep, axis=-1)
    sp = jnp.concatenate([tab_ref[2]] * rep, axis=-1)
    n = v.shape[-1]
    return v * c + pltpu.roll(v, n - half, 1) * sm + pltpu.roll(v, half, 1) * sp


def _even_prep_kernel(z_ref, ropea_ref, ropeb_ref, gains_ref, cqn_ref, ckvn_ref, wuq_ref, wuk_ref, wuv_ref,
                      bd64_ref, bd128_ref, qa_ref, ka_ref, va_ref, qb_ref, kb_ref, vb_ref):
    W = HEAD_W
    bd64 = bd64_ref[...]
    bd128 = bd128_ref[...]

    def norm_groups(v, bd, width, gain):
        ms = _group_sum(v * v, bd) * (1.0 / width)
        return v * lax.rsqrt(ms + EPS) * gain

    qa = norm_groups(z_ref[:, 0:W], bd64, DIFF_DIM, gains_ref[0:1, :])
    ka = norm_groups(z_ref[:, W:2 * W], bd64, DIFF_DIM, gains_ref[1:2, :])
    qa_ref[...] = (_rope_lanes(qa, ropea_ref, DIFF_DIM // 2) * (DIFF_DIM ** -0.5 * LOG2E)).astype(BF16)
    ka_ref[...] = _rope_lanes(ka, ropea_ref, DIFF_DIM // 2).astype(BF16)
    va_ref[...] = z_ref[:, 2 * W:3 * W].astype(BF16)

    o = 3 * W
    cq = (_rms_rows(z_ref[:, o:o + MLA_Q_RANK]) * cqn_ref[...]).astype(BF16)
    o += MLA_Q_RANK
    ckv = (_rms_rows(z_ref[:, o:o + MLA_KV_RANK]) * ckvn_ref[...]).astype(BF16)
    o += MLA_KV_RANK
    kr = z_ref[:, o:o + W]
    qb = jnp.dot(cq, wuq_ref[...], preferred_element_type=F32)
    kb = jnp.dot(ckv, wuk_ref[...], preferred_element_type=F32) + kr
    qb = norm_groups(qb, bd128, MLA_QK, gains_ref[2:3, :])
    kb = norm_groups(kb, bd128, MLA_QK, gains_ref[3:4, :])
    qb_ref[...] = (_rope_lanes(qb, ropeb_ref, MLA_ROPE // 2) * (MLA_QK ** -0.5 * LOG2E)).astype(BF16)
    kb_ref[...] = _rope_lanes(kb, ropeb_ref, MLA_ROPE // 2).astype(BF16)
    vb_ref[...] = jnp.dot(ckv, wuv_ref[...], preferred_element_type=F32).astype(BF16)


def _even_prep(geo, z, ropea, ropeb, gains, cqn, ckvn, wuq, wuk, wuv, bd64, bd128):
    B, nblk, tm, _ = geo
    n = z.shape[0]
    W = HEAD_W
    full = lambda a: pl.BlockSpec(a.shape, lambda b, i: (0,) * a.ndim)
    rope = pl.BlockSpec((3, tm, LANES), lambda b, i: (0, i, 0))
    out = pl.BlockSpec((tm, W), lambda b, i: (b * nblk + i, 0))
    return pl.pallas_call(
        _even_prep_kernel, grid=(B, nblk),
        in_specs=[pl.BlockSpec((tm, z.shape[1]), lambda b, i: (b * nblk + i, 0)), rope, rope, full(gains),
                  full(cqn), full(ckvn), full(wuq), full(wuk), full(wuv), full(bd64), full(bd128)],
        out_specs=[out] * 6, out_shape=[jax.ShapeDtypeStruct((n, W), BF16)] * 6,
        compiler_params=_cparams(("parallel", "parallel")))(z, ropea, ropeb, gains, cqn, ckvn, wuq, wuk, wuv,
                                                            bd64, bd128)


def _attn_kernel(*refs, diff, tk, row_split, n_ctx_q, ctx_len, n_keys, lam_init):
    if diff:
        q_ref, k_ref, v_ref, lam_ref, subln_ref, o_ref, s_ref = refs
    else:
        q_ref, k_ref, v_ref, o_ref, s_ref = refs
    qi = pl.program_id(2)
    tr = q_ref.shape[0] // row_split
    qs = []
    for r in range(row_split):
        q = q_ref[r * tr:(r + 1) * tr, :]
        if diff:
            lane = lax.broadcasted_iota(jnp.int32, q.shape, 1)
            zero = jnp.zeros_like(q)
            qs += [jnp.where(lane < DIFF_DIM, q, zero), jnp.where(lane >= DIFF_DIM, q, zero)]
        else:
            qs.append(q)
    nsub = len(qs) // row_split

    def scores(slot, start, size):
        k = k_ref[pl.ds(start, size), :]
        s = [lax.dot_general(qq, k, NT, preferred_element_type=F32) for qq in qs]
        for j, x in enumerate(s):
            s_ref[slot, j, :, 0:size] = x
        return tuple(jnp.max(x, axis=-1, keepdims=True) for x in s)

    def update(slot, start, size, mx, carry):
        v = v_ref[pl.ds(start, size), :]
        mn = [jnp.maximum(m, x) for (m, _, _), x in zip(carry, mx)]
        p = [jnp.exp2(s_ref[slot, j, :, 0:size] - y) for j, y in enumerate(mn)]
        alpha = [jnp.exp2(m - y) for (m, _, _), y in zip(carry, mn)]
        pv = [jnp.dot(x.astype(BF16), v, preferred_element_type=F32) for x in p]
        return tuple((y, a * l + jnp.sum(x, axis=-1, keepdims=True), a * acc + z)
                     for (_, l, acc), y, a, x, z in zip(carry, mn, alpha, p, pv))

    init = tuple((jnp.full((tr, 1), -1e30, F32), jnp.zeros((tr, 1), F32), jnp.zeros((tr, LANES), F32))
                 for _ in qs)

    def finish(carry):
        outs = [acc / l for (_, l, acc) in carry]
        for r in range(row_split):
            if diff:
                lam = (jnp.exp(jnp.sum(lam_ref[0:1, :] * lam_ref[1:2, :], axis=-1, keepdims=True))
                       - jnp.exp(jnp.sum(lam_ref[2:3, :] * lam_ref[3:4, :], axis=-1, keepdims=True)) + lam_init)
                o = _rms_rows(outs[r * nsub] - lam * outs[r * nsub + 1]) * subln_ref[...] * (1.0 - lam_init)
            else:
                o = outs[r]
            o_ref[r * tr:(r + 1) * tr, :] = o.astype(o_ref.dtype)

    @pl.when(qi < n_ctx_q)
    def _():
        finish(update(0, 0, ctx_len, scores(0, 0, ctx_len), init))

    @pl.when(qi >= n_ctx_q)
    def _():
        n = n_keys // tk
        at = lambda c: pl.multiple_of(c * tk, tk)

        def pair(j, state):
            carry, mx0 = state
            mx1 = scores(1, at(2 * j + 1), tk)
            carry = update(0, at(2 * j), tk, mx0, carry)
            mx0 = scores(0, at(2 * j + 2), tk)
            carry = update(1, at(2 * j + 1), tk, mx1, carry)
            return carry, mx0

        carry, mx0 = lax.fori_loop(0, (n - 1) // 2, pair, (init, scores(0, 0, tk)))
        if n % 2 == 0:
            mx1 = scores(1, (n - 1) * tk, tk)
            carry = update(0, (n - 2) * tk, tk, mx0, carry)
            carry = update(1, (n - 1) * tk, tk, mx1, carry)
        else:
            carry = update(0, (n - 1) * tk, tk, mx0, carry)
        finish(carry)


def _attention(B, T, ctx_len, q, k, v, lam_vecs=None, subln=None, lam_init=0.0):
    diff = lam_vecs is not None
    tq = TOKEN_BLOCK
    tk = next(c for c in (1408, 768, TOKEN_BLOCK) if T % c == 0)
    nq = T // tq
    heads = q.shape[1] // LANES
    qspec = pl.BlockSpec((tq, LANES), lambda b, h, i: (b * nq + i, h))
    kvspec = pl.BlockSpec((T, LANES), lambda b, h, i: (b, h))
    in_specs = [qspec, kvspec, kvspec]
    args = [q, k, v]
    if diff:
        in_specs += [pl.BlockSpec(lam_vecs.shape, lambda b, h, i: (0, 0)),
                     pl.BlockSpec(subln.shape, lambda b, h, i: (0, 0))]
        args += [lam_vecs, subln]
    row_split = 1 if diff else 2
    chains = row_split * (2 if diff else 1)
    return pl.pallas_call(
        functools.partial(_attn_kernel, diff=diff, tk=tk, row_split=row_split, n_ctx_q=ctx_len // tq,
                          ctx_len=ctx_len, n_keys=T, lam_init=lam_init),
        grid=(B, heads, nq), in_specs=in_specs, out_specs=qspec,
        out_shape=jax.ShapeDtypeStruct(q.shape, BF16),
        scratch_shapes=[pltpu.VMEM((2, chains, tq // row_split, max(tk, ctx_len)), F32)],
        compiler_params=_cparams(("parallel", "parallel", "arbitrary")))(*args)


def _residual_tail(x, y, mod_ref, xo_ref, hqt_ref):
    xn = x + mod_ref[0, 2:3, :] * y
    xo_ref[...] = xn
    hq = _rms_rows(xn) * (1.0 + mod_ref[0, 4:5, :]) + mod_ref[0, 3:4, :]
    hqt_ref[...] = hq.T.astype(BF16)


def _even_out_kernel(oa_ref, ob_ref, x_ref, mod_ref, wo_ref, xo_ref, hqt_ref):
    W = HEAD_W
    y = (jnp.dot(oa_ref[...], wo_ref[0:W, :], preferred_element_type=F32)
         + jnp.dot(ob_ref[...], wo_ref[W:2 * W, :], preferred_element_type=F32))
    _residual_tail(x_ref[...], y, mod_ref, xo_ref, hqt_ref)


def _dn_out_kernel(of_ref, or_ref, zg_ref, onorm_ref, x_ref, mod_ref, wo_ref, xo_ref, hqt_ref):
    parts = []
    for h in range(DN_HEADS):
        hs = slice(h * LANES, (h + 1) * LANES)
        o = of_ref[:, hs] + or_ref[:, hs]
        parts.append((_rms_rows(o) * onorm_ref[...] * _silu(zg_ref[:, hs])).astype(BF16))
    y = jnp.dot(jnp.concatenate(parts, axis=-1), wo_ref[...], preferred_element_type=F32)
    _residual_tail(x_ref[...], y, mod_ref, xo_ref, hqt_ref)


def _mixer_out(geo, kernel_fn, token_args, small_args, x, mod, wo, n_skip=0):
    B, nblk, tm, mrow = geo
    n, d = x.shape
    keep = nblk - n_skip
    hq_col = (lambda b, i: b * nblk + i) if n_skip == 0 else (
        lambda b, i: jnp.where(i < n_skip, B * keep + b * n_skip + i, b * keep + i - n_skip))
    n_hq = n
    tokspec = lambda a, col: pl.BlockSpec((tm, a[1]), lambda b, i: (b * nblk + i, col))
    in_specs = [tokspec((a, w), col) for (a, w, col) in token_args]
    in_specs += [pl.BlockSpec(a.shape, lambda b, i: (0,) * a.ndim) for a in small_args]
    in_specs += [pl.BlockSpec((tm, d), lambda b, i: (b * nblk + i, 0)),
                 pl.BlockSpec((1, 6, d), lambda b, i: (mrow(b, i), 0, 0)),
                 pl.BlockSpec(wo.shape, lambda b, i: (0, 0))]
    return pl.pallas_call(
        kernel_fn, grid=(B, nblk), in_specs=in_specs,
        out_specs=[pl.BlockSpec((tm, d), lambda b, i: (b * nblk + i, 0)),
                   pl.BlockSpec((d, tm), lambda b, i: (0, hq_col(b, i)))],
        out_shape=[jax.ShapeDtypeStruct((n, d), F32), jax.ShapeDtypeStruct((d, n_hq), BF16)],
        compiler_params=_cparams(("parallel", "parallel")))(
            *[a for (a, _, _) in token_args], *small_args, x, mod, wo)


def _peer_select_kernel(hqt_ref, wqt_ref, k1_ref, k2_ref, th_ref, e1_ref, s2_ref, e2_ref, work_ref, top_ref, cand_ref):
    K, H = PEER_TOPK, PEER_HEADS
    qt = jnp.dot(wqt_ref[...], hqt_ref[...], preferred_element_type=F32)
    dk = k1_ref.shape[1]
    for h in range(H):
        q1 = qt[(2 * h) * dk:(2 * h + 1) * dk, :].astype(BF16)
        q2 = qt[(2 * h + 1) * dk:(2 * h + 2) * dk, :].astype(BF16)
        s1 = jnp.dot(k1_ref[...], q1, preferred_element_type=F32)
        s2 = jnp.dot(k2_ref[...], q2, preferred_element_type=F32)
        th_ref[h] = s1
        s2_ref[h] = s2
        work_ref[2 * h] = s1
        work_ref[2 * h + 1] = s2

    def extract(ref, count):
        s = [ref[a] for a in range(count)]
        m = [jnp.max(x, axis=0, keepdims=True) for x in s]
        for a in range(count):
            ref[a] = jnp.where(s[a] == m[a], -jnp.inf, s[a])
        return m

    def top_round(r, carry):
        for a, m in enumerate(extract(work_ref, 2 * H)):
            top_ref[a, pl.ds(r, 1), :] = m
        return carry

    lax.fori_loop(0, K, top_round, 0)

    def candidates(h):
        v1, v2 = top_ref[2 * h], top_ref[2 * h + 1]
        row8 = lax.broadcasted_iota(jnp.int32, (SUBLANES, v1.shape[1]), 0)
        pieces = [v1[0:1, :] + v2, v1[1:2, :] + v2[0:8, :]]
        pieces += [jnp.where(row8 < K // (r1 + 1), v1[r1:r1 + 1, :] + v2[0:8, :], -jnp.inf) for r1 in range(2, 8)]
        pieces.append(v1[8:16, :] + v2[0:1, :])
        return jnp.concatenate(pieces, axis=0)

    for h in range(H):
        cand_ref[h] = candidates(h)
    tau = lax.fori_loop(0, K, lambda r, carry: tuple(extract(cand_ref, H)),
                        tuple(jnp.zeros((1, hqt_ref.shape[1]), F32) for _ in range(H)))
    for h in range(H):
        cand = candidates(h)
        m1, m2 = top_ref[2 * h, 0:1, :], top_ref[2 * h + 1, 0:1, :]
        zsum = jnp.sum(jnp.where(cand >= tau[h], jnp.exp(cand - (m1 + m2)), 0.0), axis=0, keepdims=True)
        s1 = th_ref[h]
        e1_ref[h] = jnp.exp(s1 - m1) / zsum
        e2_ref[h] = jnp.exp(s2_ref[h] - m2)
        v2 = top_ref[2 * h + 1]
        theta = jnp.full(s1.shape, jnp.inf, F32)
        for r in range(K):
            theta = jnp.where((s1 + v2[r:r + 1, :]) >= tau[h], v2[r:r + 1, :], theta)
        th_ref[h] = theta


def _peer_select(hqt, n, wqt, k1, k2):
    d = hqt.shape[0]
    tm = PEER_TOKEN_BLOCK
    H, NK = PEER_HEADS, PEER_NKEYS
    big = pl.BlockSpec((H, NK, tm), lambda j: (0, 0, j))
    bigshape = jax.ShapeDtypeStruct((H, NK, n), F32)
    full = lambda a: pl.BlockSpec(a.shape, lambda j: (0,) * a.ndim)
    return pl.pallas_call(
        _peer_select_kernel, grid=(n // tm,),
        in_specs=[pl.BlockSpec((d, tm), lambda j: (0, j)), full(wqt), full(k1), full(k2)],
        out_specs=[big] * 4, out_shape=[bigshape] * 4,
        scratch_shapes=[pltpu.VMEM((2 * H, NK, tm), F32), pltpu.VMEM((2 * H, PEER_TOPK, tm), F32),
                        pltpu.VMEM((H, PEER_CAND_ROWS, tm), F32)],
        compiler_params=_cparams(("parallel",)))(hqt, wqt, k1, k2)


def _gelu(a):
    return 0.5 * a * (1.0 + lax.erf(a * np.float32(math.sqrt(0.5))))


def _peer_dense_kernel(hqt_ref, u_ref, vt_ref, th_ref, e1_ref, s2_ref, e2_ref, zero_ref, y_ref, row_ref,
                       *piece_refs):
    c = pl.program_id(1)
    NK = PEER_NKEYS
    n_i = PEER_EXPERT_BLOCK // NK
    pieces = [piece_refs[3 * p:3 * p + 3] for p in range(len(piece_refs) // 3)]

    @pl.when(c == 0)
    def _():
        for acc_ref, _, _ in pieces:
            acc_ref[...] = jnp.zeros_like(acc_ref)

    H = PEER_HEADS
    strips = [slice(ts * LANES, (ts + 1) * LANES) for ts in range(hqt_ref.shape[1] // LANES)]
    for ii in range(n_i):
        for h in range(H):
            i = c * n_i + ii
            row_ref[0, ii, h:h + 1, :] = th_ref[h, pl.ds(i, 1), :]
            row_ref[1, ii, h:h + 1, :] = e1_ref[h, pl.ds(i, 1), :]

    JB, RUNS, PIECE = 16, 8, PEER_PIECE
    zero = zero_ref[0:JB, :]
    for p, (_, act_ref, _) in enumerate(pieces):
        pc = slice(p * PIECE, (p + 1) * PIECE)
        act_ref[...] = _gelu(jnp.dot(u_ref[...], hqt_ref[:, pc], preferred_element_type=F32))
    for p, (acc_ref, act_ref, wt_ref) in enumerate(pieces):
        for ts in range(PIECE // LANES):
            tl = strips[p * PIECE // LANES + ts]
            pl_ = strips[ts]
            for i0 in range(0, n_i, RUNS):
                for jb in range(NK // JB):
                    js = slice(jb * JB, (jb + 1) * JB)
                    g = [jnp.zeros((JB, LANES), F32) for _ in range(RUNS)]
                    for h in range(H):
                        s2 = s2_ref[h, js, tl] + zero
                        e2 = e2_ref[h, js, tl] + zero
                        for r in range(RUNS):
                            sel = s2 >= row_ref[0, i0 + r, h:h + 1, tl]
                            g[r] = g[r] + jnp.where(sel, row_ref[1, i0 + r, h:h + 1, tl] * e2, 0.0)
                    for r in range(RUNS):
                        rows = slice((i0 + r) * NK + jb * JB, (i0 + r) * NK + (jb + 1) * JB)
                        wt_ref[rows, pl_] = (g[r] * act_ref[rows, pl_]).astype(BF16)
        acc_ref[...] += jnp.dot(vt_ref[...], wt_ref[...], preferred_element_type=F32)

    @pl.when(c == pl.num_programs(1) - 1)
    def _():
        for p, (acc_ref, _, _) in enumerate(pieces):
            y_ref[p * PIECE:(p + 1) * PIECE, :] = acc_ref[...].T


def _peer_dense(hqt, n, u, vt, theta, e1, s2, e2):
    d = hqt.shape[0]
    tm, te = PEER_TOKEN_BLOCK, PEER_EXPERT_BLOCK
    H, NK = PEER_HEADS, PEER_NKEYS
    big = pl.BlockSpec((H, NK, tm), lambda j, c: (0, 0, j))
    return pl.pallas_call(
        _peer_dense_kernel, grid=(n // tm, u.shape[0] // te),
        in_specs=[pl.BlockSpec((d, tm), lambda j, c: (0, j)),
                  pl.BlockSpec((te, d), lambda j, c: (c, 0)),
                  pl.BlockSpec((d, te), lambda j, c: (0, c)),
                  big, big, big, big, pl.BlockSpec((NK, LANES), lambda j, c: (0, 0))],
        out_specs=pl.BlockSpec((tm, d), lambda j, c: (j, 0)),
        out_shape=jax.ShapeDtypeStruct((n, d), F32),
        scratch_shapes=[pltpu.VMEM((2, te // NK, H, tm), F32)]
        + [pltpu.VMEM((d, PEER_PIECE), F32), pltpu.VMEM((te, PEER_PIECE), F32),
           pltpu.VMEM((te, PEER_PIECE), BF16)] * (tm // PEER_PIECE),
        compiler_params=_cparams(("parallel", "arbitrary")))(hqt, u, vt, theta, e1, s2, e2,
                                                             jnp.zeros((NK, LANES), F32))


def _peer(hqt, n, wq, k1, k2, u_tab, v_tab):
    sel = _peer_select(hqt, n, wq.T.astype(BF16), k1.astype(BF16), k2.astype(BF16))
    return _peer_dense(hqt, n, u_tab.astype(BF16), v_tab.T.astype(BF16), *sel)


def _dn_prep_kernel(z_ref, prev_ref, next_ref, conv_ref, alog_ref, dtb_ref, ab_ref, q_ref, k_ref, v_ref, gb_ref,
                    ext_ref, *, n_ctx_blk, nblk):
    i = pl.program_id(1)
    tm = z_ref.shape[0]
    halo = prev_ref.shape[0]
    pad = DN_CONV // 2
    has_prev = jnp.logical_and(i != 0, i != n_ctx_blk)
    has_next = jnp.logical_and(i != n_ctx_blk - 1, i != nblk - 1)
    ext_ref[0:halo, :] = jnp.where(has_prev, prev_ref[...], 0.0)
    ext_ref[halo:halo + tm, :] = z_ref[...]
    ext_ref[halo + tm:, :] = jnp.where(has_next, next_ref[...], 0.0)
    nq = DN_HEADS * DN_DK
    for j in range(DN_QKV // LANES):
        cs = slice(j * LANES, (j + 1) * LANES)
        acc = conv_ref[0:1, cs] * ext_ref[halo - pad:halo - pad + tm, cs]
        for t in range(1, DN_CONV):
            acc = acc + conv_ref[t:t + 1, cs] * ext_ref[halo - pad + t:halo - pad + t + tm, cs]
        y = _silu(acc)
        if j * LANES < 2 * nq:
            y = y * lax.rsqrt(jnp.sum(y * y, axis=-1, keepdims=True) + EPS)
        if j * LANES < nq:
            q_ref[:, cs] = y * DN_DK ** -0.5
        elif j * LANES < 2 * nq:
            k_ref[:, slice(j * LANES - nq, (j + 1) * LANES - nq)] = y
        else:
            v_ref[:, slice(j * LANES - 2 * nq, (j + 1) * LANES - 2 * nq)] = y
    ab = ab_ref[...]
    nh = 2 * DN_HEADS
    xa = ab[:, 0:nh] + dtb_ref[...]
    softplus = jnp.maximum(xa, 0.0) + jnp.log(1.0 + jnp.exp(-jnp.abs(xa)))
    gb_ref[:, 0:nh] = -jnp.exp(alog_ref[...]) * softplus
    gb_ref[:, nh:2 * nh] = jax.nn.sigmoid(ab[:, nh:2 * nh])


def _dn_prep(geo, n_ctx_blk, z, conv_w, alog, dtb):
    B, nblk, tm, _ = geo
    n = z.shape[0]
    halo = SUBLANES
    r = tm // halo
    nh8 = n // halo
    d = DN_HEADS * DN_DK
    prev = pl.BlockSpec((halo, DN_QKV), lambda b, i: (jnp.maximum((b * nblk + i) * r - 1, 0), 0))
    nxt = pl.BlockSpec((halo, DN_QKV), lambda b, i: (jnp.minimum((b * nblk + i + 1) * r, nh8 - 1), 0))
    ab = z[:, DN_QKV + d:DN_QKV + d + 4 * DN_HEADS]
    tok = pl.BlockSpec((tm, d), lambda b, i: (b * nblk + i, 0))
    gbspec = pl.BlockSpec((tm, 4 * DN_HEADS), lambda b, i: (b * nblk + i, 0))
    full = lambda a: pl.BlockSpec(a.shape, lambda b, i: (0,) * a.ndim)
    return pl.pallas_call(
        functools.partial(_dn_prep_kernel, n_ctx_blk=n_ctx_blk, nblk=nblk), grid=(B, nblk),
        in_specs=[pl.BlockSpec((tm, DN_QKV), lambda b, i: (b * nblk + i, 0)), prev, nxt, full(conv_w),
                  full(alog), full(dtb), gbspec],
        out_specs=[tok, tok, tok, gbspec],
        out_shape=[jax.ShapeDtypeStruct((n, d), F32)] * 3 + [jax.ShapeDtypeStruct((n, 4 * DN_HEADS), F32)],
        scratch_shapes=[pltpu.VMEM((tm + 2 * halo, DN_QKV), F32)],
        compiler_params=_cparams(("parallel", "parallel")))(z, z, z, conv_w, alog, dtb, ab)


DN_GROUP = 4


def _bdot(a, b, dims=None):
    a, b = a.astype(BF16), b.astype(BF16)
    if dims is None:
        return jnp.dot(a, b, preferred_element_type=F32)
    return lax.dot_general(a, b, dims, preferred_element_type=F32)


def _block_diag(x, nblk):
    r, n = x.shape
    w = n // nblk
    tall = jnp.concatenate([x] * nblk, axis=0)
    rb = lax.broadcasted_iota(jnp.int32, tall.shape, 0) // r
    lb = lax.broadcasted_iota(jnp.int32, tall.shape, 1) // w
    return jnp.where(rb == lb, tall, jnp.zeros_like(tall))


def _dn_local_kernel(q_ref, k_ref, v_ref, gb_ref, u_ref, wq_ref, akd_ref, gl_ref):
    C, G, nh = DN_CHUNK, DN_GROUP, DN_HEADS
    d_all = nh * LANES
    ri = lax.broadcasted_iota(jnp.int32, (C, G * C), 0)
    ci = lax.broadcasted_iota(jnp.int32, (C, G * C), 1) % C
    eyecat = (ri == ci).astype(F32)
    ones = jnp.ones((C, C), F32)
    r2 = lax.broadcasted_iota(jnp.int32, (C, C), 0)
    c2 = lax.broadcasted_iota(jnp.int32, (C, C), 1)
    eye128 = (lax.broadcasted_iota(jnp.int32, (LANES, LANES), 0)
              == lax.broadcasted_iota(jnp.int32, (LANES, LANES), 1)).astype(BF16)
    ncol = gb_ref.shape[1]
    sel_row = lax.broadcasted_iota(jnp.int32, (ncol, d_all), 0)
    sel_head = lax.broadcasted_iota(jnp.int32, (ncol, d_all), 1) // LANES
    cat_row = lax.broadcasted_iota(jnp.int32, (ncol, G * C), 0)
    cat_head = lax.broadcasted_iota(jnp.int32, (ncol, G * C), 1) // C
    hdot = lambda a, b: jnp.dot(a, b, precision=HI, preferred_element_type=F32)
    nblk = q_ref.shape[0] // C
    rows = lambda b: slice(b * C, (b + 1) * C)
    q, k, v, gb = ([ref[rows(b), :] for b in range(nblk)] for ref in (q_ref, k_ref, v_ref, gb_ref))
    dirs = [(b, d) for b in range(nblk) for d in (0, 1)]
    groups = [(b, d, g) for b, d in dirs for g in range(nh // G)]
    gsl = lambda g: slice(g * G * LANES, (g + 1) * G * LANES)
    csl = lambda g: slice(g * G * C, (g + 1) * G * C)
    incl = [(ri >= ci), (ri <= ci)]
    strict = [(ri > ci), (ri < ci)]
    last = [C - 1, 0]
    tri = [(r2 >= c2).astype(F32), (r2 <= c2).astype(F32)]
    gcs = {(b, d): hdot(tri[d], gb[b]) for b, d in dirs}
    spread = lambda m, first: hdot(m, (sel_row == first + sel_head).astype(F32))
    gc = {(b, d): spread(gcs[b, d], d * nh) for b, d in dirs}
    bet = {(b, d): spread(gb[b], 2 * nh + d * nh) for b, d in dirs}
    gcol = {(b, d, g): hdot(gcs[b, d], (cat_row == d * nh + g * G + cat_head).astype(F32)) for b, d, g in groups}
    grow = {key: hdot(ones, x * eyecat) for key, x in gcol.items()}
    kbm = {(b, d): k[b] * bet[b, d] for b, d in dirs}
    kq = {(b, d, g): _bdot(jnp.concatenate([kbm[b, d][:, gsl(g)], q[b][:, gsl(g)]], axis=0),
                           _block_diag(k[b][:, gsl(g)], G), NT) for b, d, g in groups}
    eg = {key: jnp.exp(x) for key, x in gc.items()}
    glrow = {(b, d): gc[b, d][last[d]:last[d] + 1, :] for b, d in dirs}
    kd = {(b, d): k[b] * jnp.exp(glrow[b, d] - gc[b, d]) for b, d in dirs}
    kdt = {(b, d, g): _bdot(eye128, jnp.concatenate(
        [kd[b, d][:, (g * G + j) * LANES:(g * G + j + 1) * LANES] for j in range(G)], axis=0), NT)
        for b, d, g in groups}
    decay = {(b, d, g): jnp.where(incl[d], jnp.exp(jnp.where(incl[d], gcol[b, d, g] - grow[b, d, g], 0.0)), 0.0)
             for b, d, g in groups}
    nm, vbg, kbeg = {}, {}, {}
    for b, d, g in groups:
        x, dec = kq[b, d, g], decay[b, d, g]
        akd_ref[d, b, 0:C, csl(g)] = jnp.where(incl[d], x[C:2 * C] * dec, 0.0).astype(BF16)
        akd_ref[d, b, C:C + DN_DK, csl(g)] = kdt[b, d, g].astype(BF16)
        nm[b, d, g] = jnp.where(strict[d], -x[0:C] * dec, 0.0)
        vbg[b, d, g] = (v[b] * bet[b, d])[:, gsl(g)]
        kbeg[b, d, g] = (kbm[b, d] * eg[b, d])[:, gsl(g)]
    for b, d in dirs:
        wq_ref[d, (2 * b + 1) * C:(2 * b + 2) * C, :] = (q[b] * eg[b, d]).astype(BF16)
        gl_ref[d, b] = jnp.exp(glrow[b, d])
    tinv = {key: eyecat + x for key, x in nm.items()}
    p = dict(nm)
    pbd = {key: _block_diag(x, G) for key, x in p.items()}
    for _ in range(int(math.log2(C)) - 1):
        p = {key: _bdot(x, pbd[key]) for key, x in p.items()}
        pbd = {key: _block_diag(x, G) for key, x in p.items()}
        tinv = {key: t + _bdot(t, pbd[key]) for key, t in tinv.items()}
    for b, d, g in groups:
        t = tinv[b, d, g]
        u_ref[d, rows(b), gsl(g)] = _bdot(t, _block_diag(vbg[b, d, g], G))
        wq_ref[d, 2 * b * C:(2 * b + 1) * C, gsl(g)] = _bdot(t, _block_diag(kbeg[b, d, g], G)).astype(BF16)


def _dn_local(q, k, v, gb):
    n, d = q.shape
    C = DN_CHUNK
    nc = n // C
    nb = 2 if nc % 2 == 0 else 1
    tok = pl.BlockSpec((nb * C, d), lambda j: (j, 0))
    return pl.pallas_call(
        _dn_local_kernel, grid=(nc // nb,),
        in_specs=[tok, tok, tok, pl.BlockSpec((nb * C, gb.shape[1]), lambda j: (j, 0))],
        out_specs=[pl.BlockSpec((2, nb * C, d), lambda j: (0, j, 0)),
                   pl.BlockSpec((2, nb * 2 * C, d), lambda j: (0, j, 0)),
                   pl.BlockSpec((2, nb, C + DN_DK, DN_HEADS * C), lambda j: (0, j, 0, 0)),
                   pl.BlockSpec((2, nb, 1, d), lambda j: (0, j, 0, 0))],
        out_shape=[jax.ShapeDtypeStruct((2, n, d), F32),
                   jax.ShapeDtypeStruct((2, 2 * n, d), BF16),
                   jax.ShapeDtypeStruct((2, nc, C + DN_DK, DN_HEADS * C), BF16),
                   jax.ShapeDtypeStruct((2, nc, 1, d), F32)],
        compiler_params=_cparams(("parallel",)))(q, k, v, gb)


def _dn_scan_kernel(*refs):
    ins, (of_ref, or_ref, s_ref) = refs[:8], refs[8:]
    C = DN_CHUNK

    @pl.when(pl.program_id(1) == 0)
    def _():
        s_ref[...] = jnp.zeros_like(s_ref)

    chains = [(d, p) for d in range(2) for p in range(DN_HEADS // 2)]
    outs = (of_ref, or_ref)
    lanes = lambda p: slice(2 * p * LANES, (2 * p + 2) * LANES)
    s2 = [jnp.concatenate([s_ref[d, 2 * p], s_ref[d, 2 * p + 1]], axis=-1) for d, p in chains]
    r = [_bdot(ins[4 * d + 1][0, :, lanes(p)], _block_diag(s, 2)) for (d, p), s in zip(chains, s2)]
    vn = [ins[4 * d][0, :, lanes(p)] - x[0:C] for (d, p), x in zip(chains, r)]
    r2 = [_bdot(ins[4 * d + 2][0, 0, :, 2 * p * C:(2 * p + 2) * C], _block_diag(x, 2))
          for (d, p), x in zip(chains, vn)]
    for (d, p), s, x, y in zip(chains, s2, r, r2):
        outs[d][:, lanes(p)] = x[C:2 * C] + y[0:C]
        snew = s * ins[4 * d + 3][0, 0, :, lanes(p)] + y[C:C + DN_DK]
        s_ref[d, 2 * p] = snew[:, 0:LANES]
        s_ref[d, 2 * p + 1] = snew[:, LANES:2 * LANES]


def _dn_scan(B, T, ctx_len, u, wq, akd, gl):
    _, n, d = u.shape
    C = DN_CHUNK
    nch, ncc = T // C, ctx_len // C
    rpos = lambda c: jnp.where(c < ncc, ncc - 1 - c, nch - 1 - (c - ncc))
    pos = (lambda b, c: b * nch + c, lambda b, c: b * nch + rpos(c))
    in_specs, args = [], []
    for dd in range(2):
        p = pos[dd]
        in_specs += [pl.BlockSpec((1, C, d), lambda b, c, p=p, dd=dd: (dd, p(b, c), 0)),
                     pl.BlockSpec((1, 2 * C, d), lambda b, c, p=p, dd=dd: (dd, p(b, c), 0)),
                     pl.BlockSpec((1, 1, C + DN_DK, DN_HEADS * C), lambda b, c, p=p, dd=dd: (dd, p(b, c), 0, 0)),
                     pl.BlockSpec((1, 1, 1, d), lambda b, c, p=p, dd=dd: (dd, p(b, c), 0, 0))]
        args += [u, wq, akd, gl]
    return pl.pallas_call(
        _dn_scan_kernel, grid=(B, nch), in_specs=in_specs,
        out_specs=[pl.BlockSpec((C, d), lambda b, c: (pos[0](b, c), 0)),
                   pl.BlockSpec((C, d), lambda b, c: (pos[1](b, c), 0))],
        out_shape=[jax.ShapeDtypeStruct((n, d), F32)] * 2,
        scratch_shapes=[pltpu.VMEM((2, DN_HEADS, DN_DK, LANES), F32)],
        compiler_params=_cparams(("arbitrary", "arbitrary")))(*args)


def _final_kernel(x_ref, y_ref, mod_ref, o_ref):
    o_ref[0] = x_ref[...] + mod_ref[0, 5:6, :] * y_ref[...]


def _final(B, T, ctx_len, x, y, mod):
    n, d = x.shape
    tm = TOKEN_BLOCK
    nblk, ncb = T // tm, ctx_len // tm
    tok = pl.BlockSpec((tm, d), lambda b, i: (b * nblk + ncb + i, 0))
    ytok = pl.BlockSpec((tm, d), lambda b, i: (b * (nblk - ncb) + i, 0))
    return pl.pallas_call(
        _final_kernel, grid=(B, nblk - ncb),
        in_specs=[tok, ytok, pl.BlockSpec((1, 6, d), lambda b, i: (b, 0, 0))],
        out_specs=pl.BlockSpec((1, tm, d), lambda b, i: (b, i, 0)),
        out_shape=jax.ShapeDtypeStruct((B, T - ctx_len, d), F32),
        compiler_params=_cparams(("parallel", "parallel")))(x, y, mod)


def _rope_tables(seq, ctx_len):
    rows = seq // GRID_W
    r = jnp.broadcast_to(jnp.arange(rows, dtype=F32)[:, None], (rows, GRID_W)).reshape(-1)
    cl = jnp.broadcast_to(jnp.arange(GRID_W, dtype=F32)[None, :], (rows, GRID_W)).reshape(-1)

    def angles(rot_dim):
        nf = rot_dim // 4
        inv = ROPE_BASE ** (-jnp.arange(nf, dtype=F32) / nf)
        ang = jnp.concatenate([r[:, None] * inv, cl[:, None] * inv], axis=-1)
        ang = jnp.concatenate([jnp.zeros((ctx_len, rot_dim // 2), F32), ang], axis=0)
        return jnp.cos(ang), jnp.sin(ang)

    T = seq + ctx_len
    ca, sa = angles(DIFF_DIM)
    z = jnp.zeros_like(sa)
    ta = jnp.stack([jnp.concatenate([ca, ca] * 2, -1), jnp.concatenate([-sa, z] * 2, -1),
                    jnp.concatenate([z, sa] * 2, -1)])
    cb, sb = angles(MLA_ROPE)
    one, zn, zt = jnp.ones((T, MLA_NOPE), F32), jnp.zeros((T, MLA_NOPE), F32), jnp.zeros((T, LANES - MLA_QK), F32)
    zb = jnp.zeros_like(sb)
    tb = jnp.stack([jnp.concatenate([one, cb, cb, 1.0 + zt], -1), jnp.concatenate([zn, -sb, zb, zt], -1),
                    jnp.concatenate([zn, zb, sb, zt], -1)])
    return ta, tb


def _pad_heads(w, heads, width):
    lead = w.shape[:-1]
    w = w.reshape(lead + (heads, width))
    return jnp.pad(w, [(0, 0)] * len(lead) + [(0, 0), (0, LANES - width)]).reshape(lead + (heads * LANES,))


def _block_diag_ones(n, group):
    idx = np.arange(n) // group
    return jnp.asarray(idx[:, None] == idx[None, :], dtype=BF16)


def kernel(x, c, ctx, c_ctx, ada_w, ada_b, ev_w_in, ev_w_out, diff_q_norm, diff_k_norm, diff_lam_q1, diff_lam_k1, diff_lam_q2, diff_lam_k2, diff_subln, mla_cq_norm, mla_ckv_norm, mla_w_uq, mla_w_ukv, mla_q_norm, mla_k_norm, dn_w_in, dn_conv, dn_a_log, dn_dt_bias, dn_o_norm, dn_w_out, peer_wq, peer_k1, peer_k2, peer_u, peer_v):
    B, S, D = x.shape
    CTX = ctx.shape[1]
    T = CTX + S
    depth = ada_w.shape[0]
    tm = TOKEN_BLOCK
    assert CTX % tm == 0 and S % tm == 0 and B + 1 <= SUBLANES
    assert (B * T) % PEER_TOKEN_BLOCK == 0 and (B * S) % PEER_TOKEN_BLOCK == 0
    nblk, ncb = T // tm, CTX // tm
    geo = (B, nblk, tm, lambda b, i: jnp.where(i < ncb, B, b))

    cc = jnp.zeros((SUBLANES, D), F32).at[:B].set(c).at[B].set(c_ctx)
    mods = _adaln(cc, ada_w, ada_b).reshape(depth, SUBLANES, 6, D)
    xs = jnp.concatenate([ctx, x], axis=1).reshape(B * T, D)
    ropea, ropeb = _rope_tables(S, CTX)
    bd64, bd128 = _block_diag_ones(HEAD_W, DIFF_DIM), _block_diag_ones(HEAD_W, LANES)

    y = None
    for l in range(depth):
        i = l // 2
        mod = mods[l]
        gmod = mods[l - 1] if l else None
        n_skip = ncb if l == depth - 1 else 0
        if l % 2 == 0:
            lam_init = 0.8 - 0.6 * math.exp(-0.3 * l)
            w_in = ev_w_in[i]
            o = 3 * HEAD_W + MLA_Q_RANK + MLA_KV_RANK
            kr_rep = jnp.pad(jnp.broadcast_to(w_in[:, None, o:o + MLA_ROPE], (D, MLA_HEADS, MLA_ROPE)),
                             ((0, 0), (0, 0), (MLA_NOPE, LANES - MLA_QK))).reshape(D, HEAD_W)
            w_cat = jnp.concatenate([w_in[:, :o], kr_rep], axis=1).astype(BF16)
            res = _modmm(geo, xs, mod, w_cat, y, gmod)
            (xs, z) = res if y is not None else (xs, res)
            ukv = mla_w_ukv[i].reshape(MLA_KV_RANK, MLA_HEADS, MLA_NOPE + MLA_VDIM)
            gains = jnp.stack([jnp.tile(diff_q_norm[i], HEAD_W // DIFF_DIM), jnp.tile(diff_k_norm[i], HEAD_W // DIFF_DIM),
                               jnp.tile(jnp.pad(mla_q_norm[i], (0, LANES - MLA_QK)), MLA_HEADS),
                               jnp.tile(jnp.pad(mla_k_norm[i], (0, LANES - MLA_QK)), MLA_HEADS)])
            qa, ka, va, qb, kb, vb = _even_prep(
                geo, z, ropea, ropeb, gains, mla_cq_norm[i][None, :], mla_ckv_norm[i][None, :],
                _pad_heads(mla_w_uq[i], MLA_HEADS, MLA_QK).astype(BF16),
                _pad_heads(ukv[:, :, :MLA_NOPE].reshape(MLA_KV_RANK, -1), MLA_HEADS, MLA_NOPE).astype(BF16),
                ukv[:, :, MLA_NOPE:].reshape(MLA_KV_RANK, -1).astype(BF16), bd64, bd128)
            lam_vecs = jnp.stack([diff_lam_q1[i], diff_lam_k1[i], diff_lam_q2[i], diff_lam_k2[i]])
            oa = _attention(B, T, CTX, qa, ka, va, lam_vecs, diff_subln[i][None, :], lam_init)
            ob = _attention(B, T, CTX, qb, kb, vb)
            xs, hqt = _mixer_out(geo, _even_out_kernel, [(oa, HEAD_W, 0), (ob, HEAD_W, 0)], [], xs, mod,
                                 ev_w_out[i].astype(BF16), n_skip)
        else:
            nin = dn_w_in.shape[2]
            w_in = jnp.pad(dn_w_in[i], ((0, 0), (0, -nin % LANES))).astype(BF16)
            res = _modmm(geo, xs, mod, w_in, y, gmod)
            (xs, z) = res if y is not None else (xs, res)
            q, k, v, gb = _dn_prep(geo, ncb, z, dn_conv[i], dn_a_log[i].reshape(1, -1),
                                   dn_dt_bias[i].reshape(1, -1))
            of, orv = _dn_scan(B, T, CTX, *_dn_local(q, k, v, gb))
            xs, hqt = _mixer_out(geo, _dn_out_kernel,
                                 [(of, D, 0), (orv, D, 0), (z, D, DN_QKV // D)], [dn_o_norm[i][None, :]], xs, mod,
                                 dn_w_out[i].astype(BF16), n_skip)
        y = _peer(hqt, B * (nblk - n_skip) * tm, peer_wq[l], peer_k1[l], peer_k2[l], peer_u[l], peer_v[l])
    return _final(B, T, CTX, xs, y, mods[depth - 1])
```

```python
import functools
import math

import jax
import jax.numpy as jnp
import numpy as np
from jax import lax
from jax.experimental import pallas as pl
from jax.experimental.pallas import tpu as pltpu

F32 = jnp.float32
BF16 = jnp.bfloat16
HI = lax.Precision.HIGHEST
NT = (((1,), (1,)), ((), ()))

EPS = 1e-6
LOG2E = math.log2(math.e)
ROPE_BASE = 10000.0
GRID_W = 64
LANES, SUBLANES = 128, 8
MXU_TILE = 256
TOKEN_BLOCK = 256
VMEM_LIMIT = 56 * 1024 * 1024

DIFF_HEADS, DIFF_DIM = 4, 64
MLA_HEADS, MLA_Q_RANK, MLA_KV_RANK, MLA_NOPE, MLA_ROPE, MLA_VDIM = 4, 256, 128, 64, 32, 128
MLA_QK = MLA_NOPE + MLA_ROPE
HEAD_W = DIFF_HEADS * LANES
DN_HEADS, DN_DK, DN_CONV, DN_CHUNK = 8, 128, 5, 64
DN_QKV = 3 * DN_HEADS * DN_DK
PEER_HEADS, PEER_NKEYS, PEER_TOPK = 8, 128, 16
PEER_CAND_ROWS = 80
PEER_TOKEN_BLOCK = 512
PEER_EXPERT_BLOCK = 2048
PEER_PIECE = MXU_TILE


def _cparams(sem):
    return pltpu.CompilerParams(dimension_semantics=sem, vmem_limit_bytes=VMEM_LIMIT)


def _rms_rows(x):
    return x * lax.rsqrt(jnp.mean(x * x, axis=-1, keepdims=True) + EPS)


def _silu(x):
    return x * jax.nn.sigmoid(x)


def _group_sum(sq, bd):
    hi = sq.astype(BF16)
    lo = (sq - hi.astype(F32)).astype(BF16)
    return jnp.dot(hi, bd, preferred_element_type=F32) + jnp.dot(lo, bd, preferred_element_type=F32)


def _adaln_kernel(c_ref, w_ref, b_ref, o_ref):
    s = _silu(c_ref[...])
    o_ref[0] = jnp.dot(s, w_ref[0], precision=HI, preferred_element_type=F32) + b_ref[0]


def _adaln(cc, ada_w, ada_b):
    depth, d, n = ada_w.shape
    rows = cc.shape[0]
    tn = n // 4
    return pl.pallas_call(
        _adaln_kernel,
        grid=(depth, n // tn),
        in_specs=[pl.BlockSpec((rows, d), lambda l, j: (0, 0)),
                  pl.BlockSpec((1, d, tn), lambda l, j: (l, 0, j)),
                  pl.BlockSpec((1, 1, tn), lambda l, j: (l, 0, j))],
        out_specs=pl.BlockSpec((1, rows, tn), lambda l, j: (l, 0, j)),
        out_shape=jax.ShapeDtypeStruct((depth, rows, n), F32),
        compiler_params=_cparams(("arbitrary", "arbitrary")),
    )(cc, ada_w, ada_b.reshape(depth, 1, n))


def _modmm_kernel(*refs, has_y):
    if has_y:
        x_ref, y_ref, gmod_ref, mod_ref, w_ref, xo_ref, z_ref = refs
        x = x_ref[...] + gmod_ref[0, 5:6, :] * y_ref[...]
        xo_ref[...] = x
    else:
        x_ref, mod_ref, w_ref, z_ref = refs
        x = x_ref[...]
    h = _rms_rows(x) * (1.0 + mod_ref[0, 1:2, :]) + mod_ref[0, 0:1, :]
    z_ref[...] = jnp.dot(h.astype(BF16), w_ref[...], preferred_element_type=F32)


def _modmm(geo, x, mod, w, y=None, gmod=None):
    B, nblk, tm, mrow = geo
    n, d = x.shape
    nout = w.shape[1]
    tok = pl.BlockSpec((tm, d), lambda b, i: (b * nblk + i, 0))
    modspec = pl.BlockSpec((1, 6, d), lambda b, i: (mrow(b, i), 0, 0))
    wspec = pl.BlockSpec((d, nout), lambda b, i: (0, 0))
    zspec = pl.BlockSpec((tm, nout), lambda b, i: (b * nblk + i, 0))
    zshape = jax.ShapeDtypeStruct((n, nout), F32)
    if y is None:
        return pl.pallas_call(
            functools.partial(_modmm_kernel, has_y=False), grid=(B, nblk),
            in_specs=[tok, modspec, wspec], out_specs=zspec, out_shape=zshape,
            compiler_params=_cparams(("parallel", "parallel")))(x, mod, w)
    return pl.pallas_call(
        functools.partial(_modmm_kernel, has_y=True), grid=(B, nblk),
        in_specs=[tok, tok, modspec, modspec, wspec], out_specs=[tok, zspec],
        out_shape=[jax.ShapeDtypeStruct((n, d), F32), zshape],
        compiler_params=_cparams(("parallel", "parallel")))(x, y, gmod, mod, w)


def _rope_lanes(v, tab_ref, half):
    rep = v.shape[-1] // LANES
    c = jnp.concatenate([tab_ref[0]] * rep, axis=-1)
    sm = jnp.concatenate([tab_ref[1]] * rep, axis=-1)
    sp = jnp.concatenate([tab_ref[2]] * rep, axis=-1)
    n = v.shape[-1]
    return v * c + pltpu.roll(v, n - half, 1) * sm + pltpu.roll(v, half, 1) * sp


def _even_prep_kernel(z_ref, ropea_ref, ropeb_ref, gains_ref, cqn_ref, ckvn_ref, wuq_ref, wuk_ref, wuv_ref,
                      bd64_ref, bd128_ref, qa_ref, ka_ref, va_ref, qb_ref, kb_ref, vb_ref):
    W = HEAD_W
    bd64 = bd64_ref[...]
    bd128 = bd128_ref[...]

    def norm_groups(v, bd, width, gain):
        ms = _group_sum(v * v, bd) * (1.0 / width)
        return v * lax.rsqrt(ms + EPS) * gain

    qa = norm_groups(z_ref[:, 0:W], bd64, DIFF_DIM, gains_ref[0:1, :])
    ka = norm_groups(z_ref[:, W:2 * W], bd64, DIFF_DIM, gains_ref[1:2, :])
    qa_ref[...] = (_rope_lanes(qa, ropea_ref, DIFF_DIM // 2) * (DIFF_DIM ** -0.5 * LOG2E)).astype(BF16)
    ka_ref[...] = _rope_lanes(ka, ropea_ref, DIFF_DIM // 2).astype(BF16)
    va_ref[...] = z_ref[:, 2 * W:3 * W].astype(BF16)

    o = 3 * W
    cq = (_rms_rows(z_ref[:, o:o + MLA_Q_RANK]) * cqn_ref[...]).astype(BF16)
    o += MLA_Q_RANK
    ckv = (_rms_rows(z_ref[:, o:o + MLA_KV_RANK]) * ckvn_ref[...]).astype(BF16)
    o += MLA_KV_RANK
    kr = z_ref[:, o:o + W]
    qb = jnp.dot(cq, wuq_ref[...], preferred_element_type=F32)
    kb = jnp.dot(ckv, wuk_ref[...], preferred_element_type=F32) + kr
    qb = norm_groups(qb, bd128, MLA_QK, gains_ref[2:3, :])
    kb = norm_groups(kb, bd128, MLA_QK, gains_ref[3:4, :])
    qb_ref[...] = (_rope_lanes(qb, ropeb_ref, MLA_ROPE // 2) * (MLA_QK ** -0.5 * LOG2E)).astype(BF16)
    kb_ref[...] = _rope_lanes(kb, ropeb_ref, MLA_ROPE // 2).astype(BF16)
    vb_ref[...] = jnp.dot(ckv, wuv_ref[...], preferred_element_type=F32).astype(BF16)


def _even_prep(geo, z, ropea, ropeb, gains, cqn, ckvn, wuq, wuk, wuv, bd64, bd128):
    B, nblk, tm, _ = geo
    n = z.shape[0]
    W = HEAD_W
    full = lambda a: pl.BlockSpec(a.shape, lambda b, i: (0,) * a.ndim)
    rope = pl.BlockSpec((3, tm, LANES), lambda b, i: (0, i, 0))
    out = pl.BlockSpec((tm, W), lambda b, i: (b * nblk + i, 0))
    return pl.pallas_call(
        _even_prep_kernel, grid=(B, nblk),
        in_specs=[pl.BlockSpec((tm, z.shape[1]), lambda b, i: (b * nblk + i, 0)), rope, rope, full(gains),
                  full(cqn), full(ckvn), full(wuq), full(wuk), full(wuv), full(bd64), full(bd128)],
        out_specs=[out] * 6, out_shape=[jax.ShapeDtypeStruct((n, W), BF16)] * 6,
        compiler_params=_cparams(("parallel", "parallel")))(z, ropea, ropeb, gains, cqn, ckvn, wuq, wuk, wuv,
                                                            bd64, bd128)


def _attn_kernel(*refs, diff, tk, row_split, n_ctx_q, ctx_len, n_keys, lam_init):
    if diff:
        q_ref, k_ref, v_ref, lam_ref, subln_ref, o_ref, s_ref = refs
    else:
        q_ref, k_ref, v_ref, o_ref, s_ref = refs
    qi = pl.program_id(2)
    tr = q_ref.shape[0] // row_split
    qs = []
    for r in range(row_split):
        q = q_ref[r * tr:(r + 1) * tr, :]
        if diff:
            lane = lax.broadcasted_iota(jnp.int32, q.shape, 1)
            zero = jnp.zeros_like(q)
            qs += [jnp.where(lane < DIFF_DIM, q, zero), jnp.where(lane >= DIFF_DIM, q, zero)]
        else:
            qs.append(q)
    nsub = len(qs) // row_split

    def scores(slot, start, size):
        k = k_ref[pl.ds(start, size), :]
        s = [lax.dot_general(qq, k, NT, preferred_element_type=F32) for qq in qs]
        for j, x in enumerate(s):
            s_ref[slot, j, :, 0:size] = x
        return tuple(jnp.max(x, axis=-1, keepdims=True) for x in s)

    def update(slot, start, size, mx, carry):
        v = v_ref[pl.ds(start, size), :]
        mn = [jnp.maximum(m, x) for (m, _, _), x in zip(carry, mx)]
        p = [jnp.exp2(s_ref[slot, j, :, 0:size] - y) for j, y in enumerate(mn)]
        alpha = [jnp.exp2(m - y) for (m, _, _), y in zip(carry, mn)]
        pv = [jnp.dot(x.astype(BF16), v, preferred_element_type=F32) for x in p]
        return tuple((y, a * l + jnp.sum(x, axis=-1, keepdims=True), a * acc + z)
                     for (_, l, acc), y, a, x, z in zip(carry, mn, alpha, p, pv))

    init = tuple((jnp.full((tr, 1), -1e30, F32), jnp.zeros((tr, 1), F32), jnp.zeros((tr, LANES), F32))
                 for _ in qs)

    def finish(carry):
        outs = [acc / l for (_, l, acc) in carry]
        for r in range(row_split):
            if diff:
                lam = (jnp.exp(jnp.sum(lam_ref[0:1, :] * lam_ref[1:2, :], axis=-1, keepdims=True))
                       - jnp.exp(jnp.sum(lam_ref[2:3, :] * lam_ref[3:4, :], axis=-1, keepdims=True)) + lam_init)
                o = _rms_rows(outs[r * nsub] - lam * outs[r * nsub + 1]) * subln_ref[...] * (1.0 - lam_init)
            else:
                o = outs[r]
            o_ref[r * tr:(r + 1) * tr, :] = o.astype(o_ref.dtype)

    @pl.when(qi < n_ctx_q)
    def _():
        finish(update(0, 0, ctx_len, scores(0, 0, ctx_len), init))

    @pl.when(qi >= n_ctx_q)
    def _():
        n = n_keys // tk
        at = lambda c: pl.multiple_of(c * tk, tk)

        def pair(j, state):
            carry, mx0 = state
            mx1 = scores(1, at(2 * j + 1), tk)
            carry = update(0, at(2 * j), tk, mx0, carry)
            mx0 = scores(0, at(2 * j + 2), tk)
            carry = update(1, at(2 * j + 1), tk, mx1, carry)
            return carry, mx0

        carry, mx0 = lax.fori_loop(0, (n - 1) // 2, pair, (init, scores(0, 0, tk)))
        if n % 2 == 0:
            mx1 = scores(1, (n - 1) * tk, tk)
            carry = update(0, (n - 2) * tk, tk, mx0, carry)
            carry = update(1, (n - 1) * tk, tk, mx1, carry)
        else:
            carry = update(0, (n - 1) * tk, tk, mx0, carry)
        finish(carry)


def _attention(B, T, ctx_len, q, k, v, lam_vecs=None, subln=None, lam_init=0.0):
    diff = lam_vecs is not None
    tq = TOKEN_BLOCK
    tk = next(c for c in (1408, 768, TOKEN_BLOCK) if T % c == 0)
    nq = T // tq
    heads = q.shape[1] // LANES
    qspec = pl.BlockSpec((tq, LANES), lambda b, h, i: (b * nq + i, h))
    kvspec = pl.BlockSpec((T, LANES), lambda b, h, i: (b, h))
    in_specs = [qspec, kvspec, kvspec]
    args = [q, k, v]
    if diff:
        in_specs += [pl.BlockSpec(lam_vecs.shape, lambda b, h, i: (0, 0)),
                     pl.BlockSpec(subln.shape, lambda b, h, i: (0, 0))]
        args += [lam_vecs, subln]
    row_split = 1 if diff else 2
    chains = row_split * (2 if diff else 1)
    return pl.pallas_call(
        functools.partial(_attn_kernel, diff=diff, tk=tk, row_split=row_split, n_ctx_q=ctx_len // tq,
                          ctx_len=ctx_len, n_keys=T, lam_init=lam_init),
        grid=(B, heads, nq), in_specs=in_specs, out_specs=qspec,
        out_shape=jax.ShapeDtypeStruct(q.shape, BF16),
        scratch_shapes=[pltpu.VMEM((2, chains, tq // row_split, max(tk, ctx_len)), F32)],
        compiler_params=_cparams(("parallel", "parallel", "arbitrary")))(*args)


def _residual_tail(x, y, mod_ref, xo_ref, hqt_ref):
    xn = x + mod_ref[0, 2:3, :] * y
    xo_ref[...] = xn
    hq = _rms_rows(xn) * (1.0 + mod_ref[0, 4:5, :]) + mod_ref[0, 3:4, :]
    hqt_ref[...] = hq.T.astype(BF16)


def _even_out_kernel(oa_ref, ob_ref, x_ref, mod_ref, wo_ref, xo_ref, hqt_ref):
    W = HEAD_W
    y = (jnp.dot(oa_ref[...], wo_ref[0:W, :], preferred_element_type=F32)
         + jnp.dot(ob_ref[...], wo_ref[W:2 * W, :], preferred_element_type=F32))
    _residual_tail(x_ref[...], y, mod_ref, xo_ref, hqt_ref)


def _dn_out_kernel(of_ref, or_ref, zg_ref, onorm_ref, x_ref, mod_ref, wo_ref, xo_ref, hqt_ref):
    parts = []
    for h in range(DN_HEADS):
        hs = slice(h * LANES, (h + 1) * LANES)
        o = of_ref[:, hs] + or_ref[:, hs]
        parts.append((_rms_rows(o) * onorm_ref[...] * _silu(zg_ref[:, hs])).astype(BF16))
    y = jnp.dot(jnp.concatenate(parts, axis=-1), wo_ref[...], preferred_element_type=F32)
    _residual_tail(x_ref[...], y, mod_ref, xo_ref, hqt_ref)


def _mixer_out(geo, kernel_fn, token_args, small_args, x, mod, wo, n_skip=0):
    B, nblk, tm, mrow = geo
    n, d = x.shape
    keep = nblk - n_skip
    hq_col = (lambda b, i: b * nblk + i) if n_skip == 0 else (
        lambda b, i: jnp.where(i < n_skip, B * keep + b * n_skip + i, b * keep + i - n_skip))
    n_hq = n
    tokspec = lambda a, col: pl.BlockSpec((tm, a[1]), lambda b, i: (b * nblk + i, col))
    in_specs = [tokspec((a, w), col) for (a, w, col) in token_args]
    in_specs += [pl.BlockSpec(a.shape, lambda b, i: (0,) * a.ndim) for a in small_args]
    in_specs += [pl.BlockSpec((tm, d), lambda b, i: (b * nblk + i, 0)),
                 pl.BlockSpec((1, 6, d), lambda b, i: (mrow(b, i), 0, 0)),
                 pl.BlockSpec(wo.shape, lambda b, i: (0, 0))]
    return pl.pallas_call(
        kernel_fn, grid=(B, nblk), in_specs=in_specs,
        out_specs=[pl.BlockSpec((tm, d), lambda b, i: (b * nblk + i, 0)),
                   pl.BlockSpec((d, tm), lambda b, i: (0, hq_col(b, i)))],
        out_shape=[jax.ShapeDtypeStruct((n, d), F32), jax.ShapeDtypeStruct((d, n_hq), BF16)],
        compiler_params=_cparams(("parallel", "parallel")))(
            *[a for (a, _, _) in token_args], *small_args, x, mod, wo)


def _peer_select_kernel(hqt_ref, wqt_ref, k1_ref, k2_ref, th_ref, e1_ref, s2_ref, e2_ref, work_ref, top_ref, cand_ref):
    K, H = PEER_TOPK, PEER_HEADS
    qt = jnp.dot(wqt_ref[...], hqt_ref[...], preferred_element_type=F32)
    dk = k1_ref.shape[1]
    for h in range(H):
        q1 = qt[(2 * h) * dk:(2 * h + 1) * dk, :].astype(BF16)
        q2 = qt[(2 * h + 1) * dk:(2 * h + 2) * dk, :].astype(BF16)
        s1 = jnp.dot(k1_ref[...], q1, preferred_element_type=F32)
        s2 = jnp.dot(k2_ref[...], q2, preferred_element_type=F32)
        th_ref[h] = s1
        s2_ref[h] = s2
        work_ref[2 * h] = s1
        work_ref[2 * h + 1] = s2

    def extract(ref, count):
        s = [ref[a] for a in range(count)]
        m = [jnp.max(x, axis=0, keepdims=True) for x in s]
        for a in range(count):
            ref[a] = jnp.where(s[a] == m[a], -jnp.inf, s[a])
        return m

    def top_round(r, carry):
        for a, m in enumerate(extract(work_ref, 2 * H)):
            top_ref[a, pl.ds(r, 1), :] = m
        return carry

    lax.fori_loop(0, K, top_round, 0)

    def candidates(h):
        v1, v2 = top_ref[2 * h], top_ref[2 * h + 1]
        row8 = lax.broadcasted_iota(jnp.int32, (SUBLANES, v1.shape[1]), 0)
        pieces = [v1[0:1, :] + v2, v1[1:2, :] + v2[0:8, :]]
        pieces += [jnp.where(row8 < K // (r1 + 1), v1[r1:r1 + 1, :] + v2[0:8, :], -jnp.inf) for r1 in range(2, 8)]
        pieces.append(v1[8:16, :] + v2[0:1, :])
        return jnp.concatenate(pieces, axis=0)

    for h in range(H):
        cand_ref[h] = candidates(h)
    tau = lax.fori_loop(0, K, lambda r, carry: tuple(extract(cand_ref, H)),
                        tuple(jnp.zeros((1, hqt_ref.shape[1]), F32) for _ in range(H)))
    for h in range(H):
        cand = candidates(h)
        m1, m2 = top_ref[2 * h, 0:1, :], top_ref[2 * h + 1, 0:1, :]
        zsum = jnp.sum(jnp.where(cand >= tau[h], jnp.exp(cand - (m1 + m2)), 0.0), axis=0, keepdims=True)
        s1 = th_ref[h]
        e1_ref[h] = jnp.exp(s1 - m1) / zsum
        e2_ref[h] = jnp.exp(s2_ref[h] - m2)
        v2 = top_ref[2 * h + 1]
        theta = jnp.full(s1.shape, jnp.inf, F32)
        for r in range(K):
            theta = jnp.where((s1 + v2[r:r + 1, :]) >= tau[h], v2[r:r + 1, :], theta)
        th_ref[h] = theta


def _peer_select(hqt, n, wqt, k1, k2):
    d = hqt.shape[0]
    tm = PEER_TOKEN_BLOCK
    H, NK = PEER_HEADS, PEER_NKEYS
    big = pl.BlockSpec((H, NK, tm), lambda j: (0, 0, j))
    bigshape = jax.ShapeDtypeStruct((H, NK, n), F32)
    full = lambda a: pl.BlockSpec(a.shape, lambda j: (0,) * a.ndim)
    return pl.pallas_call(
        _peer_select_kernel, grid=(n // tm,),
        in_specs=[pl.BlockSpec((d, tm), lambda j: (0, j)), full(wqt), full(k1), full(k2)],
        out_specs=[big] * 4, out_shape=[bigshape] * 4,
        scratch_shapes=[pltpu.VMEM((2 * H, NK, tm), F32), pltpu.VMEM((2 * H, PEER_TOPK, tm), F32),
                        pltpu.VMEM((H, PEER_CAND_ROWS, tm), F32)],
        compiler_params=_cparams(("parallel",)))(hqt, wqt, k1, k2)


def _gelu(a):
    return 0.5 * a * (1.0 + lax.erf(a * np.float32(math.sqrt(0.5))))


def _peer_dense_kernel(hqt_ref, u_ref, vt_ref, th_ref, e1_ref, s2_ref, e2_ref, zero_ref, y_ref, row_ref,
                       *piece_refs):
    c = pl.program_id(1)
    NK = PEER_NKEYS
    n_i = PEER_EXPERT_BLOCK // NK
    pieces = [piece_refs[3 * p:3 * p + 3] for p in range(len(piece_refs) // 3)]

    @pl.when(c == 0)
    def _():
        for acc_ref, _, _ in pieces:
            acc_ref[...] = jnp.zeros_like(acc_ref)

    H = PEER_HEADS
    strips = [slice(ts * LANES, (ts + 1) * LANES) for ts in range(hqt_ref.shape[1] // LANES)]
    for ii in range(n_i):
        for h in range(H):
            i = c * n_i + ii
            row_ref[0, ii, h:h + 1, :] = th_ref[h, pl.ds(i, 1), :]
            row_ref[1, ii, h:h + 1, :] = e1_ref[h, pl.ds(i, 1), :]

    JB, RUNS, PIECE = 16, 8, PEER_PIECE
    zero = zero_ref[0:JB, :]
    for p, (_, act_ref, _) in enumerate(pieces):
        pc = slice(p * PIECE, (p + 1) * PIECE)
        act_ref[...] = _gelu(jnp.dot(u_ref[...], hqt_ref[:, pc], preferred_element_type=F32))
    for p, (acc_ref, act_ref, wt_ref) in enumerate(pieces):
        for ts in range(PIECE // LANES):
            tl = strips[p * PIECE // LANES + ts]
            pl_ = strips[ts]
            for i0 in range(0, n_i, RUNS):
                for jb in range(NK // JB):
                    js = slice(jb * JB, (jb + 1) * JB)
                    g = [jnp.zeros((JB, LANES), F32) for _ in range(RUNS)]
                    for h in range(H):
                        s2 = s2_ref[h, js, tl] + zero
                        e2 = e2_ref[h, js, tl] + zero
                        for r in range(RUNS):
                            sel = s2 >= row_ref[0, i0 + r, h:h + 1, tl]
                            g[r] = g[r] + jnp.where(sel, row_ref[1, i0 + r, h:h + 1, tl] * e2, 0.0)
                    for r in range(RUNS):
                        rows = slice((i0 + r) * NK + jb * JB, (i0 + r) * NK + (jb + 1) * JB)
                        wt_ref[rows, pl_] = (g[r] * act_ref[rows, pl_]).astype(BF16)
        acc_ref[...] += jnp.dot(vt_ref[...], wt_ref[...], preferred_element_type=F32)

    @pl.when(c == pl.num_programs(1) - 1)
    def _():
        for p, (acc_ref, _, _) in enumerate(pieces):
            y_ref[p * PIECE:(p + 1) * PIECE, :] = acc_ref[...].T


def _peer_dense(hqt, n, u, vt, theta, e1, s2, e2):
    d = hqt.shape[0]
    tm, te = PEER_TOKEN_BLOCK, PEER_EXPERT_BLOCK
    H, NK = PEER_HEADS, PEER_NKEYS
    big = pl.BlockSpec((H, NK, tm), lambda j, c: (0, 0, j))
    return pl.pallas_call(
        _peer_dense_kernel, grid=(n // tm, u.shape[0] // te),
        in_specs=[pl.BlockSpec((d, tm), lambda j, c: (0, j)),
                  pl.BlockSpec((te, d), lambda j, c: (c, 0)),
                  pl.BlockSpec((d, te), lambda j, c: (0, c)),
                  big, big, big, big, pl.BlockSpec((NK, LANES), lambda j, c: (0, 0))],
        out_specs=pl.BlockSpec((tm, d), lambda j, c: (j, 0)),
        out_shape=jax.ShapeDtypeStruct((n, d), F32),
        scratch_shapes=[pltpu.VMEM((2, te // NK, H, tm), F32)]
        + [pltpu.VMEM((d, PEER_PIECE), F32), pltpu.VMEM((te, PEER_PIECE), F32),
           pltpu.VMEM((te, PEER_PIECE), BF16)] * (tm // PEER_PIECE),
        compiler_params=_cparams(("parallel", "arbitrary")))(hqt, u, vt, theta, e1, s2, e2,
                                                             jnp.zeros((NK, LANES), F32))


def _peer(hqt, n, wq, k1, k2, u_tab, v_tab):
    sel = _peer_select(hqt, n, wq.T.astype(BF16), k1.astype(BF16), k2.astype(BF16))
    return _peer_dense(hqt, n, u_tab.astype(BF16), v_tab.T.astype(BF16), *sel)


def _dn_prep_kernel(z_ref, prev_ref, next_ref, conv_ref, alog_ref, dtb_ref, ab_ref, q_ref, k_ref, v_ref, gb_ref,
                    ext_ref, *, n_ctx_blk, nblk):
    i = pl.program_id(1)
    tm = z_ref.shape[0]
    halo = prev_ref.shape[0]
    pad = DN_CONV // 2
    has_prev = jnp.logical_and(i != 0, i != n_ctx_blk)
    has_next = jnp.logical_and(i != n_ctx_blk - 1, i != nblk - 1)
    ext_ref[0:halo, :] = jnp.where(has_prev, prev_ref[...], 0.0)
    ext_ref[halo:halo + tm, :] = z_ref[...]
    ext_ref[halo + tm:, :] = jnp.where(has_next, next_ref[...], 0.0)
    nq = DN_HEADS * DN_DK
    for j in range(DN_QKV // LANES):
        cs = slice(j * LANES, (j + 1) * LANES)
        acc = conv_ref[0:1, cs] * ext_ref[halo - pad:halo - pad + tm, cs]
        for t in range(1, DN_CONV):
            acc = acc + conv_ref[t:t + 1, cs] * ext_ref[halo - pad + t:halo - pad + t + tm, cs]
        y = _silu(acc)
        if j * LANES < 2 * nq:
            y = y * lax.rsqrt(jnp.sum(y * y, axis=-1, keepdims=True) + EPS)
        if j * LANES < nq:
            q_ref[:, cs] = (y * DN_DK ** -0.5).astype(q_ref.dtype)
        elif j * LANES < 2 * nq:
            k_ref[:, slice(j * LANES - nq, (j + 1) * LANES - nq)] = y.astype(k_ref.dtype)
        else:
            v_ref[:, slice(j * LANES - 2 * nq, (j + 1) * LANES - 2 * nq)] = y.astype(v_ref.dtype)
    ab = ab_ref[...]
    nh = 2 * DN_HEADS
    xa = ab[:, 0:nh] + dtb_ref[...]
    softplus = jnp.maximum(xa, 0.0) + jnp.log(1.0 + jnp.exp(-jnp.abs(xa)))
    gb_ref[:, 0:nh] = -jnp.exp(alog_ref[...]) * softplus
    gb_ref[:, nh:2 * nh] = jax.nn.sigmoid(ab[:, nh:2 * nh])


def _dn_prep(geo, n_ctx_blk, z, conv_w, alog, dtb):
    B, nblk, tm, _ = geo
    n = z.shape[0]
    halo = SUBLANES
    r = tm // halo
    nh8 = n // halo
    d = DN_HEADS * DN_DK
    prev = pl.BlockSpec((halo, DN_QKV), lambda b, i: (jnp.maximum((b * nblk + i) * r - 1, 0), 0))
    nxt = pl.BlockSpec((halo, DN_QKV), lambda b, i: (jnp.minimum((b * nblk + i + 1) * r, nh8 - 1), 0))
    ab = z[:, DN_QKV + d:DN_QKV + d + 4 * DN_HEADS]
    tok = pl.BlockSpec((tm, d), lambda b, i: (b * nblk + i, 0))
    gbspec = pl.BlockSpec((tm, 4 * DN_HEADS), lambda b, i: (b * nblk + i, 0))
    full = lambda a: pl.BlockSpec(a.shape, lambda b, i: (0,) * a.ndim)
    return pl.pallas_call(
        functools.partial(_dn_prep_kernel, n_ctx_blk=n_ctx_blk, nblk=nblk), grid=(B, nblk),
        in_specs=[pl.BlockSpec((tm, DN_QKV), lambda b, i: (b * nblk + i, 0)), prev, nxt, full(conv_w),
                  full(alog), full(dtb), gbspec],
        out_specs=[tok, tok, tok, gbspec],
        out_shape=[jax.ShapeDtypeStruct((n, d), BF16)] * 3 + [jax.ShapeDtypeStruct((n, 4 * DN_HEADS), F32)],
        scratch_shapes=[pltpu.VMEM((tm + 2 * halo, DN_QKV), F32)],
        compiler_params=_cparams(("parallel", "parallel")))(z, z, z, conv_w, alog, dtb, ab)


DN_GROUP = 4


def _bdot(a, b, dims=None):
    a, b = a.astype(BF16), b.astype(BF16)
    if dims is None:
        return jnp.dot(a, b, preferred_element_type=F32)
    return lax.dot_general(a, b, dims, preferred_element_type=F32)


def _block_diag(x, nblk):
    r, n = x.shape
    w = n // nblk
    tall = jnp.concatenate([x] * nblk, axis=0)
    rb = lax.broadcasted_iota(jnp.int32, tall.shape, 0) // r
    lb = lax.broadcasted_iota(jnp.int32, tall.shape, 1) // w
    return jnp.where(rb == lb, tall, jnp.zeros_like(tall))


def _dn_local_kernel(q_ref, k_ref, v_ref, gb_ref, u_ref, wq_ref, akd_ref, gl_ref):
    C, G, nh = DN_CHUNK, DN_GROUP, DN_HEADS
    d_all = nh * LANES
    ri = lax.broadcasted_iota(jnp.int32, (C, G * C), 0)
    ci = lax.broadcasted_iota(jnp.int32, (C, G * C), 1) % C
    eyecat = (ri == ci).astype(F32)
    ones = jnp.ones((C, C), F32)
    r2 = lax.broadcasted_iota(jnp.int32, (C, C), 0)
    c2 = lax.broadcasted_iota(jnp.int32, (C, C), 1)
    eye128 = (lax.broadcasted_iota(jnp.int32, (LANES, LANES), 0)
              == lax.broadcasted_iota(jnp.int32, (LANES, LANES), 1)).astype(BF16)
    ncol = gb_ref.shape[1]
    sel_row = lax.broadcasted_iota(jnp.int32, (ncol, d_all), 0)
    sel_head = lax.broadcasted_iota(jnp.int32, (ncol, d_all), 1) // LANES
    cat_row = lax.broadcasted_iota(jnp.int32, (ncol, G * C), 0)
    cat_head = lax.broadcasted_iota(jnp.int32, (ncol, G * C), 1) // C
    hdot = lambda a, b: jnp.dot(a, b, precision=HI, preferred_element_type=F32)
    nblk = q_ref.shape[0] // C
    rows = lambda b: slice(b * C, (b + 1) * C)
    q, k, v, gb = ([ref[rows(b), :].astype(F32) for b in range(nblk)] for ref in (q_ref, k_ref, v_ref, gb_ref))
    dirs = [(b, d) for b in range(nblk) for d in (0, 1)]
    groups = [(b, d, g) for b, d in dirs for g in range(nh // G)]
    gsl = lambda g: slice(g * G * LANES, (g + 1) * G * LANES)
    csl = lambda g: slice(g * G * C, (g + 1) * G * C)
    incl = [(ri >= ci), (ri <= ci)]
    strict = [(ri > ci), (ri < ci)]
    last = [C - 1, 0]
    tri = [(r2 >= c2).astype(F32), (r2 <= c2).astype(F32)]
    gcs = {(b, d): hdot(tri[d], gb[b]) for b, d in dirs}
    spread = lambda m, first: hdot(m, (sel_row == first + sel_head).astype(F32))
    gc = {(b, d): spread(gcs[b, d], d * nh) for b, d in dirs}
    bet = {(b, d): spread(gb[b], 2 * nh + d * nh) for b, d in dirs}
    gcol = {(b, d, g): hdot(gcs[b, d], (cat_row == d * nh + g * G + cat_head).astype(F32)) for b, d, g in groups}
    grow = {key: hdot(ones, x * eyecat) for key, x in gcol.items()}
    kbm = {(b, d): k[b] * bet[b, d] for b, d in dirs}
    kq = {(b, d, g): _bdot(jnp.concatenate([kbm[b, d][:, gsl(g)], q[b][:, gsl(g)]], axis=0),
                           _block_diag(k[b][:, gsl(g)], G), NT) for b, d, g in groups}
    eg = {key: jnp.exp(x) for key, x in gc.items()}
    glrow = {(b, d): gc[b, d][last[d]:last[d] + 1, :] for b, d in dirs}
    kd = {(b, d): k[b] * jnp.exp(glrow[b, d] - gc[b, d]) for b, d in dirs}
    kdt = {(b, d, g): _bdot(eye128, jnp.concatenate(
        [kd[b, d][:, (g * G + j) * LANES:(g * G + j + 1) * LANES] for j in range(G)], axis=0), NT)
        for b, d, g in groups}
    decay = {(b, d, g): jnp.where(incl[d], jnp.exp(jnp.where(incl[d], gcol[b, d, g] - grow[b, d, g], 0.0)), 0.0)
             for b, d, g in groups}
    nm, vbg, kbeg = {}, {}, {}
    for b, d, g in groups:
        x, dec = kq[b, d, g], decay[b, d, g]
        akd_ref[d, b, 0:C, csl(g)] = jnp.where(incl[d], x[C:2 * C] * dec, 0.0).astype(BF16)
        akd_ref[d, b, C:C + DN_DK, csl(g)] = kdt[b, d, g].astype(BF16)
        nm[b, d, g] = jnp.where(strict[d], -x[0:C] * dec, 0.0)
        vbg[b, d, g] = (v[b] * bet[b, d])[:, gsl(g)]
        kbeg[b, d, g] = (kbm[b, d] * eg[b, d])[:, gsl(g)]
    for b, d in dirs:
        wq_ref[d, (2 * b + 1) * C:(2 * b + 2) * C, :] = (q[b] * eg[b, d]).astype(BF16)
        gl_ref[d, b] = jnp.exp(glrow[b, d])
    tinv = {key: eyecat + x for key, x in nm.items()}
    p = dict(nm)
    pbd = {key: _block_diag(x, G) for key, x in p.items()}
    for _ in range(int(math.log2(C)) - 1):
        p = {key: _bdot(x, pbd[key]) for key, x in p.items()}
        pbd = {key: _block_diag(x, G) for key, x in p.items()}
        tinv = {key: t + _bdot(t, pbd[key]) for key, t in tinv.items()}
    for b, d, g in groups:
        t = tinv[b, d, g]
        u_ref[d, rows(b), gsl(g)] = _bdot(t, _block_diag(vbg[b, d, g], G))
        wq_ref[d, 2 * b * C:(2 * b + 1) * C, gsl(g)] = _bdot(t, _block_diag(kbeg[b, d, g], G)).astype(BF16)


def _dn_local(q, k, v, gb):
    n, d = q.shape
    C = DN_CHUNK
    nc = n // C
    nb = next(c for c in (4, 2, 1) if nc % c == 0)
    tok = pl.BlockSpec((nb * C, d), lambda j: (j, 0))
    return pl.pallas_call(
        _dn_local_kernel, grid=(nc // nb,),
        in_specs=[tok, tok, tok, pl.BlockSpec((nb * C, gb.shape[1]), lambda j: (j, 0))],
        out_specs=[pl.BlockSpec((2, nb * C, d), lambda j: (0, j, 0)),
                   pl.BlockSpec((2, nb * 2 * C, d), lambda j: (0, j, 0)),
                   pl.BlockSpec((2, nb, C + DN_DK, DN_HEADS * C), lambda j: (0, j, 0, 0)),
                   pl.BlockSpec((2, nb, 1, d), lambda j: (0, j, 0, 0))],
        out_shape=[jax.ShapeDtypeStruct((2, n, d), F32),
                   jax.ShapeDtypeStruct((2, 2 * n, d), BF16),
                   jax.ShapeDtypeStruct((2, nc, C + DN_DK, DN_HEADS * C), BF16),
                   jax.ShapeDtypeStruct((2, nc, 1, d), F32)],
        compiler_params=_cparams(("parallel",)))(q, k, v, gb)


def _dn_scan_kernel(*refs):
    ins, (of_ref, or_ref, s_ref) = refs[:8], refs[8:]
    C = DN_CHUNK

    @pl.when(pl.program_id(1) == 0)
    def _():
        s_ref[...] = jnp.zeros_like(s_ref)

    chains = [(d, p) for d in range(2) for p in range(DN_HEADS // 2)]
    outs = (of_ref, or_ref)
    lanes = lambda p: slice(2 * p * LANES, (2 * p + 2) * LANES)
    s2 = [jnp.concatenate([s_ref[d, 2 * p], s_ref[d, 2 * p + 1]], axis=-1) for d, p in chains]
    r = [_bdot(ins[4 * d + 1][0, :, lanes(p)], _block_diag(s, 2)) for (d, p), s in zip(chains, s2)]
    vn = [ins[4 * d][0, :, lanes(p)] - x[0:C] for (d, p), x in zip(chains, r)]
    r2 = [_bdot(ins[4 * d + 2][0, 0, :, 2 * p * C:(2 * p + 2) * C], _block_diag(x, 2))
          for (d, p), x in zip(chains, vn)]
    for (d, p), s, x, y in zip(chains, s2, r, r2):
        outs[d][:, lanes(p)] = x[C:2 * C] + y[0:C]
        snew = s * ins[4 * d + 3][0, 0, :, lanes(p)] + y[C:C + DN_DK]
        s_ref[d, 2 * p] = snew[:, 0:LANES]
        s_ref[d, 2 * p + 1] = snew[:, LANES:2 * LANES]


def _dn_scan(B, T, ctx_len, u, wq, akd, gl):
    _, n, d = u.shape
    C = DN_CHUNK
    nch, ncc = T // C, ctx_len // C
    rpos = lambda c: jnp.where(c < ncc, ncc - 1 - c, nch - 1 - (c - ncc))
    pos = (lambda b, c: b * nch + c, lambda b, c: b * nch + rpos(c))
    in_specs, args = [], []
    for dd in range(2):
        p = pos[dd]
        in_specs += [pl.BlockSpec((1, C, d), lambda b, c, p=p, dd=dd: (dd, p(b, c), 0)),
                     pl.BlockSpec((1, 2 * C, d), lambda b, c, p=p, dd=dd: (dd, p(b, c), 0)),
                     pl.BlockSpec((1, 1, C + DN_DK, DN_HEADS * C), lambda b, c, p=p, dd=dd: (dd, p(b, c), 0, 0)),
                     pl.BlockSpec((1, 1, 1, d), lambda b, c, p=p, dd=dd: (dd, p(b, c), 0, 0))]
        args += [u, wq, akd, gl]
    return pl.pallas_call(
        _dn_scan_kernel, grid=(B, nch), in_specs=in_specs,
        out_specs=[pl.BlockSpec((C, d), lambda b, c: (pos[0](b, c), 0)),
                   pl.BlockSpec((C, d), lambda b, c: (pos[1](b, c), 0))],
        out_shape=[jax.ShapeDtypeStruct((n, d), F32)] * 2,
        scratch_shapes=[pltpu.VMEM((2, DN_HEADS, DN_DK, LANES), F32)],
        compiler_params=_cparams(("arbitrary", "arbitrary")))(*args)


def _final_kernel(x_ref, y_ref, mod_ref, o_ref):
    o_ref[0] = x_ref[...] + mod_ref[0, 5:6, :] * y_ref[...]


def _final(B, T, ctx_len, x, y, mod):
    n, d = x.shape
    tm = TOKEN_BLOCK
    nblk, ncb = T // tm, ctx_len // tm
    tok = pl.BlockSpec((tm, d), lambda b, i: (b * nblk + ncb + i, 0))
    ytok = pl.BlockSpec((tm, d), lambda b, i: (b * (nblk - ncb) + i, 0))
    return pl.pallas_call(
        _final_kernel, grid=(B, nblk - ncb),
        in_specs=[tok, ytok, pl.BlockSpec((1, 6, d), lambda b, i: (b, 0, 0))],
        out_specs=pl.BlockSpec((1, tm, d), lambda b, i: (b, i, 0)),
        out_shape=jax.ShapeDtypeStruct((B, T - ctx_len, d), F32),
        compiler_params=_cparams(("parallel", "parallel")))(x, y, mod)


def _rope_tables(seq, ctx_len):
    rows = seq // GRID_W
    r = jnp.broadcast_to(jnp.arange(rows, dtype=F32)[:, None], (rows, GRID_W)).reshape(-1)
    cl = jnp.broadcast_to(jnp.arange(GRID_W, dtype=F32)[None, :], (rows, GRID_W)).reshape(-1)

    def angles(rot_dim):
        nf = rot_dim // 4
        inv = ROPE_BASE ** (-jnp.arange(nf, dtype=F32) / nf)
        ang = jnp.concatenate([r[:, None] * inv, cl[:, None] * inv], axis=-1)
        ang = jnp.concatenate([jnp.zeros((ctx_len, rot_dim // 2), F32), ang], axis=0)
        return jnp.cos(ang), jnp.sin(ang)

    T = seq + ctx_len
    ca, sa = angles(DIFF_DIM)
    z = jnp.zeros_like(sa)
    ta = jnp.stack([jnp.concatenate([ca, ca] * 2, -1), jnp.concatenate([-sa, z] * 2, -1),
                    jnp.concatenate([z, sa] * 2, -1)])
    cb, sb = angles(MLA_ROPE)
    one, zn, zt = jnp.ones((T, MLA_NOPE), F32), jnp.zeros((T, MLA_NOPE), F32), jnp.zeros((T, LANES - MLA_QK), F32)
    zb = jnp.zeros_like(sb)
    tb = jnp.stack([jnp.concatenate([one, cb, cb, 1.0 + zt], -1), jnp.concatenate([zn, -sb, zb, zt], -1),
                    jnp.concatenate([zn, zb, sb, zt], -1)])
    return ta, tb


def _pad_heads(w, heads, width):
    lead = w.shape[:-1]
    w = w.reshape(lead + (heads, width))
    return jnp.pad(w, [(0, 0)] * len(lead) + [(0, 0), (0, LANES - width)]).reshape(lead + (heads * LANES,))


def _block_diag_ones(n, group):
    idx = np.arange(n) // group
    return jnp.asarray(idx[:, None] == idx[None, :], dtype=BF16)


def kernel(x, c, ctx, c_ctx, ada_w, ada_b, ev_w_in, ev_w_out, diff_q_norm, diff_k_norm, diff_lam_q1, diff_lam_k1, diff_lam_q2, diff_lam_k2, diff_subln, mla_cq_norm, mla_ckv_norm, mla_w_uq, mla_w_ukv, mla_q_norm, mla_k_norm, dn_w_in, dn_conv, dn_a_log, dn_dt_bias, dn_o_norm, dn_w_out, peer_wq, peer_k1, peer_k2, peer_u, peer_v):
    B, S, D = x.shape
    CTX = ctx.shape[1]
    T = CTX + S
    depth = ada_w.shape[0]
    tm = TOKEN_BLOCK
    assert CTX % tm == 0 and S % tm == 0 and B + 1 <= SUBLANES
    assert (B * T) % PEER_TOKEN_BLOCK == 0 and (B * S) % PEER_TOKEN_BLOCK == 0
    nblk, ncb = T // tm, CTX // tm
    geo = (B, nblk, tm, lambda b, i: jnp.where(i < ncb, B, b))

    cc = jnp.zeros((SUBLANES, D), F32).at[:B].set(c).at[B].set(c_ctx)
    mods = _adaln(cc, ada_w, ada_b).reshape(depth, SUBLANES, 6, D)
    xs = jnp.concatenate([ctx, x], axis=1).reshape(B * T, D)
    ropea, ropeb = _rope_tables(S, CTX)
    bd64, bd128 = _block_diag_ones(HEAD_W, DIFF_DIM), _block_diag_ones(HEAD_W, LANES)

    y = None
    for l in range(depth):
        i = l // 2
        mod = mods[l]
        gmod = mods[l - 1] if l else None
        n_skip = ncb if l == depth - 1 else 0
        if l % 2 == 0:
            lam_init = 0.8 - 0.6 * math.exp(-0.3 * l)
            w_in = ev_w_in[i]
            o = 3 * HEAD_W + MLA_Q_RANK + MLA_KV_RANK
            kr_rep = jnp.pad(jnp.broadcast_to(w_in[:, None, o:o + MLA_ROPE], (D, MLA_HEADS, MLA_ROPE)),
                             ((0, 0), (0, 0), (MLA_NOPE, LANES - MLA_QK))).reshape(D, HEAD_W)
            w_cat = jnp.concatenate([w_in[:, :o], kr_rep], axis=1).astype(BF16)
            res = _modmm(geo, xs, mod, w_cat, y, gmod)
            (xs, z) = res if y is not None else (xs, res)
            ukv = mla_w_ukv[i].reshape(MLA_KV_RANK, MLA_HEADS, MLA_NOPE + MLA_VDIM)
            gains = jnp.stack([jnp.tile(diff_q_norm[i], HEAD_W // DIFF_DIM), jnp.tile(diff_k_norm[i], HEAD_W // DIFF_DIM),
                               jnp.tile(jnp.pad(mla_q_norm[i], (0, LANES - MLA_QK)), MLA_HEADS),
                               jnp.tile(jnp.pad(mla_k_norm[i], (0, LANES - MLA_QK)), MLA_HEADS)])
            qa, ka, va, qb, kb, vb = _even_prep(
                geo, z, ropea, ropeb, gains, mla_cq_norm[i][None, :], mla_ckv_norm[i][None, :],
                _pad_heads(mla_w_uq[i], MLA_HEADS, MLA_QK).astype(BF16),
                _pad_heads(ukv[:, :, :MLA_NOPE].reshape(MLA_KV_RANK, -1), MLA_HEADS, MLA_NOPE).astype(BF16),
                ukv[:, :, MLA_NOPE:].reshape(MLA_KV_RANK, -1).astype(BF16), bd64, bd128)
            lam_vecs = jnp.stack([diff_lam_q1[i], diff_lam_k1[i], diff_lam_q2[i], diff_lam_k2[i]])
            oa = _attention(B, T, CTX, qa, ka, va, lam_vecs, diff_subln[i][None, :], lam_init)
            ob = _attention(B, T, CTX, qb, kb, vb)
            xs, hqt = _mixer_out(geo, _even_out_kernel, [(oa, HEAD_W, 0), (ob, HEAD_W, 0)], [], xs, mod,
                                 ev_w_out[i].astype(BF16), n_skip)
        else:
            nin = dn_w_in.shape[2]
            w_in = jnp.pad(dn_w_in[i], ((0, 0), (0, -nin % LANES))).astype(BF16)
            res = _modmm(geo, xs, mod, w_in, y, gmod)
            (xs, z) = res if y is not None else (xs, res)
            q, k, v, gb = _dn_prep(geo, ncb, z, dn_conv[i], dn_a_log[i].reshape(1, -1),
                                   dn_dt_bias[i].reshape(1, -1))
            of, orv = _dn_scan(B, T, CTX, *_dn_local(q, k, v, gb))
            xs, hqt = _mixer_out(geo, _dn_out_kernel,
                                 [(of, D, 0), (orv, D, 0), (z, D, DN_QKV // D)], [dn_o_norm[i][None, :]], xs, mod,
                                 dn_w_out[i].astype(BF16), n_skip)
        y = _peer(hqt, B * (nblk - n_skip) * tm, peer_wq[l], peer_k1[l], peer_k2[l], peer_u[l], peer_v[l])
    return _final(B, T, CTX, xs, y, mods[depth - 1])
```

```python
import functools
import math

import jax
import jax.numpy as jnp
import numpy as np
from jax import lax
from jax.experimental import pallas as pl
from jax.experimental.pallas import tpu as pltpu

F32 = jnp.float32
BF16 = jnp.bfloat16
HI = lax.Precision.HIGHEST
NT = (((1,), (1,)), ((), ()))

EPS = 1e-6
LOG2E = math.log2(math.e)
ROPE_BASE = 10000.0
GRID_W = 64
LANES, SUBLANES = 128, 8
MXU_TILE = 256
TOKEN_BLOCK = 256
VMEM_LIMIT = 56 * 1024 * 1024

DIFF_HEADS, DIFF_DIM = 4, 64
MLA_HEADS, MLA_Q_RANK, MLA_KV_RANK, MLA_NOPE, MLA_ROPE, MLA_VDIM = 4, 256, 128, 64, 32, 128
MLA_QK = MLA_NOPE + MLA_ROPE
HEAD_W = DIFF_HEADS * LANES
DN_HEADS, DN_DK, DN_CONV, DN_CHUNK = 8, 128, 5, 64
DN_QKV = 3 * DN_HEADS * DN_DK
PEER_HEADS, PEER_NKEYS, PEER_TOPK = 8, 128, 16
PEER_CAND_ROWS = 80
PEER_TOKEN_BLOCK = 512
PEER_EXPERT_BLOCK = 2048
PEER_PIECE = MXU_TILE


def _cparams(sem):
    return pltpu.CompilerParams(dimension_semantics=sem, vmem_limit_bytes=VMEM_LIMIT)


def _rms_rows(x):
    return x * lax.rsqrt(jnp.mean(x * x, axis=-1, keepdims=True) + EPS)


def _silu(x):
    return x * jax.nn.sigmoid(x)


def _group_sum(sq, bd):
    hi = sq.astype(BF16)
    lo = (sq - hi.astype(F32)).astype(BF16)
    return jnp.dot(hi, bd, preferred_element_type=F32) + jnp.dot(lo, bd, preferred_element_type=F32)


def _adaln_kernel(c_ref, w_ref, b_ref, o_ref):
    s = _silu(c_ref[...])
    o_ref[0] = jnp.dot(s, w_ref[0], precision=HI, preferred_element_type=F32) + b_ref[0]


def _adaln(cc, ada_w, ada_b):
    depth, d, n = ada_w.shape
    rows = cc.shape[0]
    tn = n // 4
    return pl.pallas_call(
        _adaln_kernel,
        grid=(depth, n // tn),
        in_specs=[pl.BlockSpec((rows, d), lambda l, j: (0, 0)),
                  pl.BlockSpec((1, d, tn), lambda l, j: (l, 0, j)),
                  pl.BlockSpec((1, 1, tn), lambda l, j: (l, 0, j))],
        out_specs=pl.BlockSpec((1, rows, tn), lambda l, j: (l, 0, j)),
        out_shape=jax.ShapeDtypeStruct((depth, rows, n), F32),
        compiler_params=_cparams(("arbitrary", "arbitrary")),
    )(cc, ada_w, ada_b.reshape(depth, 1, n))


def _modmm_kernel(*refs, has_y):
    if has_y:
        x_ref, y_ref, gmod_ref, mod_ref, w_ref, xo_ref, z_ref = refs
        x = x_ref[...] + gmod_ref[0, 5:6, :] * y_ref[...]
        xo_ref[...] = x
    else:
        x_ref, mod_ref, w_ref, z_ref = refs
        x = x_ref[...]
    h = _rms_rows(x) * (1.0 + mod_ref[0, 1:2, :]) + mod_ref[0, 0:1, :]
    z_ref[...] = jnp.dot(h.astype(BF16), w_ref[...], preferred_element_type=F32)


def _modmm(geo, x, mod, w, y=None, gmod=None):
    B, nblk, tm, mrow = geo
    n, d = x.shape
    nout = w.shape[1]
    tok = pl.BlockSpec((tm, d), lambda b, i: (b * nblk + i, 0))
    modspec = pl.BlockSpec((1, 6, d), lambda b, i: (mrow(b, i), 0, 0))
    wspec = pl.BlockSpec((d, nout), lambda b, i: (0, 0))
    zspec = pl.BlockSpec((tm, nout), lambda b, i: (b * nblk + i, 0))
    zshape = jax.ShapeDtypeStruct((n, nout), F32)
    if y is None:
        return pl.pallas_call(
            functools.partial(_modmm_kernel, has_y=False), grid=(B, nblk),
            in_specs=[tok, modspec, wspec], out_specs=zspec, out_shape=zshape,
            compiler_params=_cparams(("parallel", "parallel")))(x, mod, w)
    return pl.pallas_call(
        functools.partial(_modmm_kernel, has_y=True), grid=(B, nblk),
        in_specs=[tok, tok, modspec, modspec, wspec], out_specs=[tok, zspec],
        out_shape=[jax.ShapeDtypeStruct((n, d), F32), zshape],
        compiler_params=_cparams(("parallel", "parallel")))(x, y, gmod, mod, w)


def _rope_lanes(v, tab_ref, half):
    rep = v.shape[-1] // LANES
    c = jnp.concatenate([tab_ref[0]] * rep, axis=-1)
    sm = jnp.concatenate([tab_ref[1]] * rep, axis=-1)
    sp = jnp.concatenate([tab_ref[2]] * rep, axis=-1)
    n = v.shape[-1]
    return v * c + pltpu.roll(v, n - half, 1) * sm + pltpu.roll(v, half, 1) * sp


def _even_prep_kernel(z_ref, ropea_ref, ropeb_ref, gains_ref, cqn_ref, ckvn_ref, wuq_ref, wuk_ref, wuv_ref,
                      bd64_ref, bd128_ref, qa_ref, ka_ref, va_ref, qb_ref, kb_ref, vb_ref):
    W = HEAD_W
    bd64 = bd64_ref[...]
    bd128 = bd128_ref[...]

    def norm_groups(v, bd, width, gain):
        ms = _group_sum(v * v, bd) * (1.0 / width)
        return v * lax.rsqrt(ms + EPS) * gain

    qa = norm_groups(z_ref[:, 0:W], bd64, DIFF_DIM, gains_ref[0:1, :])
    ka = norm_groups(z_ref[:, W:2 * W], bd64, DIFF_DIM, gains_ref[1:2, :])
    qa_ref[...] = (_rope_lanes(qa, ropea_ref, DIFF_DIM // 2) * (DIFF_DIM ** -0.5 * LOG2E)).astype(BF16)
    ka_ref[...] = _rope_lanes(ka, ropea_ref, DIFF_DIM // 2).astype(BF16)
    va_ref[...] = z_ref[:, 2 * W:3 * W].astype(BF16)

    o = 3 * W
    cq = (_rms_rows(z_ref[:, o:o + MLA_Q_RANK]) * cqn_ref[...]).astype(BF16)
    o += MLA_Q_RANK
    ckv = (_rms_rows(z_ref[:, o:o + MLA_KV_RANK]) * ckvn_ref[...]).astype(BF16)
    o += MLA_KV_RANK
    kr = z_ref[:, o:o + W]
    qb = jnp.dot(cq, wuq_ref[...], preferred_element_type=F32)
    kb = jnp.dot(ckv, wuk_ref[...], preferred_element_type=F32) + kr
    qb = norm_groups(qb, bd128, MLA_QK, gains_ref[2:3, :])
    kb = norm_groups(kb, bd128, MLA_QK, gains_ref[3:4, :])
    qb_ref[...] = (_rope_lanes(qb, ropeb_ref, MLA_ROPE // 2) * (MLA_QK ** -0.5 * LOG2E)).astype(BF16)
    kb_ref[...] = _rope_lanes(kb, ropeb_ref, MLA_ROPE // 2).astype(BF16)
    vb_ref[...] = jnp.dot(ckv, wuv_ref[...], preferred_element_type=F32).astype(BF16)


def _even_prep(geo, z, ropea, ropeb, gains, cqn, ckvn, wuq, wuk, wuv, bd64, bd128):
    B, nblk, tm, _ = geo
    n = z.shape[0]
    W = HEAD_W
    full = lambda a: pl.BlockSpec(a.shape, lambda b, i: (0,) * a.ndim)
    rope = pl.BlockSpec((3, tm, LANES), lambda b, i: (0, i, 0))
    out = pl.BlockSpec((tm, W), lambda b, i: (b * nblk + i, 0))
    return pl.pallas_call(
        _even_prep_kernel, grid=(B, nblk),
        in_specs=[pl.BlockSpec((tm, z.shape[1]), lambda b, i: (b * nblk + i, 0)), rope, rope, full(gains),
                  full(cqn), full(ckvn), full(wuq), full(wuk), full(wuv), full(bd64), full(bd128)],
        out_specs=[out] * 6, out_shape=[jax.ShapeDtypeStruct((n, W), BF16)] * 6,
        compiler_params=_cparams(("parallel", "parallel")))(z, ropea, ropeb, gains, cqn, ckvn, wuq, wuk, wuv,
                                                            bd64, bd128)


def _attn_kernel(*refs, diff, tk, row_split, n_ctx_q, ctx_len, n_keys, lam_init):
    if diff:
        q_ref, k_ref, v_ref, lam_ref, subln_ref, o_ref, s_ref = refs
    else:
        q_ref, k_ref, v_ref, o_ref, s_ref = refs
    qi = pl.program_id(2)
    tr = q_ref.shape[0] // row_split
    qs = []
    for r in range(row_split):
        q = q_ref[r * tr:(r + 1) * tr, :]
        if diff:
            lane = lax.broadcasted_iota(jnp.int32, q.shape, 1)
            zero = jnp.zeros_like(q)
            qs += [jnp.where(lane < DIFF_DIM, q, zero), jnp.where(lane >= DIFF_DIM, q, zero)]
        else:
            qs.append(q)
    nsub = len(qs) // row_split

    def scores(slot, start, size):
        k = k_ref[pl.ds(start, size), :]
        s = [lax.dot_general(qq, k, NT, preferred_element_type=F32) for qq in qs]
        for j, x in enumerate(s):
            s_ref[slot, j, :, 0:size] = x
        return tuple(jnp.max(x, axis=-1, keepdims=True) for x in s)

    def update(slot, start, size, mx, carry):
        v = v_ref[pl.ds(start, size), :]
        mn = [jnp.maximum(m, x) for (m, _, _), x in zip(carry, mx)]
        p = [jnp.exp2(s_ref[slot, j, :, 0:size] - y) for j, y in enumerate(mn)]
        alpha = [jnp.exp2(m - y) for (m, _, _), y in zip(carry, mn)]
        pv = [jnp.dot(x.astype(BF16), v, preferred_element_type=F32) for x in p]
        return tuple((y, a * l + jnp.sum(x, axis=-1, keepdims=True), a * acc + z)
                     for (_, l, acc), y, a, x, z in zip(carry, mn, alpha, p, pv))

    init = tuple((jnp.full((tr, 1), -1e30, F32), jnp.zeros((tr, 1), F32), jnp.zeros((tr, LANES), F32))
                 for _ in qs)

    def finish(carry):
        outs = [acc / l for (_, l, acc) in carry]
        for r in range(row_split):
            if diff:
                lam = (jnp.exp(jnp.sum(lam_ref[0:1, :] * lam_ref[1:2, :], axis=-1, keepdims=True))
                       - jnp.exp(jnp.sum(lam_ref[2:3, :] * lam_ref[3:4, :], axis=-1, keepdims=True)) + lam_init)
                o = _rms_rows(outs[r * nsub] - lam * outs[r * nsub + 1]) * subln_ref[...] * (1.0 - lam_init)
            else:
                o = outs[r]
            o_ref[r * tr:(r + 1) * tr, :] = o.astype(o_ref.dtype)

    @pl.when(qi < n_ctx_q)
    def _():
        finish(update(0, 0, ctx_len, scores(0, 0, ctx_len), init))

    @pl.when(qi >= n_ctx_q)
    def _():
        n = n_keys // tk
        at = lambda c: pl.multiple_of(c * tk, tk)

        def pair(j, state):
            carry, mx0 = state
            mx1 = scores(1, at(2 * j + 1), tk)
            carry = update(0, at(2 * j), tk, mx0, carry)
            mx0 = scores(0, at(2 * j + 2), tk)
            carry = update(1, at(2 * j + 1), tk, mx1, carry)
            return carry, mx0

        carry, mx0 = lax.fori_loop(0, (n - 1) // 2, pair, (init, scores(0, 0, tk)))
        if n % 2 == 0:
            mx1 = scores(1, (n - 1) * tk, tk)
            carry = update(0, (n - 2) * tk, tk, mx0, carry)
            carry = update(1, (n - 1) * tk, tk, mx1, carry)
        else:
            carry = update(0, (n - 1) * tk, tk, mx0, carry)
        finish(carry)


def _attention(B, T, ctx_len, q, k, v, lam_vecs=None, subln=None, lam_init=0.0):
    diff = lam_vecs is not None
    tq = TOKEN_BLOCK
    tk = next(c for c in (1408, 768, TOKEN_BLOCK) if T % c == 0)
    nq = T // tq
    heads = q.shape[1] // LANES
    qspec = pl.BlockSpec((tq, LANES), lambda b, h, i: (b * nq + i, h))
    kvspec = pl.BlockSpec((T, LANES), lambda b, h, i: (b, h))
    in_specs = [qspec, kvspec, kvspec]
    args = [q, k, v]
    if diff:
        in_specs += [pl.BlockSpec(lam_vecs.shape, lambda b, h, i: (0, 0)),
                     pl.BlockSpec(subln.shape, lambda b, h, i: (0, 0))]
        args += [lam_vecs, subln]
    row_split = 1 if diff else 2
    chains = row_split * (2 if diff else 1)
    return pl.pallas_call(
        functools.partial(_attn_kernel, diff=diff, tk=tk, row_split=row_split, n_ctx_q=ctx_len // tq,
                          ctx_len=ctx_len, n_keys=T, lam_init=lam_init),
        grid=(B, heads, nq), in_specs=in_specs, out_specs=qspec,
        out_shape=jax.ShapeDtypeStruct(q.shape, BF16),
        scratch_shapes=[pltpu.VMEM((2, chains, tq // row_split, max(tk, ctx_len)), F32)],
        compiler_params=_cparams(("parallel", "parallel", "arbitrary")))(*args)


def _residual_tail(x, y, mod_ref, xo_ref, hqt_ref):
    xn = x + mod_ref[0, 2:3, :] * y
    xo_ref[...] = xn
    hq = _rms_rows(xn) * (1.0 + mod_ref[0, 4:5, :]) + mod_ref[0, 3:4, :]
    hqt_ref[...] = hq.T.astype(BF16)


def _even_out_kernel(oa_ref, ob_ref, x_ref, mod_ref, wo_ref, xo_ref, hqt_ref):
    W = HEAD_W
    y = (jnp.dot(oa_ref[...], wo_ref[0:W, :], preferred_element_type=F32)
         + jnp.dot(ob_ref[...], wo_ref[W:2 * W, :], preferred_element_type=F32))
    _residual_tail(x_ref[...], y, mod_ref, xo_ref, hqt_ref)


def _dn_out_kernel(of_ref, or_ref, zg_ref, onorm_ref, x_ref, mod_ref, wo_ref, xo_ref, hqt_ref):
    parts = []
    for h in range(DN_HEADS):
        hs = slice(h * LANES, (h + 1) * LANES)
        o = of_ref[:, hs] + or_ref[:, hs]
        parts.append((_rms_rows(o) * onorm_ref[...] * _silu(zg_ref[:, hs])).astype(BF16))
    y = jnp.dot(jnp.concatenate(parts, axis=-1), wo_ref[...], preferred_element_type=F32)
    _residual_tail(x_ref[...], y, mod_ref, xo_ref, hqt_ref)


def _mixer_out(geo, kernel_fn, token_args, small_args, x, mod, wo, n_skip=0):
    B, nblk, tm, mrow = geo
    n, d = x.shape
    keep = nblk - n_skip
    hq_col = (lambda b, i: b * nblk + i) if n_skip == 0 else (
        lambda b, i: jnp.where(i < n_skip, B * keep + b * n_skip + i, b * keep + i - n_skip))
    n_hq = n
    tokspec = lambda a, col: pl.BlockSpec((tm, a[1]), lambda b, i: (b * nblk + i, col))
    in_specs = [tokspec((a, w), col) for (a, w, col) in token_args]
    in_specs += [pl.BlockSpec(a.shape, lambda b, i: (0,) * a.ndim) for a in small_args]
    in_specs += [pl.BlockSpec((tm, d), lambda b, i: (b * nblk + i, 0)),
                 pl.BlockSpec((1, 6, d), lambda b, i: (mrow(b, i), 0, 0)),
                 pl.BlockSpec(wo.shape, lambda b, i: (0, 0))]
    return pl.pallas_call(
        kernel_fn, grid=(B, nblk), in_specs=in_specs,
        out_specs=[pl.BlockSpec((tm, d), lambda b, i: (b * nblk + i, 0)),
                   pl.BlockSpec((d, tm), lambda b, i: (0, hq_col(b, i)))],
        out_shape=[jax.ShapeDtypeStruct((n, d), F32), jax.ShapeDtypeStruct((d, n_hq), BF16)],
        compiler_params=_cparams(("parallel", "parallel")))(
            *[a for (a, _, _) in token_args], *small_args, x, mod, wo)


def _peer_select_kernel(hqt_ref, wqt_ref, k1_ref, k2_ref, th_ref, e1_ref, s2_ref, e2_ref, work_ref, top_ref, cand_ref):
    K, H = PEER_TOPK, PEER_HEADS
    qt = jnp.dot(wqt_ref[...], hqt_ref[...], preferred_element_type=F32)
    dk = k1_ref.shape[1]
    for h in range(H):
        q1 = qt[(2 * h) * dk:(2 * h + 1) * dk, :].astype(BF16)
        q2 = qt[(2 * h + 1) * dk:(2 * h + 2) * dk, :].astype(BF16)
        s1 = jnp.dot(k1_ref[...], q1, preferred_element_type=F32)
        s2 = jnp.dot(k2_ref[...], q2, preferred_element_type=F32)
        th_ref[h] = s1
        s2_ref[h] = s2
        work_ref[2 * h] = s1
        work_ref[2 * h + 1] = s2

    def extract(ref, count):
        s = [ref[a] for a in range(count)]
        m = [jnp.max(x, axis=0, keepdims=True) for x in s]
        for a in range(count):
            ref[a] = jnp.where(s[a] == m[a], -jnp.inf, s[a])
        return m

    def top_round(r, carry):
        for a, m in enumerate(extract(work_ref, 2 * H)):
            top_ref[a, pl.ds(r, 1), :] = m
        return carry

    lax.fori_loop(0, K, top_round, 0)

    def candidates(h):
        v1, v2 = top_ref[2 * h], top_ref[2 * h + 1]
        row8 = lax.broadcasted_iota(jnp.int32, (SUBLANES, v1.shape[1]), 0)
        pieces = [v1[0:1, :] + v2, v1[1:2, :] + v2[0:8, :]]
        pieces += [jnp.where(row8 < K // (r1 + 1), v1[r1:r1 + 1, :] + v2[0:8, :], -jnp.inf) for r1 in range(2, 8)]
        pieces.append(v1[8:16, :] + v2[0:1, :])
        return jnp.concatenate(pieces, axis=0)

    for h in range(H):
        cand_ref[h] = candidates(h)
    tau = lax.fori_loop(0, K, lambda r, carry: tuple(extract(cand_ref, H)),
                        tuple(jnp.zeros((1, hqt_ref.shape[1]), F32) for _ in range(H)))
    for h in range(H):
        cand = candidates(h)
        m1, m2 = top_ref[2 * h, 0:1, :], top_ref[2 * h + 1, 0:1, :]
        zsum = jnp.sum(jnp.where(cand >= tau[h], jnp.exp(cand - (m1 + m2)), 0.0), axis=0, keepdims=True)
        s1 = th_ref[h]
        e1_ref[h] = jnp.exp(s1 - m1) / zsum
        e2_ref[h] = jnp.exp(s2_ref[h] - m2)
        v2 = top_ref[2 * h + 1]
        theta = jnp.full(s1.shape, jnp.inf, F32)
        for r in range(K):
            theta = jnp.where((s1 + v2[r:r + 1, :]) >= tau[h], v2[r:r + 1, :], theta)
        th_ref[h] = theta


def _peer_select(hqt, n, wqt, k1, k2):
    d = hqt.shape[0]
    tm = PEER_TOKEN_BLOCK
    H, NK = PEER_HEADS, PEER_NKEYS
    big = pl.BlockSpec((H, NK, tm), lambda j: (0, 0, j))
    bigshape = jax.ShapeDtypeStruct((H, NK, n), F32)
    full = lambda a: pl.BlockSpec(a.shape, lambda j: (0,) * a.ndim)
    return pl.pallas_call(
        _peer_select_kernel, grid=(n // tm,),
        in_specs=[pl.BlockSpec((d, tm), lambda j: (0, j)), full(wqt), full(k1), full(k2)],
        out_specs=[big] * 4, out_shape=[bigshape] * 4,
        scratch_shapes=[pltpu.VMEM((2 * H, NK, tm), F32), pltpu.VMEM((2 * H, PEER_TOPK, tm), F32),
                        pltpu.VMEM((H, PEER_CAND_ROWS, tm), F32)],
        compiler_params=_cparams(("parallel",)))(hqt, wqt, k1, k2)


def _gelu(a):
    return 0.5 * a * (1.0 + lax.erf(a * np.float32(math.sqrt(0.5))))


def _peer_dense_kernel(hqt_ref, u_ref, vt_ref, th_ref, e1_ref, s2_ref, e2_ref, zero_ref, y_ref, row_ref,
                       *piece_refs):
    c = pl.program_id(1)
    NK = PEER_NKEYS
    n_i = PEER_EXPERT_BLOCK // NK
    pieces = [piece_refs[3 * p:3 * p + 3] for p in range(len(piece_refs) // 3)]

    @pl.when(c == 0)
    def _():
        for acc_ref, _, _ in pieces:
            acc_ref[...] = jnp.zeros_like(acc_ref)

    H = PEER_HEADS
    strips = [slice(ts * LANES, (ts + 1) * LANES) for ts in range(hqt_ref.shape[1] // LANES)]
    for ii in range(n_i):
        for h in range(H):
            i = c * n_i + ii
            row_ref[0, ii, h:h + 1, :] = th_ref[h, pl.ds(i, 1), :]
            row_ref[1, ii, h:h + 1, :] = e1_ref[h, pl.ds(i, 1), :]

    JB, RUNS, PIECE = 16, 8, PEER_PIECE
    zero = zero_ref[0:JB, :]
    for p, (_, act_ref, _) in enumerate(pieces):
        pc = slice(p * PIECE, (p + 1) * PIECE)
        act_ref[...] = _gelu(jnp.dot(u_ref[...], hqt_ref[:, pc], preferred_element_type=F32))
    for p, (acc_ref, act_ref, wt_ref) in enumerate(pieces):
        for ts in range(PIECE // LANES):
            tl = strips[p * PIECE // LANES + ts]
            pl_ = strips[ts]
            for i0 in range(0, n_i, RUNS):
                for jb in range(NK // JB):
                    js = slice(jb * JB, (jb + 1) * JB)
                    g = [jnp.zeros((JB, LANES), F32) for _ in range(RUNS)]
                    for h in range(H):
                        s2 = s2_ref[h, js, tl] + zero
                        e2 = e2_ref[h, js, tl] + zero
                        for r in range(RUNS):
                            sel = s2 >= row_ref[0, i0 + r, h:h + 1, tl]
                            g[r] = g[r] + jnp.where(sel, row_ref[1, i0 + r, h:h + 1, tl] * e2, 0.0)
                    for r in range(RUNS):
                        rows = slice((i0 + r) * NK + jb * JB, (i0 + r) * NK + (jb + 1) * JB)
                        wt_ref[rows, pl_] = (g[r] * act_ref[rows, pl_]).astype(BF16)
        acc_ref[...] += jnp.dot(vt_ref[...], wt_ref[...], preferred_element_type=F32)

    @pl.when(c == pl.num_programs(1) - 1)
    def _():
        for p, (acc_ref, _, _) in enumerate(pieces):
            y_ref[p * PIECE:(p + 1) * PIECE, :] = acc_ref[...].T


def _peer_dense(hqt, n, u, vt, theta, e1, s2, e2):
    d = hqt.shape[0]
    tm, te = PEER_TOKEN_BLOCK, PEER_EXPERT_BLOCK
    H, NK = PEER_HEADS, PEER_NKEYS
    big = pl.BlockSpec((H, NK, tm), lambda j, c: (0, 0, j))
    return pl.pallas_call(
        _peer_dense_kernel, grid=(n // tm, u.shape[0] // te),
        in_specs=[pl.BlockSpec((d, tm), lambda j, c: (0, j)),
                  pl.BlockSpec((te, d), lambda j, c: (c, 0)),
                  pl.BlockSpec((d, te), lambda j, c: (0, c)),
                  big, big, big, big, pl.BlockSpec((NK, LANES), lambda j, c: (0, 0))],
        out_specs=pl.BlockSpec((tm, d), lambda j, c: (j, 0)),
        out_shape=jax.ShapeDtypeStruct((n, d), F32),
        scratch_shapes=[pltpu.VMEM((2, te // NK, H, tm), F32)]
        + [pltpu.VMEM((d, PEER_PIECE), F32), pltpu.VMEM((te, PEER_PIECE), F32),
           pltpu.VMEM((te, PEER_PIECE), BF16)] * (tm // PEER_PIECE),
        compiler_params=_cparams(("parallel", "arbitrary")))(hqt, u, vt, theta, e1, s2, e2,
                                                             jnp.zeros((NK, LANES), F32))


def _peer(hqt, n, wq, k1, k2, u_tab, v_tab):
    sel = _peer_select(hqt, n, wq.T.astype(BF16), k1.astype(BF16), k2.astype(BF16))
    return _peer_dense(hqt, n, u_tab.astype(BF16), v_tab.T.astype(BF16), *sel)


def _dn_prep_kernel(z_ref, prev_ref, next_ref, conv_ref, alog_ref, dtb_ref, ab_ref, q_ref, k_ref, v_ref, gb_ref,
                    ext_ref, *, n_ctx_blk, nblk):
    i = pl.program_id(1)
    tm = z_ref.shape[0]
    halo = prev_ref.shape[0]
    pad = DN_CONV // 2
    has_prev = jnp.logical_and(i != 0, i != n_ctx_blk)
    has_next = jnp.logical_and(i != n_ctx_blk - 1, i != nblk - 1)
    ext_ref[0:halo, :] = jnp.where(has_prev, prev_ref[...], 0.0)
    ext_ref[halo:halo + tm, :] = z_ref[...]
    ext_ref[halo + tm:, :] = jnp.where(has_next, next_ref[...], 0.0)
    nq = DN_HEADS * DN_DK
    for j in range(DN_QKV // LANES):
        cs = slice(j * LANES, (j + 1) * LANES)
        acc = conv_ref[0:1, cs] * ext_ref[halo - pad:halo - pad + tm, cs]
        for t in range(1, DN_CONV):
            acc = acc + conv_ref[t:t + 1, cs] * ext_ref[halo - pad + t:halo - pad + t + tm, cs]
        y = _silu(acc)
        if j * LANES < 2 * nq:
            y = y * lax.rsqrt(jnp.sum(y * y, axis=-1, keepdims=True) + EPS)
        if j * LANES < nq:
            q_ref[:, cs] = (y * DN_DK ** -0.5).astype(q_ref.dtype)
        elif j * LANES < 2 * nq:
            k_ref[:, slice(j * LANES - nq, (j + 1) * LANES - nq)] = y.astype(k_ref.dtype)
        else:
            v_ref[:, slice(j * LANES - 2 * nq, (j + 1) * LANES - 2 * nq)] = y.astype(v_ref.dtype)
    ab = ab_ref[...]
    nh = 2 * DN_HEADS
    xa = ab[:, 0:nh] + dtb_ref[...]
    softplus = jnp.maximum(xa, 0.0) + jnp.log(1.0 + jnp.exp(-jnp.abs(xa)))
    gb_ref[:, 0:nh] = -jnp.exp(alog_ref[...]) * softplus
    gb_ref[:, nh:2 * nh] = jax.nn.sigmoid(ab[:, nh:2 * nh])


def _dn_prep(geo, n_ctx_blk, z, conv_w, alog, dtb):
    B, nblk, tm, _ = geo
    n = z.shape[0]
    halo = SUBLANES
    r = tm // halo
    nh8 = n // halo
    d = DN_HEADS * DN_DK
    prev = pl.BlockSpec((halo, DN_QKV), lambda b, i: (jnp.maximum((b * nblk + i) * r - 1, 0), 0))
    nxt = pl.BlockSpec((halo, DN_QKV), lambda b, i: (jnp.minimum((b * nblk + i + 1) * r, nh8 - 1), 0))
    ab = z[:, DN_QKV + d:DN_QKV + d + 4 * DN_HEADS]
    tok = pl.BlockSpec((tm, d), lambda b, i: (b * nblk + i, 0))
    gbspec = pl.BlockSpec((tm, 4 * DN_HEADS), lambda b, i: (b * nblk + i, 0))
    full = lambda a: pl.BlockSpec(a.shape, lambda b, i: (0,) * a.ndim)
    return pl.pallas_call(
        functools.partial(_dn_prep_kernel, n_ctx_blk=n_ctx_blk, nblk=nblk), grid=(B, nblk),
        in_specs=[pl.BlockSpec((tm, DN_QKV), lambda b, i: (b * nblk + i, 0)), prev, nxt, full(conv_w),
                  full(alog), full(dtb), gbspec],
        out_specs=[tok, tok, tok, gbspec],
        out_shape=[jax.ShapeDtypeStruct((n, d), F32)] * 3 + [jax.ShapeDtypeStruct((n, 4 * DN_HEADS), F32)],
        scratch_shapes=[pltpu.VMEM((tm + 2 * halo, DN_QKV), F32)],
        compiler_params=_cparams(("parallel", "parallel")))(z, z, z, conv_w, alog, dtb, ab)


DN_GROUP = 4


def _bdot(a, b, dims=None):
    a, b = a.astype(BF16), b.astype(BF16)
    if dims is None:
        return jnp.dot(a, b, preferred_element_type=F32)
    return lax.dot_general(a, b, dims, preferred_element_type=F32)


def _block_diag(x, nblk):
    r, n = x.shape
    w = n // nblk
    tall = jnp.concatenate([x] * nblk, axis=0)
    rb = lax.broadcasted_iota(jnp.int32, tall.shape, 0) // r
    lb = lax.broadcasted_iota(jnp.int32, tall.shape, 1) // w
    return jnp.where(rb == lb, tall, jnp.zeros_like(tall))


def _dn_local_kernel(q_ref, k_ref, v_ref, gb_ref, u_ref, wq_ref, akd_ref, gl_ref):
    C, G, nh = DN_CHUNK, DN_GROUP, DN_HEADS
    d_all = nh * LANES
    ri = lax.broadcasted_iota(jnp.int32, (C, G * C), 0)
    ci = lax.broadcasted_iota(jnp.int32, (C, G * C), 1) % C
    eyecat = (ri == ci).astype(F32)
    ones = jnp.ones((C, C), F32)
    r2 = lax.broadcasted_iota(jnp.int32, (C, C), 0)
    c2 = lax.broadcasted_iota(jnp.int32, (C, C), 1)
    eye128 = (lax.broadcasted_iota(jnp.int32, (LANES, LANES), 0)
              == lax.broadcasted_iota(jnp.int32, (LANES, LANES), 1)).astype(BF16)
    ncol = gb_ref.shape[1]
    sel_row = lax.broadcasted_iota(jnp.int32, (ncol, d_all), 0)
    sel_head = lax.broadcasted_iota(jnp.int32, (ncol, d_all), 1) // LANES
    cat_row = lax.broadcasted_iota(jnp.int32, (ncol, G * C), 0)
    cat_head = lax.broadcasted_iota(jnp.int32, (ncol, G * C), 1) // C
    hdot = lambda a, b: jnp.dot(a, b, precision=HI, preferred_element_type=F32)
    nblk = q_ref.shape[0] // C
    rows = lambda b: slice(b * C, (b + 1) * C)
    q, k, v, gb = ([ref[rows(b), :].astype(F32) for b in range(nblk)] for ref in (q_ref, k_ref, v_ref, gb_ref))
    dirs = [(b, d) for b in range(nblk) for d in (0, 1)]
    groups = [(b, d, g) for b, d in dirs for g in range(nh // G)]
    gsl = lambda g: slice(g * G * LANES, (g + 1) * G * LANES)
    csl = lambda g: slice(g * G * C, (g + 1) * G * C)
    incl = [(ri >= ci), (ri <= ci)]
    strict = [(ri > ci), (ri < ci)]
    last = [C - 1, 0]
    tri = [(r2 >= c2).astype(F32), (r2 <= c2).astype(F32)]
    gcs = {(b, d): hdot(tri[d], gb[b]) for b, d in dirs}
    spread = lambda m, first: hdot(m, (sel_row == first + sel_head).astype(F32))
    gc = {(b, d): spread(gcs[b, d], d * nh) for b, d in dirs}
    bet = {(b, d): spread(gb[b], 2 * nh + d * nh) for b, d in dirs}
    gcol = {(b, d, g): hdot(gcs[b, d], (cat_row == d * nh + g * G + cat_head).astype(F32)) for b, d, g in groups}
    grow = {key: hdot(ones, x * eyecat) for key, x in gcol.items()}
    kbm = {(b, d): k[b] * bet[b, d] for b, d in dirs}
    kq = {(b, d, g): _bdot(jnp.concatenate([kbm[b, d][:, gsl(g)], q[b][:, gsl(g)]], axis=0),
                           _block_diag(k[b][:, gsl(g)], G), NT) for b, d, g in groups}
    eg = {key: jnp.exp(x) for key, x in gc.items()}
    glrow = {(b, d): gc[b, d][last[d]:last[d] + 1, :] for b, d in dirs}
    kd = {(b, d): k[b] * jnp.exp(glrow[b, d] - gc[b, d]) for b, d in dirs}
    kdt = {(b, d, g): _bdot(eye128, jnp.concatenate(
        [kd[b, d][:, (g * G + j) * LANES:(g * G + j + 1) * LANES] for j in range(G)], axis=0), NT)
        for b, d, g in groups}
    decay = {(b, d, g): jnp.where(incl[d], jnp.exp(jnp.where(incl[d], gcol[b, d, g] - grow[b, d, g], 0.0)), 0.0)
             for b, d, g in groups}
    nm, vbg, kbeg = {}, {}, {}
    for b, d, g in groups:
        x, dec = kq[b, d, g], decay[b, d, g]
        akd_ref[d, b, 0:C, csl(g)] = jnp.where(incl[d], x[C:2 * C] * dec, 0.0).astype(BF16)
        akd_ref[d, b, C:C + DN_DK, csl(g)] = kdt[b, d, g].astype(BF16)
        nm[b, d, g] = jnp.where(strict[d], -x[0:C] * dec, 0.0)
        vbg[b, d, g] = (v[b] * bet[b, d])[:, gsl(g)]
        kbeg[b, d, g] = (kbm[b, d] * eg[b, d])[:, gsl(g)]
    for b, d in dirs:
        wq_ref[d, (2 * b + 1) * C:(2 * b + 2) * C, :] = (q[b] * eg[b, d]).astype(BF16)
        gl_ref[d, b] = jnp.exp(glrow[b, d])
    tinv = {key: eyecat + x for key, x in nm.items()}
    p = dict(nm)
    pbd = {key: _block_diag(x, G) for key, x in p.items()}
    for _ in range(int(math.log2(C)) - 1):
        p = {key: _bdot(x, pbd[key]) for key, x in p.items()}
        pbd = {key: _block_diag(x, G) for key, x in p.items()}
        tinv = {key: t + _bdot(t, pbd[key]) for key, t in tinv.items()}
    for b, d, g in groups:
        t = tinv[b, d, g]
        u_ref[d, rows(b), gsl(g)] = _bdot(t, _block_diag(vbg[b, d, g], G))
        wq_ref[d, 2 * b * C:(2 * b + 1) * C, gsl(g)] = _bdot(t, _block_diag(kbeg[b, d, g], G)).astype(BF16)


def _dn_local(q, k, v, gb):
    n, d = q.shape
    C = DN_CHUNK
    nc = n // C
    nb = next(c for c in (4, 2, 1) if nc % c == 0)
    tok = pl.BlockSpec((nb * C, d), lambda j: (j, 0))
    return pl.pallas_call(
        _dn_local_kernel, grid=(nc // nb,),
        in_specs=[tok, tok, tok, pl.BlockSpec((nb * C, gb.shape[1]), lambda j: (j, 0))],
        out_specs=[pl.BlockSpec((2, nb * C, d), lambda j: (0, j, 0)),
                   pl.BlockSpec((2, nb * 2 * C, d), lambda j: (0, j, 0)),
                   pl.BlockSpec((2, nb, C + DN_DK, DN_HEADS * C), lambda j: (0, j, 0, 0)),
                   pl.BlockSpec((2, nb, 1, d), lambda j: (0, j, 0, 0))],
        out_shape=[jax.ShapeDtypeStruct((2, n, d), F32),
                   jax.ShapeDtypeStruct((2, 2 * n, d), BF16),
                   jax.ShapeDtypeStruct((2, nc, C + DN_DK, DN_HEADS * C), BF16),
                   jax.ShapeDtypeStruct((2, nc, 1, d), F32)],
        compiler_params=_cparams(("parallel",)))(q, k, v, gb)


def _dn_scan_kernel(*refs):
    ins, (of_ref, or_ref, s_ref) = refs[:8], refs[8:]
    C = DN_CHUNK

    @pl.when(pl.program_id(1) == 0)
    def _():
        s_ref[...] = jnp.zeros_like(s_ref)

    chains = [(d, p) for d in range(2) for p in range(DN_HEADS // 2)]
    outs = (of_ref, or_ref)
    lanes = lambda p: slice(2 * p * LANES, (2 * p + 2) * LANES)
    s2 = [jnp.concatenate([s_ref[d, 2 * p], s_ref[d, 2 * p + 1]], axis=-1) for d, p in chains]
    r = [_bdot(ins[4 * d + 1][0, :, lanes(p)], _block_diag(s, 2)) for (d, p), s in zip(chains, s2)]
    vn = [ins[4 * d][0, :, lanes(p)] - x[0:C] for (d, p), x in zip(chains, r)]
    r2 = [_bdot(ins[4 * d + 2][0, 0, :, 2 * p * C:(2 * p + 2) * C], _block_diag(x, 2))
          for (d, p), x in zip(chains, vn)]
    for (d, p), s, x, y in zip(chains, s2, r, r2):
        outs[d][:, lanes(p)] = x[C:2 * C] + y[0:C]
        snew = s * ins[4 * d + 3][0, 0, :, lanes(p)] + y[C:C + DN_DK]
        s_ref[d, 2 * p] = snew[:, 0:LANES]
        s_ref[d, 2 * p + 1] = snew[:, LANES:2 * LANES]


def _dn_scan(B, T, ctx_len, u, wq, akd, gl):
    _, n, d = u.shape
    C = DN_CHUNK
    nch, ncc = T // C, ctx_len // C
    rpos = lambda c: jnp.where(c < ncc, ncc - 1 - c, nch - 1 - (c - ncc))
    pos = (lambda b, c: b * nch + c, lambda b, c: b * nch + rpos(c))
    in_specs, args = [], []
    for dd in range(2):
        p = pos[dd]
        in_specs += [pl.BlockSpec((1, C, d), lambda b, c, p=p, dd=dd: (dd, p(b, c), 0)),
                     pl.BlockSpec((1, 2 * C, d), lambda b, c, p=p, dd=dd: (dd, p(b, c), 0)),
                     pl.BlockSpec((1, 1, C + DN_DK, DN_HEADS * C), lambda b, c, p=p, dd=dd: (dd, p(b, c), 0, 0)),
                     pl.BlockSpec((1, 1, 1, d), lambda b, c, p=p, dd=dd: (dd, p(b, c), 0, 0))]
        args += [u, wq, akd, gl]
    return pl.pallas_call(
        _dn_scan_kernel, grid=(B, nch), in_specs=in_specs,
        out_specs=[pl.BlockSpec((C, d), lambda b, c: (pos[0](b, c), 0)),
                   pl.BlockSpec((C, d), lambda b, c: (pos[1](b, c), 0))],
        out_shape=[jax.ShapeDtypeStruct((n, d), F32)] * 2,
        scratch_shapes=[pltpu.VMEM((2, DN_HEADS, DN_DK, LANES), F32)],
        compiler_params=_cparams(("arbitrary", "arbitrary")))(*args)


def _final_kernel(x_ref, y_ref, mod_ref, o_ref):
    o_ref[0] = x_ref[...] + mod_ref[0, 5:6, :] * y_ref[...]


def _final(B, T, ctx_len, x, y, mod):
    n, d = x.shape
    tm = TOKEN_BLOCK
    nblk, ncb = T // tm, ctx_len // tm
    tok = pl.BlockSpec((tm, d), lambda b, i: (b * nblk + ncb + i, 0))
    ytok = pl.BlockSpec((tm, d), lambda b, i: (b * (nblk - ncb) + i, 0))
    return pl.pallas_call(
        _final_kernel, grid=(B, nblk - ncb),
        in_specs=[tok, ytok, pl.BlockSpec((1, 6, d), lambda b, i: (b, 0, 0))],
        out_specs=pl.BlockSpec((1, tm, d), lambda b, i: (b, i, 0)),
        out_shape=jax.ShapeDtypeStruct((B, T - ctx_len, d), F32),
        compiler_params=_cparams(("parallel", "parallel")))(x, y, mod)


def _rope_tables(seq, ctx_len):
    rows = seq // GRID_W
    r = jnp.broadcast_to(jnp.arange(rows, dtype=F32)[:, None], (rows, GRID_W)).reshape(-1)
    cl = jnp.broadcast_to(jnp.arange(GRID_W, dtype=F32)[None, :], (rows, GRID_W)).reshape(-1)

    def angles(rot_dim):
        nf = rot_dim // 4
        inv = ROPE_BASE ** (-jnp.arange(nf, dtype=F32) / nf)
        ang = jnp.concatenate([r[:, None] * inv, cl[:, None] * inv], axis=-1)
        ang = jnp.concatenate([jnp.zeros((ctx_len, rot_dim // 2), F32), ang], axis=0)
        return jnp.cos(ang), jnp.sin(ang)

    T = seq + ctx_len
    ca, sa = angles(DIFF_DIM)
    z = jnp.zeros_like(sa)
    ta = jnp.stack([jnp.concatenate([ca, ca] * 2, -1), jnp.concatenate([-sa, z] * 2, -1),
                    jnp.concatenate([z, sa] * 2, -1)])
    cb, sb = angles(MLA_ROPE)
    one, zn, zt = jnp.ones((T, MLA_NOPE), F32), jnp.zeros((T, MLA_NOPE), F32), jnp.zeros((T, LANES - MLA_QK), F32)
    zb = jnp.zeros_like(sb)
    tb = jnp.stack([jnp.concatenate([one, cb, cb, 1.0 + zt], -1), jnp.concatenate([zn, -sb, zb, zt], -1),
                    jnp.concatenate([zn, zb, sb, zt], -1)])
    return ta, tb


def _pad_heads(w, heads, width):
    lead = w.shape[:-1]
    w = w.reshape(lead + (heads, width))
    return jnp.pad(w, [(0, 0)] * len(lead) + [(0, 0), (0, LANES - width)]).reshape(lead + (heads * LANES,))


def _block_diag_ones(n, group):
    idx = np.arange(n) // group
    return jnp.asarray(idx[:, None] == idx[None, :], dtype=BF16)


def kernel(x, c, ctx, c_ctx, ada_w, ada_b, ev_w_in, ev_w_out, diff_q_norm, diff_k_norm, diff_lam_q1, diff_lam_k1, diff_lam_q2, diff_lam_k2, diff_subln, mla_cq_norm, mla_ckv_norm, mla_w_uq, mla_w_ukv, mla_q_norm, mla_k_norm, dn_w_in, dn_conv, dn_a_log, dn_dt_bias, dn_o_norm, dn_w_out, peer_wq, peer_k1, peer_k2, peer_u, peer_v):
    B, S, D = x.shape
    CTX = ctx.shape[1]
    T = CTX + S
    depth = ada_w.shape[0]
    tm = TOKEN_BLOCK
    assert CTX % tm == 0 and S % tm == 0 and B + 1 <= SUBLANES
    assert (B * T) % PEER_TOKEN_BLOCK == 0 and (B * S) % PEER_TOKEN_BLOCK == 0
    nblk, ncb = T // tm, CTX // tm
    geo = (B, nblk, tm, lambda b, i: jnp.where(i < ncb, B, b))

    cc = jnp.zeros((SUBLANES, D), F32).at[:B].set(c).at[B].set(c_ctx)
    mods = _adaln(cc, ada_w, ada_b).reshape(depth, SUBLANES, 6, D)
    xs = jnp.concatenate([ctx, x], axis=1).reshape(B * T, D)
    ropea, ropeb = _rope_tables(S, CTX)
    bd64, bd128 = _block_diag_ones(HEAD_W, DIFF_DIM), _block_diag_ones(HEAD_W, LANES)

    y = None
    for l in range(depth):
        i = l // 2
        mod = mods[l]
        gmod = mods[l - 1] if l else None
        n_skip = ncb if l == depth - 1 else 0
        if l % 2 == 0:
            lam_init = 0.8 - 0.6 * math.exp(-0.3 * l)
            w_in = ev_w_in[i]
            o = 3 * HEAD_W + MLA_Q_RANK + MLA_KV_RANK
            kr_rep = jnp.pad(jnp.broadcast_to(w_in[:, None, o:o + MLA_ROPE], (D, MLA_HEADS, MLA_ROPE)),
                             ((0, 0), (0, 0), (MLA_NOPE, LANES - MLA_QK))).reshape(D, HEAD_W)
            w_cat = jnp.concatenate([w_in[:, :o], kr_rep], axis=1).astype(BF16)
            res = _modmm(geo, xs, mod, w_cat, y, gmod)
            (xs, z) = res if y is not None else (xs, res)
            ukv = mla_w_ukv[i].reshape(MLA_KV_RANK, MLA_HEADS, MLA_NOPE + MLA_VDIM)
            gains = jnp.stack([jnp.tile(diff_q_norm[i], HEAD_W // DIFF_DIM), jnp.tile(diff_k_norm[i], HEAD_W // DIFF_DIM),
                               jnp.tile(jnp.pad(mla_q_norm[i], (0, LANES - MLA_QK)), MLA_HEADS),
                               jnp.tile(jnp.pad(mla_k_norm[i], (0, LANES - MLA_QK)), MLA_HEADS)])
            qa, ka, va, qb, kb, vb = _even_prep(
                geo, z, ropea, ropeb, gains, mla_cq_norm[i][None, :], mla_ckv_norm[i][None, :],
                _pad_heads(mla_w_uq[i], MLA_HEADS, MLA_QK).astype(BF16),
                _pad_heads(ukv[:, :, :MLA_NOPE].reshape(MLA_KV_RANK, -1), MLA_HEADS, MLA_NOPE).astype(BF16),
                ukv[:, :, MLA_NOPE:].reshape(MLA_KV_RANK, -1).astype(BF16), bd64, bd128)
            lam_vecs = jnp.stack([diff_lam_q1[i], diff_lam_k1[i], diff_lam_q2[i], diff_lam_k2[i]])
            oa = _attention(B, T, CTX, qa, ka, va, lam_vecs, diff_subln[i][None, :], lam_init)
            ob = _attention(B, T, CTX, qb, kb, vb)
            xs, hqt = _mixer_out(geo, _even_out_kernel, [(oa, HEAD_W, 0), (ob, HEAD_W, 0)], [], xs, mod,
                                 ev_w_out[i].astype(BF16), n_skip)
        else:
            nin = dn_w_in.shape[2]
            w_in = jnp.pad(dn_w_in[i], ((0, 0), (0, -nin % LANES))).astype(BF16)
            res = _modmm(geo, xs, mod, w_in, y, gmod)
            (xs, z) = res if y is not None else (xs, res)
            q, k, v, gb = _dn_prep(geo, ncb, z, dn_conv[i], dn_a_log[i].reshape(1, -1),
                                   dn_dt_bias[i].reshape(1, -1))
            of, orv = _dn_scan(B, T, CTX, *_dn_local(q, k, v, gb))
            xs, hqt = _mixer_out(geo, _dn_out_kernel,
                                 [(of, D, 0), (orv, D, 0), (z, D, DN_QKV // D)], [dn_o_norm[i][None, :]], xs, mod,
                                 dn_w_out[i].astype(BF16), n_skip)
        y = _peer(hqt, B * (nblk - n_skip) * tm, peer_wq[l], peer_k1[l], peer_k2[l], peer_u[l], peer_v[l])
    return _final(B, T, CTX, xs, y, mods[depth - 1])
```

```python
import functools
import math

import jax
import jax.numpy as jnp
import numpy as np
from jax import lax
from jax.experimental import pallas as pl
from jax.experimental.pallas import tpu as pltpu

F32 = jnp.float32
BF16 = jnp.bfloat16
HI = lax.Precision.HIGHEST
NT = (((1,), (1,)), ((), ()))

EPS = 1e-6
LOG2E = math.log2(math.e)
ROPE_BASE = 10000.0
GRID_W = 64
LANES, SUBLANES = 128, 8
MXU_TILE = 256
TOKEN_BLOCK = 256
VMEM_LIMIT = 56 * 1024 * 1024

DIFF_HEADS, DIFF_DIM = 4, 64
MLA_HEADS, MLA_Q_RANK, MLA_KV_RANK, MLA_NOPE, MLA_ROPE, MLA_VDIM = 4, 256, 128, 64, 32, 128
MLA_QK = MLA_NOPE + MLA_ROPE
HEAD_W = DIFF_HEADS * LANES
DN_HEADS, DN_DK, DN_CONV, DN_CHUNK = 8, 128, 5, 64
DN_QKV = 3 * DN_HEADS * DN_DK
PEER_HEADS, PEER_NKEYS, PEER_TOPK = 8, 128, 16
PEER_CAND_ROWS = 80
PEER_TOKEN_BLOCK = 512
PEER_EXPERT_BLOCK = 2048
PEER_PIECE = MXU_TILE


def _cparams(sem):
    return pltpu.CompilerParams(dimension_semantics=sem, vmem_limit_bytes=VMEM_LIMIT)


def _rms_rows(x):
    return x * lax.rsqrt(jnp.mean(x * x, axis=-1, keepdims=True) + EPS)


def _silu(x):
    return x * jax.nn.sigmoid(x)


def _group_sum(sq, bd):
    hi = sq.astype(BF16)
    lo = (sq - hi.astype(F32)).astype(BF16)
    return jnp.dot(hi, bd, preferred_element_type=F32) + jnp.dot(lo, bd, preferred_element_type=F32)


def _adaln_kernel(c_ref, w_ref, b_ref, o_ref):
    s = _silu(c_ref[...])
    o_ref[0] = jnp.dot(s, w_ref[0], precision=HI, preferred_element_type=F32) + b_ref[0]


def _adaln(cc, ada_w, ada_b):
    depth, d, n = ada_w.shape
    rows = cc.shape[0]
    tn = n // 4
    return pl.pallas_call(
        _adaln_kernel,
        grid=(depth, n // tn),
        in_specs=[pl.BlockSpec((rows, d), lambda l, j: (0, 0)),
                  pl.BlockSpec((1, d, tn), lambda l, j: (l, 0, j)),
                  pl.BlockSpec((1, 1, tn), lambda l, j: (l, 0, j))],
        out_specs=pl.BlockSpec((1, rows, tn), lambda l, j: (l, 0, j)),
        out_shape=jax.ShapeDtypeStruct((depth, rows, n), F32),
        compiler_params=_cparams(("arbitrary", "arbitrary")),
    )(cc, ada_w, ada_b.reshape(depth, 1, n))


def _modmm_kernel(*refs, has_y):
    if has_y:
        x_ref, y_ref, gmod_ref, mod_ref, w_ref, xo_ref, z_ref = refs
        x = x_ref[...] + gmod_ref[0, 5:6, :] * y_ref[...]
        xo_ref[...] = x
    else:
        x_ref, mod_ref, w_ref, z_ref = refs
        x = x_ref[...]
    h = _rms_rows(x) * (1.0 + mod_ref[0, 1:2, :]) + mod_ref[0, 0:1, :]
    z_ref[...] = jnp.dot(h.astype(BF16), w_ref[...], preferred_element_type=F32)


def _modmm(geo, x, mod, w, y=None, gmod=None):
    B, nblk, tm, mrow = geo
    n, d = x.shape
    nout = w.shape[1]
    tok = pl.BlockSpec((tm, d), lambda b, i: (b * nblk + i, 0))
    modspec = pl.BlockSpec((1, 6, d), lambda b, i: (mrow(b, i), 0, 0))
    wspec = pl.BlockSpec((d, nout), lambda b, i: (0, 0))
    zspec = pl.BlockSpec((tm, nout), lambda b, i: (b * nblk + i, 0))
    zshape = jax.ShapeDtypeStruct((n, nout), F32)
    if y is None:
        return pl.pallas_call(
            functools.partial(_modmm_kernel, has_y=False), grid=(B, nblk),
            in_specs=[tok, modspec, wspec], out_specs=zspec, out_shape=zshape,
            compiler_params=_cparams(("parallel", "parallel")))(x, mod, w)
    return pl.pallas_call(
        functools.partial(_modmm_kernel, has_y=True), grid=(B, nblk),
        in_specs=[tok, tok, modspec, modspec, wspec], out_specs=[tok, zspec],
        out_shape=[jax.ShapeDtypeStruct((n, d), F32), zshape],
        compiler_params=_cparams(("parallel", "parallel")))(x, y, gmod, mod, w)


def _rope_lanes(v, tab_ref, half):
    rep = v.shape[-1] // LANES
    c = jnp.concatenate([tab_ref[0]] * rep, axis=-1)
    sm = jnp.concatenate([tab_ref[1]] * rep, axis=-1)
    sp = jnp.concatenate([tab_ref[2]] * rep, axis=-1)
    n = v.shape[-1]
    return v * c + pltpu.roll(v, n - half, 1) * sm + pltpu.roll(v, half, 1) * sp


def _even_prep_kernel(z_ref, ropea_ref, ropeb_ref, gains_ref, cqn_ref, ckvn_ref, wuq_ref, wuk_ref, wuv_ref,
                      bd64_ref, bd128_ref, qa_ref, ka_ref, va_ref, qb_ref, kb_ref, vb_ref):
    W = HEAD_W
    bd64 = bd64_ref[...]
    bd128 = bd128_ref[...]

    def norm_groups(v, bd, width, gain):
        ms = _group_sum(v * v, bd) * (1.0 / width)
        return v * lax.rsqrt(ms + EPS) * gain

    qa = norm_groups(z_ref[:, 0:W], bd64, DIFF_DIM, gains_ref[0:1, :])
    ka = norm_groups(z_ref[:, W:2 * W], bd64, DIFF_DIM, gains_ref[1:2, :])
    qa_ref[...] = (_rope_lanes(qa, ropea_ref, DIFF_DIM // 2) * (DIFF_DIM ** -0.5 * LOG2E)).astype(BF16)
    ka_ref[...] = _rope_lanes(ka, ropea_ref, DIFF_DIM // 2).astype(BF16)
    va_ref[...] = z_ref[:, 2 * W:3 * W].astype(BF16)

    o = 3 * W
    cq = (_rms_rows(z_ref[:, o:o + MLA_Q_RANK]) * cqn_ref[...]).astype(BF16)
    o += MLA_Q_RANK
    ckv = (_rms_rows(z_ref[:, o:o + MLA_KV_RANK]) * ckvn_ref[...]).astype(BF16)
    o += MLA_KV_RANK
    kr = z_ref[:, o:o + W]
    qb = jnp.dot(cq, wuq_ref[...], preferred_element_type=F32)
    kb = jnp.dot(ckv, wuk_ref[...], preferred_element_type=F32) + kr
    qb = norm_groups(qb, bd128, MLA_QK, gains_ref[2:3, :])
    kb = norm_groups(kb, bd128, MLA_QK, gains_ref[3:4, :])
    qb_ref[...] = (_rope_lanes(qb, ropeb_ref, MLA_ROPE // 2) * (MLA_QK ** -0.5 * LOG2E)).astype(BF16)
    kb_ref[...] = _rope_lanes(kb, ropeb_ref, MLA_ROPE // 2).astype(BF16)
    vb_ref[...] = jnp.dot(ckv, wuv_ref[...], preferred_element_type=F32).astype(BF16)


def _even_prep(geo, z, ropea, ropeb, gains, cqn, ckvn, wuq, wuk, wuv, bd64, bd128):
    B, nblk, tm, _ = geo
    n = z.shape[0]
    W = HEAD_W
    full = lambda a: pl.BlockSpec(a.shape, lambda b, i: (0,) * a.ndim)
    rope = pl.BlockSpec((3, tm, LANES), lambda b, i: (0, i, 0))
    out = pl.BlockSpec((tm, W), lambda b, i: (b * nblk + i, 0))
    return pl.pallas_call(
        _even_prep_kernel, grid=(B, nblk),
        in_specs=[pl.BlockSpec((tm, z.shape[1]), lambda b, i: (b * nblk + i, 0)), rope, rope, full(gains),
                  full(cqn), full(ckvn), full(wuq), full(wuk), full(wuv), full(bd64), full(bd128)],
        out_specs=[out] * 6, out_shape=[jax.ShapeDtypeStruct((n, W), BF16)] * 6,
        compiler_params=_cparams(("parallel", "parallel")))(z, ropea, ropeb, gains, cqn, ckvn, wuq, wuk, wuv,
                                                            bd64, bd128)


def _attn_kernel(*refs, diff, tk, row_split, n_ctx_q, ctx_len, n_keys, lam_init):
    if diff:
        q_ref, k_ref, v_ref, lam_ref, subln_ref, o_ref, s_ref = refs
    else:
        q_ref, k_ref, v_ref, o_ref, s_ref = refs
    qi = pl.program_id(2)
    tr = q_ref.shape[0] // row_split
    qs = []
    for r in range(row_split):
        q = q_ref[r * tr:(r + 1) * tr, :]
        if diff:
            lane = lax.broadcasted_iota(jnp.int32, q.shape, 1)
            zero = jnp.zeros_like(q)
            qs += [jnp.where(lane < DIFF_DIM, q, zero), jnp.where(lane >= DIFF_DIM, q, zero)]
        else:
            qs.append(q)
    nsub = len(qs) // row_split

    def scores(slot, start, size):
        k = k_ref[pl.ds(start, size), :]
        s = [lax.dot_general(qq, k, NT, preferred_element_type=F32) for qq in qs]
        for j, x in enumerate(s):
            s_ref[slot, j, :, 0:size] = x
        return tuple(jnp.max(x, axis=-1, keepdims=True) for x in s)

    def update(slot, start, size, mx, carry):
        v = v_ref[pl.ds(start, size), :]
        mn = [jnp.maximum(m, x) for (m, _, _), x in zip(carry, mx)]
        p = [jnp.exp2(s_ref[slot, j, :, 0:size] - y) for j, y in enumerate(mn)]
        alpha = [jnp.exp2(m - y) for (m, _, _), y in zip(carry, mn)]
        pv = [jnp.dot(x.astype(BF16), v, preferred_element_type=F32) for x in p]
        return tuple((y, a * l + jnp.sum(x, axis=-1, keepdims=True), a * acc + z)
                     for (_, l, acc), y, a, x, z in zip(carry, mn, alpha, p, pv))

    init = tuple((jnp.full((tr, 1), -1e30, F32), jnp.zeros((tr, 1), F32), jnp.zeros((tr, LANES), F32))
                 for _ in qs)

    def finish(carry):
        outs = [acc / l for (_, l, acc) in carry]
        for r in range(row_split):
            if diff:
                lam = (jnp.exp(jnp.sum(lam_ref[0:1, :] * lam_ref[1:2, :], axis=-1, keepdims=True))
                       - jnp.exp(jnp.sum(lam_ref[2:3, :] * lam_ref[3:4, :], axis=-1, keepdims=True)) + lam_init)
                o = _rms_rows(outs[r * nsub] - lam * outs[r * nsub + 1]) * subln_ref[...] * (1.0 - lam_init)
            else:
                o = outs[r]
            o_ref[r * tr:(r + 1) * tr, :] = o.astype(o_ref.dtype)

    @pl.when(qi < n_ctx_q)
    def _():
        finish(update(0, 0, ctx_len, scores(0, 0, ctx_len), init))

    @pl.when(qi >= n_ctx_q)
    def _():
        n = n_keys // tk
        at = lambda c: pl.multiple_of(c * tk, tk)

        def pair(j, state):
            carry, mx0 = state
            mx1 = scores(1, at(2 * j + 1), tk)
            carry = update(0, at(2 * j), tk, mx0, carry)
            mx0 = scores(0, at(2 * j + 2), tk)
            carry = update(1, at(2 * j + 1), tk, mx1, carry)
            return carry, mx0

        carry, mx0 = lax.fori_loop(0, (n - 1) // 2, pair, (init, scores(0, 0, tk)))
        if n % 2 == 0:
            mx1 = scores(1, (n - 1) * tk, tk)
            carry = update(0, (n - 2) * tk, tk, mx0, carry)
            carry = update(1, (n - 1) * tk, tk, mx1, carry)
        else:
            carry = update(0, (n - 1) * tk, tk, mx0, carry)
        finish(carry)


def _attention(B, T, ctx_len, q, k, v, lam_vecs=None, subln=None, lam_init=0.0):
    diff = lam_vecs is not None
    tq = TOKEN_BLOCK
    tk = next(c for c in (2816, 1408, 768, TOKEN_BLOCK) if T % c == 0)
    nq = T // tq
    heads = q.shape[1] // LANES
    qspec = pl.BlockSpec((tq, LANES), lambda b, h, i: (b * nq + i, h))
    kvspec = pl.BlockSpec((T, LANES), lambda b, h, i: (b, h))
    in_specs = [qspec, kvspec, kvspec]
    args = [q, k, v]
    if diff:
        in_specs += [pl.BlockSpec(lam_vecs.shape, lambda b, h, i: (0, 0)),
                     pl.BlockSpec(subln.shape, lambda b, h, i: (0, 0))]
        args += [lam_vecs, subln]
    row_split = 1 if diff else 2
    chains = row_split * (2 if diff else 1)
    return pl.pallas_call(
        functools.partial(_attn_kernel, diff=diff, tk=tk, row_split=row_split, n_ctx_q=ctx_len // tq,
                          ctx_len=ctx_len, n_keys=T, lam_init=lam_init),
        grid=(B, heads, nq), in_specs=in_specs, out_specs=qspec,
        out_shape=jax.ShapeDtypeStruct(q.shape, BF16),
        scratch_shapes=[pltpu.VMEM((2, chains, tq // row_split, max(tk, ctx_len)), F32)],
        compiler_params=_cparams(("parallel", "parallel", "arbitrary")))(*args)


def _residual_tail(x, y, mod_ref, xo_ref, hqt_ref):
    xn = x + mod_ref[0, 2:3, :] * y
    xo_ref[...] = xn
    hq = _rms_rows(xn) * (1.0 + mod_ref[0, 4:5, :]) + mod_ref[0, 3:4, :]
    hqt_ref[...] = hq.T.astype(BF16)


def _even_out_kernel(oa_ref, ob_ref, x_ref, mod_ref, wo_ref, xo_ref, hqt_ref):
    W = HEAD_W
    y = (jnp.dot(oa_ref[...], wo_ref[0:W, :], preferred_element_type=F32)
         + jnp.dot(ob_ref[...], wo_ref[W:2 * W, :], preferred_element_type=F32))
    _residual_tail(x_ref[...], y, mod_ref, xo_ref, hqt_ref)


def _dn_out_kernel(of_ref, or_ref, zg_ref, onorm_ref, x_ref, mod_ref, wo_ref, xo_ref, hqt_ref):
    parts = []
    for h in range(DN_HEADS):
        hs = slice(h * LANES, (h + 1) * LANES)
        o = of_ref[:, hs] + or_ref[:, hs]
        parts.append((_rms_rows(o) * onorm_ref[...] * _silu(zg_ref[:, hs])).astype(BF16))
    y = jnp.dot(jnp.concatenate(parts, axis=-1), wo_ref[...], preferred_element_type=F32)
    _residual_tail(x_ref[...], y, mod_ref, xo_ref, hqt_ref)


def _mixer_out(geo, kernel_fn, token_args, small_args, x, mod, wo, n_skip=0):
    B, nblk, tm, mrow = geo
    n, d = x.shape
    keep = nblk - n_skip
    hq_col = (lambda b, i: b * nblk + i) if n_skip == 0 else (
        lambda b, i: jnp.where(i < n_skip, B * keep + b * n_skip + i, b * keep + i - n_skip))
    n_hq = n
    tokspec = lambda a, col: pl.BlockSpec((tm, a[1]), lambda b, i: (b * nblk + i, col))
    in_specs = [tokspec((a, w), col) for (a, w, col) in token_args]
    in_specs += [pl.BlockSpec(a.shape, lambda b, i: (0,) * a.ndim) for a in small_args]
    in_specs += [pl.BlockSpec((tm, d), lambda b, i: (b * nblk + i, 0)),
                 pl.BlockSpec((1, 6, d), lambda b, i: (mrow(b, i), 0, 0)),
                 pl.BlockSpec(wo.shape, lambda b, i: (0, 0))]
    return pl.pallas_call(
        kernel_fn, grid=(B, nblk), in_specs=in_specs,
        out_specs=[pl.BlockSpec((tm, d), lambda b, i: (b * nblk + i, 0)),
                   pl.BlockSpec((d, tm), lambda b, i: (0, hq_col(b, i)))],
        out_shape=[jax.ShapeDtypeStruct((n, d), F32), jax.ShapeDtypeStruct((d, n_hq), BF16)],
        compiler_params=_cparams(("parallel", "parallel")))(
            *[a for (a, _, _) in token_args], *small_args, x, mod, wo)


def _peer_select_kernel(hqt_ref, wqt_ref, k1_ref, k2_ref, th_ref, e1_ref, s2_ref, e2_ref, work_ref, top_ref, cand_ref):
    K, H = PEER_TOPK, PEER_HEADS
    qt = jnp.dot(wqt_ref[...], hqt_ref[...], preferred_element_type=F32)
    dk = k1_ref.shape[1]
    for h in range(H):
        q1 = qt[(2 * h) * dk:(2 * h + 1) * dk, :].astype(BF16)
        q2 = qt[(2 * h + 1) * dk:(2 * h + 2) * dk, :].astype(BF16)
        s1 = jnp.dot(k1_ref[...], q1, preferred_element_type=F32)
        s2 = jnp.dot(k2_ref[...], q2, preferred_element_type=F32)
        th_ref[h] = s1
        s2_ref[h] = s2
        work_ref[2 * h] = s1
        work_ref[2 * h + 1] = s2

    def extract(ref, count):
        s = [ref[a] for a in range(count)]
        m = [jnp.max(x, axis=0, keepdims=True) for x in s]
        for a in range(count):
            ref[a] = jnp.where(s[a] == m[a], -jnp.inf, s[a])
        return m

    def top_round(r, carry):
        for a, m in enumerate(extract(work_ref, 2 * H)):
            top_ref[a, pl.ds(r, 1), :] = m
        return carry

    lax.fori_loop(0, K, top_round, 0)

    def candidates(h):
        v1, v2 = top_ref[2 * h], top_ref[2 * h + 1]
        row8 = lax.broadcasted_iota(jnp.int32, (SUBLANES, v1.shape[1]), 0)
        pieces = [v1[0:1, :] + v2, v1[1:2, :] + v2[0:8, :]]
        pieces += [jnp.where(row8 < K // (r1 + 1), v1[r1:r1 + 1, :] + v2[0:8, :], -jnp.inf) for r1 in range(2, 8)]
        pieces.append(v1[8:16, :] + v2[0:1, :])
        return jnp.concatenate(pieces, axis=0)

    for h in range(H):
        cand_ref[h] = candidates(h)
    tau = lax.fori_loop(0, K, lambda r, carry: tuple(extract(cand_ref, H)),
                        tuple(jnp.zeros((1, hqt_ref.shape[1]), F32) for _ in range(H)))
    for h in range(H):
        cand = candidates(h)
        m1, m2 = top_ref[2 * h, 0:1, :], top_ref[2 * h + 1, 0:1, :]
        zsum = jnp.sum(jnp.where(cand >= tau[h], jnp.exp(cand - (m1 + m2)), 0.0), axis=0, keepdims=True)
        s1 = th_ref[h]
        e1_ref[h] = jnp.exp(s1 - m1) / zsum
        e2_ref[h] = jnp.exp(s2_ref[h] - m2)
        v2 = top_ref[2 * h + 1]
        theta = jnp.full(s1.shape, jnp.inf, F32)
        for r in range(K):
            theta = jnp.where((s1 + v2[r:r + 1, :]) >= tau[h], v2[r:r + 1, :], theta)
        th_ref[h] = theta


def _peer_select(hqt, n, wqt, k1, k2):
    d = hqt.shape[0]
    tm = PEER_TOKEN_BLOCK
    H, NK = PEER_HEADS, PEER_NKEYS
    big = pl.BlockSpec((H, NK, tm), lambda j: (0, 0, j))
    bigshape = jax.ShapeDtypeStruct((H, NK, n), F32)
    full = lambda a: pl.BlockSpec(a.shape, lambda j: (0,) * a.ndim)
    return pl.pallas_call(
        _peer_select_kernel, grid=(n // tm,),
        in_specs=[pl.BlockSpec((d, tm), lambda j: (0, j)), full(wqt), full(k1), full(k2)],
        out_specs=[big] * 4, out_shape=[bigshape] * 4,
        scratch_shapes=[pltpu.VMEM((2 * H, NK, tm), F32), pltpu.VMEM((2 * H, PEER_TOPK, tm), F32),
                        pltpu.VMEM((H, PEER_CAND_ROWS, tm), F32)],
        compiler_params=_cparams(("parallel",)))(hqt, wqt, k1, k2)


def _gelu(a):
    return 0.5 * a * (1.0 + lax.erf(a * np.float32(math.sqrt(0.5))))


def _peer_dense_kernel(hqt_ref, u_ref, vt_ref, th_ref, e1_ref, s2_ref, e2_ref, zero_ref, y_ref, row_ref,
                       *piece_refs):
    c = pl.program_id(1)
    NK = PEER_NKEYS
    n_i = PEER_EXPERT_BLOCK // NK
    pieces = [piece_refs[3 * p:3 * p + 3] for p in range(len(piece_refs) // 3)]

    @pl.when(c == 0)
    def _():
        for acc_ref, _, _ in pieces:
            acc_ref[...] = jnp.zeros_like(acc_ref)

    H = PEER_HEADS
    strips = [slice(ts * LANES, (ts + 1) * LANES) for ts in range(hqt_ref.shape[1] // LANES)]
    for ii in range(n_i):
        for h in range(H):
            i = c * n_i + ii
            row_ref[0, ii, h:h + 1, :] = th_ref[h, pl.ds(i, 1), :]
            row_ref[1, ii, h:h + 1, :] = e1_ref[h, pl.ds(i, 1), :]

    JB, RUNS, PIECE = 16, 8, PEER_PIECE
    zero = zero_ref[0:JB, :]
    for p, (_, act_ref, _) in enumerate(pieces):
        pc = slice(p * PIECE, (p + 1) * PIECE)
        act_ref[...] = _gelu(jnp.dot(u_ref[...], hqt_ref[:, pc], preferred_element_type=F32))
    for p, (acc_ref, act_ref, wt_ref) in enumerate(pieces):
        for ts in range(PIECE // LANES):
            tl = strips[p * PIECE // LANES + ts]
            pl_ = strips[ts]
            for i0 in range(0, n_i, RUNS):
                for jb in range(NK // JB):
                    js = slice(jb * JB, (jb + 1) * JB)
                    g = [jnp.zeros((JB, LANES), F32) for _ in range(RUNS)]
                    for h in range(H):
                        s2 = s2_ref[h, js, tl] + zero
                        e2 = e2_ref[h, js, tl] + zero
                        for r in range(RUNS):
                            sel = s2 >= row_ref[0, i0 + r, h:h + 1, tl]
                            g[r] = g[r] + jnp.where(sel, row_ref[1, i0 + r, h:h + 1, tl] * e2, 0.0)
                    for r in range(RUNS):
                        rows = slice((i0 + r) * NK + jb * JB, (i0 + r) * NK + (jb + 1) * JB)
                        wt_ref[rows, pl_] = (g[r] * act_ref[rows, pl_]).astype(BF16)
        acc_ref[...] += jnp.dot(vt_ref[...], wt_ref[...], preferred_element_type=F32)

    @pl.when(c == pl.num_programs(1) - 1)
    def _():
        for p, (acc_ref, _, _) in enumerate(pieces):
            y_ref[p * PIECE:(p + 1) * PIECE, :] = acc_ref[...].T


def _peer_dense(hqt, n, u, vt, theta, e1, s2, e2):
    d = hqt.shape[0]
    tm, te = PEER_TOKEN_BLOCK, PEER_EXPERT_BLOCK
    H, NK = PEER_HEADS, PEER_NKEYS
    big = pl.BlockSpec((H, NK, tm), lambda j, c: (0, 0, j))
    return pl.pallas_call(
        _peer_dense_kernel, grid=(n // tm, u.shape[0] // te),
        in_specs=[pl.BlockSpec((d, tm), lambda j, c: (0, j)),
                  pl.BlockSpec((te, d), lambda j, c: (c, 0)),
                  pl.BlockSpec((d, te), lambda j, c: (0, c)),
                  big, big, big, big, pl.BlockSpec((NK, LANES), lambda j, c: (0, 0))],
        out_specs=pl.BlockSpec((tm, d), lambda j, c: (j, 0)),
        out_shape=jax.ShapeDtypeStruct((n, d), F32),
        scratch_shapes=[pltpu.VMEM((2, te // NK, H, tm), F32)]
        + [pltpu.VMEM((d, PEER_PIECE), F32), pltpu.VMEM((te, PEER_PIECE), F32),
           pltpu.VMEM((te, PEER_PIECE), BF16)] * (tm // PEER_PIECE),
        compiler_params=_cparams(("parallel", "arbitrary")))(hqt, u, vt, theta, e1, s2, e2,
                                                             jnp.zeros((NK, LANES), F32))


def _peer(hqt, n, wq, k1, k2, u_tab, v_tab):
    sel = _peer_select(hqt, n, wq.T.astype(BF16), k1.astype(BF16), k2.astype(BF16))
    return _peer_dense(hqt, n, u_tab.astype(BF16), v_tab.T.astype(BF16), *sel)


def _dn_prep_kernel(z_ref, prev_ref, next_ref, conv_ref, alog_ref, dtb_ref, ab_ref, q_ref, k_ref, v_ref, gb_ref,
                    ext_ref, *, n_ctx_blk, nblk):
    i = pl.program_id(1)
    tm = z_ref.shape[0]
    halo = prev_ref.shape[0]
    pad = DN_CONV // 2
    has_prev = jnp.logical_and(i != 0, i != n_ctx_blk)
    has_next = jnp.logical_and(i != n_ctx_blk - 1, i != nblk - 1)
    ext_ref[0:halo, :] = jnp.where(has_prev, prev_ref[...], 0.0)
    ext_ref[halo:halo + tm, :] = z_ref[...]
    ext_ref[halo + tm:, :] = jnp.where(has_next, next_ref[...], 0.0)
    nq = DN_HEADS * DN_DK
    for j in range(DN_QKV // LANES):
        cs = slice(j * LANES, (j + 1) * LANES)
        acc = conv_ref[0:1, cs] * ext_ref[halo - pad:halo - pad + tm, cs]
        for t in range(1, DN_CONV):
            acc = acc + conv_ref[t:t + 1, cs] * ext_ref[halo - pad + t:halo - pad + t + tm, cs]
        y = _silu(acc)
        if j * LANES < 2 * nq:
            y = y * lax.rsqrt(jnp.sum(y * y, axis=-1, keepdims=True) + EPS)
        if j * LANES < nq:
            q_ref[:, cs] = (y * DN_DK ** -0.5).astype(q_ref.dtype)
        elif j * LANES < 2 * nq:
            k_ref[:, slice(j * LANES - nq, (j + 1) * LANES - nq)] = y.astype(k_ref.dtype)
        else:
            v_ref[:, slice(j * LANES - 2 * nq, (j + 1) * LANES - 2 * nq)] = y.astype(v_ref.dtype)
    ab = ab_ref[...]
    nh = 2 * DN_HEADS
    xa = ab[:, 0:nh] + dtb_ref[...]
    softplus = jnp.maximum(xa, 0.0) + jnp.log(1.0 + jnp.exp(-jnp.abs(xa)))
    gb_ref[:, 0:nh] = -jnp.exp(alog_ref[...]) * softplus
    gb_ref[:, nh:2 * nh] = jax.nn.sigmoid(ab[:, nh:2 * nh])


def _dn_prep(geo, n_ctx_blk, z, conv_w, alog, dtb):
    B, nblk, tm, _ = geo
    n = z.shape[0]
    halo = SUBLANES
    r = tm // halo
    nh8 = n // halo
    d = DN_HEADS * DN_DK
    prev = pl.BlockSpec((halo, DN_QKV), lambda b, i: (jnp.maximum((b * nblk + i) * r - 1, 0), 0))
    nxt = pl.BlockSpec((halo, DN_QKV), lambda b, i: (jnp.minimum((b * nblk + i + 1) * r, nh8 - 1), 0))
    ab = z[:, DN_QKV + d:DN_QKV + d + 4 * DN_HEADS]
    tok = pl.BlockSpec((tm, d), lambda b, i: (b * nblk + i, 0))
    gbspec = pl.BlockSpec((tm, 4 * DN_HEADS), lambda b, i: (b * nblk + i, 0))
    full = lambda a: pl.BlockSpec(a.shape, lambda b, i: (0,) * a.ndim)
    return pl.pallas_call(
        functools.partial(_dn_prep_kernel, n_ctx_blk=n_ctx_blk, nblk=nblk), grid=(B, nblk),
        in_specs=[pl.BlockSpec((tm, DN_QKV), lambda b, i: (b * nblk + i, 0)), prev, nxt, full(conv_w),
                  full(alog), full(dtb), gbspec],
        out_specs=[tok, tok, tok, gbspec],
        out_shape=[jax.ShapeDtypeStruct((n, d), F32)] * 3 + [jax.ShapeDtypeStruct((n, 4 * DN_HEADS), F32)],
        scratch_shapes=[pltpu.VMEM((tm + 2 * halo, DN_QKV), F32)],
        compiler_params=_cparams(("parallel", "parallel")))(z, z, z, conv_w, alog, dtb, ab)


DN_GROUP = 4


def _bdot(a, b, dims=None):
    a, b = a.astype(BF16), b.astype(BF16)
    if dims is None:
        return jnp.dot(a, b, preferred_element_type=F32)
    return lax.dot_general(a, b, dims, preferred_element_type=F32)


def _block_diag(x, nblk):
    r, n = x.shape
    w = n // nblk
    tall = jnp.concatenate([x] * nblk, axis=0)
    rb = lax.broadcasted_iota(jnp.int32, tall.shape, 0) // r
    lb = lax.broadcasted_iota(jnp.int32, tall.shape, 1) // w
    return jnp.where(rb == lb, tall, jnp.zeros_like(tall))


def _dn_local_kernel(q_ref, k_ref, v_ref, gb_ref, u_ref, wq_ref, akd_ref, gl_ref):
    C, G, nh = DN_CHUNK, DN_GROUP, DN_HEADS
    d_all = nh * LANES
    ri = lax.broadcasted_iota(jnp.int32, (C, G * C), 0)
    ci = lax.broadcasted_iota(jnp.int32, (C, G * C), 1) % C
    eyecat = (ri == ci).astype(F32)
    ones = jnp.ones((C, C), F32)
    r2 = lax.broadcasted_iota(jnp.int32, (C, C), 0)
    c2 = lax.broadcasted_iota(jnp.int32, (C, C), 1)
    eye128 = (lax.broadcasted_iota(jnp.int32, (LANES, LANES), 0)
              == lax.broadcasted_iota(jnp.int32, (LANES, LANES), 1)).astype(BF16)
    ncol = gb_ref.shape[1]
    sel_row = lax.broadcasted_iota(jnp.int32, (ncol, d_all), 0)
    sel_head = lax.broadcasted_iota(jnp.int32, (ncol, d_all), 1) // LANES
    cat_row = lax.broadcasted_iota(jnp.int32, (ncol, G * C), 0)
    cat_head = lax.broadcasted_iota(jnp.int32, (ncol, G * C), 1) // C
    hdot = lambda a, b: jnp.dot(a, b, precision=HI, preferred_element_type=F32)
    nblk = q_ref.shape[0] // C
    rows = lambda b: slice(b * C, (b + 1) * C)
    q, k, v, gb = ([ref[rows(b), :].astype(F32) for b in range(nblk)] for ref in (q_ref, k_ref, v_ref, gb_ref))
    dirs = [(b, d) for b in range(nblk) for d in (0, 1)]
    groups = [(b, d, g) for b, d in dirs for g in range(nh // G)]
    gsl = lambda g: slice(g * G * LANES, (g + 1) * G * LANES)
    csl = lambda g: slice(g * G * C, (g + 1) * G * C)
    incl = [(ri >= ci), (ri <= ci)]
    strict = [(ri > ci), (ri < ci)]
    last = [C - 1, 0]
    tri = [(r2 >= c2).astype(F32), (r2 <= c2).astype(F32)]
    gcs = {(b, d): hdot(tri[d], gb[b]) for b, d in dirs}
    spread = lambda m, first: hdot(m, (sel_row == first + sel_head).astype(F32))
    gc = {(b, d): spread(gcs[b, d], d * nh) for b, d in dirs}
    bet = {(b, d): spread(gb[b], 2 * nh + d * nh) for b, d in dirs}
    gcol = {(b, d, g): hdot(gcs[b, d], (cat_row == d * nh + g * G + cat_head).astype(F32)) for b, d, g in groups}
    grow = {key: hdot(ones, x * eyecat) for key, x in gcol.items()}
    kbm = {(b, d): k[b] * bet[b, d] for b, d in dirs}
    kq = {(b, d, g): _bdot(jnp.concatenate([kbm[b, d][:, gsl(g)], q[b][:, gsl(g)]], axis=0),
                           _block_diag(k[b][:, gsl(g)], G), NT) for b, d, g in groups}
    eg = {key: jnp.exp(x) for key, x in gc.items()}
    glrow = {(b, d): gc[b, d][last[d]:last[d] + 1, :] for b, d in dirs}
    kd = {(b, d): k[b] * jnp.exp(glrow[b, d] - gc[b, d]) for b, d in dirs}
    kdt = {(b, d, g): _bdot(eye128, jnp.concatenate(
        [kd[b, d][:, (g * G + j) * LANES:(g * G + j + 1) * LANES] for j in range(G)], axis=0), NT)
        for b, d, g in groups}
    decay = {(b, d, g): jnp.where(incl[d], jnp.exp(jnp.where(incl[d], gcol[b, d, g] - grow[b, d, g], 0.0)), 0.0)
             for b, d, g in groups}
    nm, vbg, kbeg = {}, {}, {}
    for b, d, g in groups:
        x, dec = kq[b, d, g], decay[b, d, g]
        akd_ref[d, b, 0:C, csl(g)] = jnp.where(incl[d], x[C:2 * C] * dec, 0.0).astype(BF16)
        akd_ref[d, b, C:C + DN_DK, csl(g)] = kdt[b, d, g].astype(BF16)
        nm[b, d, g] = jnp.where(strict[d], -x[0:C] * dec, 0.0)
        vbg[b, d, g] = (v[b] * bet[b, d])[:, gsl(g)]
        kbeg[b, d, g] = (kbm[b, d] * eg[b, d])[:, gsl(g)]
    for b, d in dirs:
        wq_ref[d, (2 * b + 1) * C:(2 * b + 2) * C, :] = (q[b] * eg[b, d]).astype(BF16)
        gl_ref[d, b] = jnp.exp(glrow[b, d])
    tinv = {key: eyecat + x for key, x in nm.items()}
    p = dict(nm)
    pbd = {key: _block_diag(x, G) for key, x in p.items()}
    for _ in range(int(math.log2(C)) - 1):
        p = {key: _bdot(x, pbd[key]) for key, x in p.items()}
        pbd = {key: _block_diag(x, G) for key, x in p.items()}
        tinv = {key: t + _bdot(t, pbd[key]) for key, t in tinv.items()}
    for b, d, g in groups:
        t = tinv[b, d, g]
        u_ref[d, rows(b), gsl(g)] = _bdot(t, _block_diag(vbg[b, d, g], G))
        wq_ref[d, 2 * b * C:(2 * b + 1) * C, gsl(g)] = _bdot(t, _block_diag(kbeg[b, d, g], G)).astype(BF16)


def _dn_local(q, k, v, gb):
    n, d = q.shape
    C = DN_CHUNK
    nc = n // C
    nb = next(c for c in (4, 2, 1) if nc % c == 0)
    tok = pl.BlockSpec((nb * C, d), lambda j: (j, 0))
    return pl.pallas_call(
        _dn_local_kernel, grid=(nc // nb,),
        in_specs=[tok, tok, tok, pl.BlockSpec((nb * C, gb.shape[1]), lambda j: (j, 0))],
        out_specs=[pl.BlockSpec((2, nb * C, d), lambda j: (0, j, 0)),
                   pl.BlockSpec((2, nb * 2 * C, d), lambda j: (0, j, 0)),
                   pl.BlockSpec((2, nb, C + DN_DK, DN_HEADS * C), lambda j: (0, j, 0, 0)),
                   pl.BlockSpec((2, nb, 1, d), lambda j: (0, j, 0, 0))],
        out_shape=[jax.ShapeDtypeStruct((2, n, d), F32),
                   jax.ShapeDtypeStruct((2, 2 * n, d), BF16),
                   jax.ShapeDtypeStruct((2, nc, C + DN_DK, DN_HEADS * C), BF16),
                   jax.ShapeDtypeStruct((2, nc, 1, d), F32)],
        compiler_params=_cparams(("parallel",)))(q, k, v, gb)


def _dn_scan_kernel(*refs):
    ins, (of_ref, or_ref, s_ref) = refs[:8], refs[8:]
    C = DN_CHUNK

    @pl.when(pl.program_id(1) == 0)
    def _():
        s_ref[...] = jnp.zeros_like(s_ref)

    chains = [(d, p) for d in range(2) for p in range(DN_HEADS // 2)]
    outs = (of_ref, or_ref)
    lanes = lambda p: slice(2 * p * LANES, (2 * p + 2) * LANES)
    s2 = [jnp.concatenate([s_ref[d, 2 * p], s_ref[d, 2 * p + 1]], axis=-1) for d, p in chains]
    r = [_bdot(ins[4 * d + 1][0, :, lanes(p)], _block_diag(s, 2)) for (d, p), s in zip(chains, s2)]
    vn = [ins[4 * d][0, :, lanes(p)] - x[0:C] for (d, p), x in zip(chains, r)]
    r2 = [_bdot(ins[4 * d + 2][0, 0, :, 2 * p * C:(2 * p + 2) * C], _block_diag(x, 2))
          for (d, p), x in zip(chains, vn)]
    for (d, p), s, x, y in zip(chains, s2, r, r2):
        outs[d][:, lanes(p)] = x[C:2 * C] + y[0:C]
        snew = s * ins[4 * d + 3][0, 0, :, lanes(p)] + y[C:C + DN_DK]
        s_ref[d, 2 * p] = snew[:, 0:LANES]
        s_ref[d, 2 * p + 1] = snew[:, LANES:2 * LANES]


def _dn_scan(B, T, ctx_len, u, wq, akd, gl):
    _, n, d = u.shape
    C = DN_CHUNK
    nch, ncc = T // C, ctx_len // C
    rpos = lambda c: jnp.where(c < ncc, ncc - 1 - c, nch - 1 - (c - ncc))
    pos = (lambda b, c: b * nch + c, lambda b, c: b * nch + rpos(c))
    in_specs, args = [], []
    for dd in range(2):
        p = pos[dd]
        in_specs += [pl.BlockSpec((1, C, d), lambda b, c, p=p, dd=dd: (dd, p(b, c), 0)),
                     pl.BlockSpec((1, 2 * C, d), lambda b, c, p=p, dd=dd: (dd, p(b, c), 0)),
                     pl.BlockSpec((1, 1, C + DN_DK, DN_HEADS * C), lambda b, c, p=p, dd=dd: (dd, p(b, c), 0, 0)),
                     pl.BlockSpec((1, 1, 1, d), lambda b, c, p=p, dd=dd: (dd, p(b, c), 0, 0))]
        args += [u, wq, akd, gl]
    return pl.pallas_call(
        _dn_scan_kernel, grid=(B, nch), in_specs=in_specs,
        out_specs=[pl.BlockSpec((C, d), lambda b, c: (pos[0](b, c), 0)),
                   pl.BlockSpec((C, d), lambda b, c: (pos[1](b, c), 0))],
        out_shape=[jax.ShapeDtypeStruct((n, d), F32)] * 2,
        scratch_shapes=[pltpu.VMEM((2, DN_HEADS, DN_DK, LANES), F32)],
        compiler_params=_cparams(("arbitrary", "arbitrary")))(*args)


def _final_kernel(x_ref, y_ref, mod_ref, o_ref):
    o_ref[0] = x_ref[...] + mod_ref[0, 5:6, :] * y_ref[...]


def _final(B, T, ctx_len, x, y, mod):
    n, d = x.shape
    tm = TOKEN_BLOCK
    nblk, ncb = T // tm, ctx_len // tm
    tok = pl.BlockSpec((tm, d), lambda b, i: (b * nblk + ncb + i, 0))
    ytok = pl.BlockSpec((tm, d), lambda b, i: (b * (nblk - ncb) + i, 0))
    return pl.pallas_call(
        _final_kernel, grid=(B, nblk - ncb),
        in_specs=[tok, ytok, pl.BlockSpec((1, 6, d), lambda b, i: (b, 0, 0))],
        out_specs=pl.BlockSpec((1, tm, d), lambda b, i: (b, i, 0)),
        out_shape=jax.ShapeDtypeStruct((B, T - ctx_len, d), F32),
        compiler_params=_cparams(("parallel", "parallel")))(x, y, mod)


def _rope_tables(seq, ctx_len):
    rows = seq // GRID_W
    r = jnp.broadcast_to(jnp.arange(rows, dtype=F32)[:, None], (rows, GRID_W)).reshape(-1)
    cl = jnp.broadcast_to(jnp.arange(GRID_W, dtype=F32)[None, :], (rows, GRID_W)).reshape(-1)

    def angles(rot_dim):
        nf = rot_dim // 4
        inv = ROPE_BASE ** (-jnp.arange(nf, dtype=F32) / nf)
        ang = jnp.concatenate([r[:, None] * inv, cl[:, None] * inv], axis=-1)
        ang = jnp.concatenate([jnp.zeros((ctx_len, rot_dim // 2), F32), ang], axis=0)
        return jnp.cos(ang), jnp.sin(ang)

    T = seq + ctx_len
    ca, sa = angles(DIFF_DIM)
    z = jnp.zeros_like(sa)
    ta = jnp.stack([jnp.concatenate([ca, ca] * 2, -1), jnp.concatenate([-sa, z] * 2, -1),
                    jnp.concatenate([z, sa] * 2, -1)])
    cb, sb = angles(MLA_ROPE)
    one, zn, zt = jnp.ones((T, MLA_NOPE), F32), jnp.zeros((T, MLA_NOPE), F32), jnp.zeros((T, LANES - MLA_QK), F32)
    zb = jnp.zeros_like(sb)
    tb = jnp.stack([jnp.concatenate([one, cb, cb, 1.0 + zt], -1), jnp.concatenate([zn, -sb, zb, zt], -1),
                    jnp.concatenate([zn, zb, sb, zt], -1)])
    return ta, tb


def _pad_heads(w, heads, width):
    lead = w.shape[:-1]
    w = w.reshape(lead + (heads, width))
    return jnp.pad(w, [(0, 0)] * len(lead) + [(0, 0), (0, LANES - width)]).reshape(lead + (heads * LANES,))


def _block_diag_ones(n, group):
    idx = np.arange(n) // group
    return jnp.asarray(idx[:, None] == idx[None, :], dtype=BF16)


def kernel(x, c, ctx, c_ctx, ada_w, ada_b, ev_w_in, ev_w_out, diff_q_norm, diff_k_norm, diff_lam_q1, diff_lam_k1, diff_lam_q2, diff_lam_k2, diff_subln, mla_cq_norm, mla_ckv_norm, mla_w_uq, mla_w_ukv, mla_q_norm, mla_k_norm, dn_w_in, dn_conv, dn_a_log, dn_dt_bias, dn_o_norm, dn_w_out, peer_wq, peer_k1, peer_k2, peer_u, peer_v):
    B, S, D = x.shape
    CTX = ctx.shape[1]
    T = CTX + S
    depth = ada_w.shape[0]
    tm = TOKEN_BLOCK
    assert CTX % tm == 0 and S % tm == 0 and B + 1 <= SUBLANES
    assert (B * T) % PEER_TOKEN_BLOCK == 0 and (B * S) % PEER_TOKEN_BLOCK == 0
    nblk, ncb = T // tm, CTX // tm
    geo = (B, nblk, tm, lambda b, i: jnp.where(i < ncb, B, b))

    cc = jnp.zeros((SUBLANES, D), F32).at[:B].set(c).at[B].set(c_ctx)
    mods = _adaln(cc, ada_w, ada_b).reshape(depth, SUBLANES, 6, D)
    xs = jnp.concatenate([ctx, x], axis=1).reshape(B * T, D)
    ropea, ropeb = _rope_tables(S, CTX)
    bd64, bd128 = _block_diag_ones(HEAD_W, DIFF_DIM), _block_diag_ones(HEAD_W, LANES)

    y = None
    for l in range(depth):
        i = l // 2
        mod = mods[l]
        gmod = mods[l - 1] if l else None
        n_skip = ncb if l == depth - 1 else 0
        if l % 2 == 0:
            lam_init = 0.8 - 0.6 * math.exp(-0.3 * l)
            w_in = ev_w_in[i]
            o = 3 * HEAD_W + MLA_Q_RANK + MLA_KV_RANK
            kr_rep = jnp.pad(jnp.broadcast_to(w_in[:, None, o:o + MLA_ROPE], (D, MLA_HEADS, MLA_ROPE)),
                             ((0, 0), (0, 0), (MLA_NOPE, LANES - MLA_QK))).reshape(D, HEAD_W)
            w_cat = jnp.concatenate([w_in[:, :o], kr_rep], axis=1).astype(BF16)
            res = _modmm(geo, xs, mod, w_cat, y, gmod)
            (xs, z) = res if y is not None else (xs, res)
            ukv = mla_w_ukv[i].reshape(MLA_KV_RANK, MLA_HEADS, MLA_NOPE + MLA_VDIM)
            gains = jnp.stack([jnp.tile(diff_q_norm[i], HEAD_W // DIFF_DIM), jnp.tile(diff_k_norm[i], HEAD_W // DIFF_DIM),
                               jnp.tile(jnp.pad(mla_q_norm[i], (0, LANES - MLA_QK)), MLA_HEADS),
                               jnp.tile(jnp.pad(mla_k_norm[i], (0, LANES - MLA_QK)), MLA_HEADS)])
            qa, ka, va, qb, kb, vb = _even_prep(
                geo, z, ropea, ropeb, gains, mla_cq_norm[i][None, :], mla_ckv_norm[i][None, :],
                _pad_heads(mla_w_uq[i], MLA_HEADS, MLA_QK).astype(BF16),
                _pad_heads(ukv[:, :, :MLA_NOPE].reshape(MLA_KV_RANK, -1), MLA_HEADS, MLA_NOPE).astype(BF16),
                ukv[:, :, MLA_NOPE:].reshape(MLA_KV_RANK, -1).astype(BF16), bd64, bd128)
            lam_vecs = jnp.stack([diff_lam_q1[i], diff_lam_k1[i], diff_lam_q2[i], diff_lam_k2[i]])
            oa = _attention(B, T, CTX, qa, ka, va, lam_vecs, diff_subln[i][None, :], lam_init)
            ob = _attention(B, T, CTX, qb, kb, vb)
            xs, hqt = _mixer_out(geo, _even_out_kernel, [(oa, HEAD_W, 0), (ob, HEAD_W, 0)], [], xs, mod,
                                 ev_w_out[i].astype(BF16), n_skip)
        else:
            nin = dn_w_in.shape[2]
            w_in = jnp.pad(dn_w_in[i], ((0, 0), (0, -nin % LANES))).astype(BF16)
            res = _modmm(geo, xs, mod, w_in, y, gmod)
            (xs, z) = res if y is not None else (xs, res)
            q, k, v, gb = _dn_prep(geo, ncb, z, dn_conv[i], dn_a_log[i].reshape(1, -1),
                                   dn_dt_bias[i].reshape(1, -1))
            of, orv = _dn_scan(B, T, CTX, *_dn_local(q, k, v, gb))
            xs, hqt = _mixer_out(geo, _dn_out_kernel,
                                 [(of, D, 0), (orv, D, 0), (z, D, DN_QKV // D)], [dn_o_norm[i][None, :]], xs, mod,
                                 dn_w_out[i].astype(BF16), n_skip)
        y = _peer(hqt, B * (nblk - n_skip) * tm, peer_wq[l], peer_k1[l], peer_k2[l], peer_u[l], peer_v[l])
    return _final(B, T, CTX, xs, y, mods[depth - 1])
```
